```python
import jax, jax.numpy as jnp
from jax import lax
import numpy as np

D_MODEL = 1024
BATCH = 16
SEQ = 2048
DEPTH = 2

HEAD_DIM = 64
D_MIX = D_MODEL
RET_HEADS = 4
GMLP_GROUPS = 4
NSA_HEADS = 8
NSA_KV_HEADS = 2
NSA_GROUP = NSA_HEADS // NSA_KV_HEADS
RET_WIDTH = RET_HEADS * HEAD_DIM
GMLP_WIDTH = GMLP_GROUPS * HEAD_DIM
NSA_WIDTH = NSA_HEADS * HEAD_DIM
KV_WIDTH = NSA_KV_HEADS * HEAD_DIM
N_BRANCH = 3
RET_CHUNK = 128
GMLP_CHUNK = 128
CMP_LEN = 32
CMP_STRIDE = 16
CMP_HIDDEN = 128
SEL_BLOCK = 64
SEL_TOPK = 8
WINDOW = 512
NSA_Q_BLOCK = 64
ROPE_THETA = 10000.0
N_EXPERTS = 16
N_EXPERT_GROUPS = 4
EXPERTS_PER_GROUP = N_EXPERTS // N_EXPERT_GROUPS
EXPERT_TOPK = 2
D_EXPERT = 512
NORM_EPS = 1e-6
NEG_INF = -1e30
FORCE_SCORE = 1e4
IN_SPLITS = (RET_WIDTH, RET_WIDTH, RET_WIDTH, RET_WIDTH, GMLP_WIDTH, GMLP_WIDTH, NSA_WIDTH,
             KV_WIDTH, KV_WIDTH, KV_WIDTH, KV_WIDTH, KV_WIDTH, KV_WIDTH, NSA_HEADS * N_BRANCH)
N_IN = sum(IN_SPLITS)

kernel_name = "hybrid_retention_gmlp_nsa_grouped_moe"


def rms_norm(x, g):
    xf = x.astype(jnp.float32)
    y = xf * lax.rsqrt(jnp.mean(xf * xf, axis=-1, keepdims=True) + NORM_EPS)
    return (y * g.astype(jnp.float32)).astype(x.dtype)


def layer_norm(x, g):
    xf = x.astype(jnp.float32)
    mu = jnp.mean(xf, axis=-1, keepdims=True)
    var = jnp.mean(jnp.square(xf - mu), axis=-1, keepdims=True)
    return ((xf - mu) * lax.rsqrt(var + NORM_EPS) * g.astype(jnp.float32)).astype(x.dtype)


def rope(x, pos):
    half = x.shape[-1] // 2
    inv_freq = jnp.power(ROPE_THETA, -jnp.arange(half, dtype=jnp.float32) / half)
    ang = pos.astype(jnp.float32)[..., None] * inv_freq
    cos = jnp.cos(ang)[:, :, None, :]
    sin = jnp.sin(ang)[:, :, None, :]
    xf = x.astype(jnp.float32)
    x1, x2 = xf[..., :half], xf[..., half:]
    return jnp.concatenate([x1 * cos - x2 * sin, x1 * sin + x2 * cos], axis=-1).astype(x.dtype)


def retention(q, k, v, g, pos, norm_g):
    B, S, H, d = q.shape
    C = RET_CHUNK
    nC = S // C
    dt = q.dtype
    q = rope(q, pos)
    k = rope(k, pos) * (d ** -0.5)
    log_gamma = jnp.log1p(-jnp.power(2.0, -5.0 - jnp.arange(H, dtype=jnp.float32)))
    idx = jnp.arange(C, dtype=jnp.float32)
    diff = idx[:, None] - idx[None, :]
    decay_in = jnp.where(diff >= 0, jnp.exp(jnp.maximum(diff, 0.0)[None] * log_gamma[:, None, None]), 0.0).astype(dt)
    decay_q = jnp.exp((idx + 1.0)[:, None] * log_gamma[None]).astype(dt)
    decay_k = jnp.exp((C - 1.0 - idx)[:, None] * log_gamma[None]).astype(dt)
    decay_chunk = jnp.exp(C * log_gamma).astype(dt)
    qc = q.reshape(B, nC, C, H, d).swapaxes(0, 1)
    kc = k.reshape(B, nC, C, H, d).swapaxes(0, 1)
    vc = v.reshape(B, nC, C, H, d).swapaxes(0, 1)

    def step(state, qkv):
        qb, kb, vb = qkv
        inner = jnp.einsum('bnhd,bmhd->bhnm', qb, kb) * decay_in
        o = jnp.einsum('bhnm,bmhe->bnhe', inner, vb)
        o = o + jnp.einsum('bnhd,bhde->bnhe', qb, state) * decay_q[None, :, :, None]
        state = state * decay_chunk[None, :, None, None] + jnp.einsum('bmhd,bmhe->bhde', kb * decay_k[None, :, :, None], vb)
        return state, o

    state0 = jnp.zeros((B, H, d, d), dt)
    _, o = lax.scan(step, state0, (qc, kc, vc))
    o = rms_norm(o.swapaxes(0, 1).reshape(B, S, H, d), norm_g)
    return o.reshape(B, S, H * d) * jax.nn.silu(g)


def chunked_spatial_gating(u, v, ln_g, w_s, b_s):
    B, S, _ = u.shape
    nC = S // GMLP_CHUNK
    u = jax.nn.gelu(u)
    v = layer_norm(jax.nn.gelu(v), ln_g)
    vc = v.reshape(B, nC, GMLP_CHUNK, GMLP_GROUPS, HEAD_DIM)
    causal = jnp.tril(jnp.ones((GMLP_CHUNK, GMLP_CHUNK), dtype=bool))
    w = jnp.where(causal[None], w_s, 0.0)
    s = jnp.einsum('gts,bnsgc->bntgc', w, vc) + b_s.T[None, None, :, :, None]
    return u * s.reshape(B, S, GMLP_WIDTH)


def compress_blocks(kv, pe, w1, w2):
    B, S, G, d = kv.shape
    chunks = kv.reshape(B, S // CMP_STRIDE, CMP_STRIDE, G, d)
    blocks = jnp.concatenate([chunks[:, :-1], chunks[:, 1:]], axis=2)
    h = jax.nn.gelu(jnp.einsum('bnlgd,ldh->bngh', blocks + pe[None, None, :, None, :], w1))
    return jnp.einsum('bngh,hd->bngd', h, w2)


def masked_softmax(s, mask):
    p = jax.nn.softmax(jnp.where(mask, s.astype(jnp.float32), NEG_INF), axis=-1)
    return p * mask


def native_sparse_attention(q, kc_raw, vc_raw, ks, vs, kw, vw, gate_logits, pos,
                            q_norm_g, k_norm_g, pe_k, pe_v, w1_k, w2_k, w1_v, w2_v):
    B, S, H, d = q.shape
    G, R = NSA_KV_HEADS, NSA_GROUP
    dt = q.dtype
    scale = d ** -0.5
    q = rope(rms_norm(q, q_norm_g), pos)
    kc = compress_blocks(kc_raw, pe_k, w1_k, w2_k)
    vc = compress_blocks(vc_raw, pe_v, w1_v, w2_v)
    kc = rope(rms_norm(kc, k_norm_g[0]), pos[:, CMP_LEN - 1::CMP_STRIDE])
    ks = rope(rms_norm(ks, k_norm_g[1]), pos)
    kw = rope(rms_norm(kw, k_norm_g[2]), pos)
    n_cmp = kc.shape[1]
    n_sel = S // SEL_BLOCK
    top_n = min(SEL_TOPK, n_sel)
    cmp_start = jnp.arange(n_cmp) * CMP_STRIDE
    cmp_end = cmp_start + CMP_LEN - 1
    sel_start = jnp.arange(n_sel) * SEL_BLOCK
    overlap = ((cmp_start[:, None] < sel_start[None] + SEL_BLOCK) & (cmp_start[:, None] + CMP_LEN > sel_start[None])).astype(jnp.float32)
    ks_blk = ks.reshape(B, n_sel, SEL_BLOCK, G, d).transpose(0, 3, 1, 2, 4)
    vs_blk = vs.reshape(B, n_sel, SEL_BLOCK, G, d).transpose(0, 3, 1, 2, 4)
    kw_pad = jnp.pad(kw, ((0, 0), (WINDOW, 0), (0, 0), (0, 0)))
    vw_pad = jnp.pad(vw, ((0, 0), (WINDOW, 0), (0, 0), (0, 0)))
    gates = jax.nn.sigmoid(gate_logits.reshape(B, S, G, R, N_BRANCH))
    n_blk = S // NSA_Q_BLOCK
    q_blocks = q.reshape(B, n_blk, NSA_Q_BLOCK, G, R, d).swapaxes(0, 1)
    g_blocks = gates.reshape(B, n_blk, NSA_Q_BLOCK, G, R, N_BRANCH).swapaxes(0, 1)
    b_ix = jnp.arange(B)[:, None, None, None]
    g_ix = jnp.arange(G)[None, :, None, None]
    sel_off = jnp.arange(SEL_BLOCK)
    win_off = jnp.arange(NSA_Q_BLOCK + WINDOW) - WINDOW
    j = jnp.arange(n_sel)

    def block(args):
        i, qb, gb = args
        t = i * NSA_Q_BLOCK + jnp.arange(NSA_Q_BLOCK)
        s_c = jnp.einsum('bqgrd,bngd->bgrqn', qb, kc) * scale
        p_c = masked_softmax(s_c, cmp_end[None, :] <= t[:, None])
        o_c = jnp.einsum('bgrqn,bngd->bqgrd', p_c.astype(dt), vc)
        imp = jnp.einsum('bgrqn,nj->bgqj', p_c, overlap)
        cur = t // SEL_BLOCK
        forced = (j[None] == 0) | (j[None] == cur[:, None]) | (j[None] == cur[:, None] - 1)
        allowed = sel_start[None] <= t[:, None]
        score = jnp.where(forced, FORCE_SCORE, jnp.where(allowed, imp, -1.0))
        _, idx = lax.top_k(score, top_n)
        k_g = ks_blk[b_ix, g_ix, idx]
        v_g = vs_blk[b_ix, g_ix, idx]
        tok = idx[..., None] * SEL_BLOCK + sel_off
        m_s = (tok <= t[None, None, :, None, None]).reshape(B, G, 1, NSA_Q_BLOCK, top_n * SEL_BLOCK)
        s_s = jnp.einsum('bqgrd,bgqkld->bgrqkl', qb, k_g).reshape(B, G, R, NSA_Q_BLOCK, top_n * SEL_BLOCK) * scale
        p_s = masked_softmax(s_s, m_s).reshape(B, G, R, NSA_Q_BLOCK, top_n, SEL_BLOCK)
        o_s = jnp.einsum('bgrqkl,bgqkld->bqgrd', p_s.astype(dt), v_g)
        start = i * NSA_Q_BLOCK
        k_w = lax.dynamic_slice_in_dim(kw_pad, start, NSA_Q_BLOCK + WINDOW, axis=1)
        v_w = lax.dynamic_slice_in_dim(vw_pad, start, NSA_Q_BLOCK + WINDOW, axis=1)
        spos = start + win_off
        m_w = (spos[None] <= t[:, None]) & (t[:, None] - spos[None] < WINDOW) & (spos[None] >= 0)
        s_w = jnp.einsum('bqgrd,bsgd->bgrqs', qb, k_w) * scale
        p_w = masked_softmax(s_w, m_w)
        o_w = jnp.einsum('bgrqs,bsgd->bqgrd', p_w.astype(dt), v_w)
        return gb[..., 0:1] * o_c + gb[..., 1:2] * o_s + gb[..., 2:3] * o_w

    out = lax.map(block, (jnp.arange(n_blk), q_blocks, g_blocks))
    return out.swapaxes(0, 1).reshape(B, S, H * d)


def grouped_moe(h, router_w, router_b, w_gate, w_up, w_down):
    B, S, D = h.shape
    probs = jax.nn.softmax((h @ router_w).astype(jnp.float32), axis=-1)
    sel = (probs + router_b.astype(jnp.float32)).reshape(B, S, N_EXPERT_GROUPS, EXPERTS_PER_GROUP)
    group_score = jnp.sum(lax.top_k(sel, EXPERT_TOPK)[0], axis=-1)
    best_group = jnp.argmax(group_score, axis=-1)
    in_group = (best_group[..., None] == jnp.arange(N_EXPERT_GROUPS))[..., None]
    sel = jnp.where(in_group, sel, NEG_INF).reshape(B, S, N_EXPERTS)
    _, top_idx = lax.top_k(sel, EXPERT_TOPK)
    top_w = jnp.take_along_axis(probs, top_idx, axis=-1)
    top_w = top_w / jnp.sum(top_w, axis=-1, keepdims=True)
    gates = jnp.einsum('bske,bsk->bse', jax.nn.one_hot(top_idx, N_EXPERTS, dtype=jnp.float32), top_w).astype(h.dtype)
    out = jnp.zeros_like(h)
    for e in range(N_EXPERTS):
        a = jax.nn.silu(h @ w_gate[e]) * (h @ w_up[e])
        out = out + (gates[..., e:e + 1] * a) @ w_down[e]
    return out


def setup_inputs(seed: int = 0) -> dict:
    key = jax.random.key(seed)
    ks = jax.random.split(key, 32)
    f32 = jnp.float32
    nrm = lambda k, shape, s: jax.random.normal(k, shape, f32) * s
    L = DEPTH
    return {
        "x": nrm(ks[0], (BATCH, SEQ, D_MODEL), 1.0),
        "c": nrm(ks[1], (BATCH, D_MODEL), 1.0),
        "positions": (jnp.arange(SEQ, dtype=jnp.int32)[None, :] + jax.random.randint(ks[2], (BATCH, 1), 0, 1024, dtype=jnp.int32)),
        "ada_w": nrm(ks[3], (L, D_MODEL, 6 * D_MODEL), 0.5 * D_MODEL ** -0.5),
        "ada_b": nrm(ks[4], (L, 6 * D_MODEL), 0.02),
        "norm_mix_g": 1.0 + nrm(ks[5], (L, D_MODEL), 0.05),
        "norm_ffn_g": 1.0 + nrm(ks[6], (L, D_MODEL), 0.05),
        "w_in": nrm(ks[7], (L, D_MODEL, N_IN), D_MODEL ** -0.5),
        "w_out": nrm(ks[8], (L, D_MIX, D_MODEL), D_MIX ** -0.5),
        "ret_norm_g": 1.0 + nrm(ks[9], (L, RET_HEADS, HEAD_DIM), 0.05),
        "gmlp_ln_g": 1.0 + nrm(ks[10], (L, GMLP_WIDTH), 0.05),
        "gmlp_ws": nrm(ks[11], (L, GMLP_GROUPS, GMLP_CHUNK, GMLP_CHUNK), GMLP_CHUNK ** -0.5),
        "gmlp_b": 1.0 + nrm(ks[12], (L, GMLP_GROUPS, GMLP_CHUNK), 0.05),
        "nsa_q_norm_g": 1.0 + nrm(ks[13], (L, HEAD_DIM), 0.05),
        "nsa_k_norm_g": 1.0 + nrm(ks[14], (L, N_BRANCH, HEAD_DIM), 0.05),
        "cmp_pe_k": nrm(ks[15], (L, CMP_LEN, HEAD_DIM), 0.1),
        "cmp_pe_v": nrm(ks[16], (L, CMP_LEN, HEAD_DIM), 0.1),
        "cmp_w1_k": nrm(ks[17], (L, CMP_LEN, HEAD_DIM, CMP_HIDDEN), (CMP_LEN * HEAD_DIM) ** -0.5),
        "cmp_w2_k": nrm(ks[18], (L, CMP_HIDDEN, HEAD_DIM), CMP_HIDDEN ** -0.5),
        "cmp_w1_v": nrm(ks[19], (L, CMP_LEN, HEAD_DIM, CMP_HIDDEN), (CMP_LEN * HEAD_DIM) ** -0.5),
        "cmp_w2_v": nrm(ks[20], (L, CMP_HIDDEN, HEAD_DIM), CMP_HIDDEN ** -0.5),
        "router_w": nrm(ks[21], (D_MODEL, N_EXPERTS), D_MODEL ** -0.5),
        "router_b": nrm(ks[22], (N_EXPERTS,), 0.01),
        "moe_w_gate": nrm(ks[23], (L, N_EXPERTS, D_MODEL, D_EXPERT), D_MODEL ** -0.5),
        "moe_w_up": nrm(ks[24], (L, N_EXPERTS, D_MODEL, D_EXPERT), D_MODEL ** -0.5),
        "moe_w_down": nrm(ks[25], (L, N_EXPERTS, D_EXPERT, D_MODEL), D_EXPERT ** -0.5),
    }


def reference(x, c, positions, ada_w, ada_b, norm_mix_g, norm_ffn_g, w_in, w_out,
              ret_norm_g, gmlp_ln_g, gmlp_ws, gmlp_b, nsa_q_norm_g, nsa_k_norm_g,
              cmp_pe_k, cmp_pe_v, cmp_w1_k, cmp_w2_k, cmp_w1_v, cmp_w2_v,
              router_w, router_b, moe_w_gate, moe_w_up, moe_w_down):
    B, S, _ = x.shape
    offsets = [int(o) for o in np.cumsum(IN_SPLITS)[:-1]]
    for l in range(DEPTH):
        mod = jax.nn.silu(c) @ ada_w[l] + ada_b[l]
        sh1, sc1, g1, sh2, sc2, g2 = [m[:, None, :] for m in jnp.split(mod, 6, axis=-1)]
        h = rms_norm(x, norm_mix_g[l]) * (1.0 + sc1) + sh1
        z = h @ w_in[l]
        rq, rk, rv, rg, gu, gv, nq, kc, vc, kse, vse, kwi, vwi, gl = jnp.split(z, offsets, axis=-1)
        head = lambda a, n: a.reshape(B, S, n, HEAD_DIM)
        y_ret = retention(head(rq, RET_HEADS), head(rk, RET_HEADS), head(rv, RET_HEADS), rg, positions, ret_norm_g[l])
        y_gmlp = chunked_spatial_gating(gu, gv, gmlp_ln_g[l], gmlp_ws[l], gmlp_b[l])
        y_nsa = native_sparse_attention(
            head(nq, NSA_HEADS), head(kc, NSA_KV_HEADS), head(vc, NSA_KV_HEADS),
            head(kse, NSA_KV_HEADS), head(vse, NSA_KV_HEADS), head(kwi, NSA_KV_HEADS), head(vwi, NSA_KV_HEADS),
            gl, positions, nsa_q_norm_g[l], nsa_k_norm_g[l],
            cmp_pe_k[l], cmp_pe_v[l], cmp_w1_k[l], cmp_w2_k[l], cmp_w1_v[l], cmp_w2_v[l])
        y = jnp.concatenate([y_ret, y_gmlp, y_nsa], axis=-1) @ w_out[l]
        x = x + g1 * y
        h2 = rms_norm(x, norm_ffn_g[l]) * (1.0 + sc2) + sh2
        x = x + g2 * grouped_moe(h2, router_w, router_b, moe_w_gate[l], moe_w_up[l], moe_w_down[l])
    return x
```

```python
import functools

import numpy as np
import jax
import jax.numpy as jnp
from jax import lax
from jax.experimental import pallas as pl
from jax.experimental.pallas import tpu as pltpu

F32 = jnp.float32
BF16 = jnp.bfloat16

HEAD_DIM = 64
LANES = 128
RET_HEADS = 4
GMLP_GROUPS = 4
NSA_HEADS = 8
NSA_KV_HEADS = 2
NSA_GROUP = NSA_HEADS // NSA_KV_HEADS
N_BRANCH = 3
CHUNK = 128
CMP_LEN = 32
CMP_STRIDE = 16
SEL_BLOCK = 64
SEL_TOPK = 8
WINDOW = 512
ROPE_THETA = 10000.0
N_EXPERT_GROUPS = 4
EXPERT_TOPK = 2
NORM_EPS = 1e-6
NEG_INF = -1e30
FORCE_SCORE = 1e4

RET_W = RET_HEADS * HEAD_DIM
GMLP_W = GMLP_GROUPS * HEAD_DIM
NSA_W = NSA_HEADS * HEAD_DIM
KV_W = NSA_KV_HEADS * HEAD_DIM
GATE_W = NSA_HEADS * N_BRANCH
OFF_RET = 0
OFF_GMLP = 4 * RET_W
OFF_Q = OFF_GMLP + 2 * GMLP_W
OFF_KV = OFF_Q + NSA_W
OFF_GL = OFF_KV + 6 * KV_W
N_IN = OFF_GL + GATE_W
N_IN_PAD = OFF_GL + LANES

Q_TILE = 128
KEY_CHUNK = 512
VMEM_LIMIT = 48 * 1024 * 1024


def _cparams(*sem):
    return pltpu.CompilerParams(dimension_semantics=sem, vmem_limit_bytes=VMEM_LIMIT)


def _dot(a, b):
    return jnp.dot(a.astype(BF16), b.astype(BF16), preferred_element_type=F32)


def _dot_nt(a, b):
    return lax.dot_general(a.astype(BF16), b.astype(BF16), (((1,), (1,)), ((), ())),
                           preferred_element_type=F32)


def _split(a):
    hi = a.astype(BF16)
    lo = (a - hi.astype(F32)).astype(BF16)
    return hi, lo


def _dot_split_lhs(a, b):
    hi, lo = _split(a)
    return (jnp.dot(hi, b, preferred_element_type=F32) + jnp.dot(lo, b, preferred_element_type=F32))


def _dot_split_rhs(a, b):
    hi, lo = _split(b)
    return (jnp.dot(a, hi, preferred_element_type=F32) + jnp.dot(a, lo, preferred_element_type=F32))


def _head_ones(width=LANES):
    r = lax.broadcasted_iota(jnp.int32, (width, width), 0) // HEAD_DIM
    c = lax.broadcasted_iota(jnp.int32, (width, width), 1) // HEAD_DIM
    return jnp.where(r == c, 1.0, 0.0).astype(BF16)


def _rotate_half(y):
    lane = lax.broadcasted_iota(jnp.int32, y.shape, 1)
    first = (lane & (HEAD_DIM - 1)) < (HEAD_DIM // 2)
    return jnp.where(first, -pltpu.roll(y, LANES - HEAD_DIM // 2, 1), pltpu.roll(y, HEAD_DIM // 2, 1))


def _rope_slab(y, cos, sin):
    return y * cos + _rotate_half(y) * sin


def _rope(x, cos, sin):
    return jnp.concatenate(
        [_rope_slab(x[:, k * LANES:(k + 1) * LANES], cos, sin) for k in range(x.shape[1] // LANES)], axis=1)


def _head_rms_rope(x, gain, cos, sin, ones):
    outs = []
    for k in range(x.shape[1] // LANES):
        xs = x[:, k * LANES:(k + 1) * LANES]
        ssq = _dot_split_lhs(xs * xs, ones)
        y = xs * lax.rsqrt(ssq * (1.0 / HEAD_DIM) + NORM_EPS) * gain[:, k * LANES:(k + 1) * LANES]
        outs.append(_rope_slab(y, cos, sin))
    return outs[0] if len(outs) == 1 else jnp.concatenate(outs, axis=1)


def _rms_mod(x, gain, scale, shift):
    y = x * lax.rsqrt(jnp.mean(x * x, axis=-1, keepdims=True) + NORM_EPS)
    return (y * gain) * (1.0 + scale) + shift


def _softmax_rows(s, valid):
    sm = jnp.where(valid, s, NEG_INF)
    e = jnp.exp(sm - jnp.max(sm, axis=-1, keepdims=True))
    return jnp.where(valid, e / jnp.sum(e, axis=-1, keepdims=True), 0.0)


def _stack_heads(q):
    return jnp.concatenate([q[:, r * HEAD_DIM:(r + 1) * HEAD_DIM] for r in range(NSA_GROUP)], axis=0)


def _unstack_heads(o, q_rows):
    return jnp.concatenate([o[r * q_rows:(r + 1) * q_rows] for r in range(NSA_GROUP)], axis=1)


def _rope_table_kernel(pos_ref, invf_ref, cos_ref, sin_ref):
    ang = pos_ref[0].astype(F32) * invf_ref[...]
    cos_ref[0] = jnp.cos(ang)
    sin_ref[0] = jnp.sin(ang)


def _ada_kernel(c_ref, w_ref, b_ref, o_ref):
    c = c_ref[...]
    a = c * jax.nn.sigmoid(c)
    o_ref[0, 0] = jnp.dot(a, w_ref[0], preferred_element_type=F32,
                          precision=lax.Precision.HIGHEST) + b_ref[0, 0]


def _in_proj_kernel(x_ref, mod_ref, g_ref, w_ref, zr_ref, zg_ref, zq_ref, zkv_ref, zgl_ref):
    mod = mod_ref[0]
    h = _rms_mod(x_ref[...], g_ref[...], mod[1:2], mod[0:1]).astype(BF16)
    for ref, lo, hi in ((zr_ref, OFF_RET, OFF_GMLP), (zg_ref, OFF_GMLP, OFF_Q), (zq_ref, OFF_Q, OFF_KV),
                        (zkv_ref, OFF_KV, OFF_GL), (zgl_ref, OFF_GL, N_IN_PAD)):
        ref[...] = jnp.dot(h, w_ref[:, lo:hi], preferred_element_type=F32)


def _mixer_kernel(zr_ref, zg_ref, cos_ref, sin_ref, din_ref, dq_ref, dk_ref, dc_ref, rg_ref,
                  lng_ref, ws_ref, bs_ref, yret_ref, ygm_ref, state_ref):
    @pl.when(pl.program_id(1) == 0)
    def _():
        state_ref[...] = jnp.zeros_like(state_ref)

    cos, sin = cos_ref[0], sin_ref[0]
    q = _rope(zr_ref[:, 0:RET_W], cos, sin)
    k = _rope(zr_ref[:, RET_W:2 * RET_W], cos, sin) * (HEAD_DIM ** -0.5)
    v = zr_ref[:, 2 * RET_W:3 * RET_W]
    g = zr_ref[:, 3 * RET_W:4 * RET_W]
    outs = []
    for h in range(RET_HEADS):
        sl = slice(h * HEAD_DIM, (h + 1) * HEAD_DIM)
        qh, kh, vh = q[:, sl], k[:, sl], v[:, sl]
        st = state_ref[h]
        inner = _dot_nt(qh, kh) * din_ref[h]
        o = _dot(inner, vh) + _dot(qh, st) * dq_ref[h]
        state_ref[h] = st * dc_ref[h] + _dot((kh * dk_ref[h]).T, vh)
        o = o * lax.rsqrt(jnp.mean(o * o, axis=-1, keepdims=True) + NORM_EPS) * rg_ref[h]
        outs.append(o)
    yret_ref[...] = jnp.concatenate(outs, axis=1) * (g * jax.nn.sigmoid(g))

    u = jax.nn.gelu(zg_ref[:, 0:GMLP_W])
    vv = jax.nn.gelu(zg_ref[:, GMLP_W:2 * GMLP_W])
    mu = jnp.mean(vv, axis=-1, keepdims=True)
    var = jnp.mean(jnp.square(vv - mu), axis=-1, keepdims=True)
    vn = (vv - mu) * lax.rsqrt(var + NORM_EPS) * lng_ref[...]
    row = lax.broadcasted_iota(jnp.int32, (CHUNK, CHUNK), 0)
    col = lax.broadcasted_iota(jnp.int32, (CHUNK, CHUNK), 1)
    ss = []
    for gi in range(GMLP_GROUPS):
        w = jnp.where(row >= col, ws_ref[gi], 0.0)
        ss.append(_dot(w, vn[:, gi * HEAD_DIM:(gi + 1) * HEAD_DIM]) + bs_ref[gi])
    ygm_ref[...] = u * jnp.concatenate(ss, axis=1)


def _nsa_prep_kernel(zq_ref, ks_ref, kw_ref, cos_ref, sin_ref, qg_ref, kg_ref, q_out, ks_out, kw_out):
    ones = _head_ones()
    cos, sin = cos_ref[...], sin_ref[...]
    q_out[...] = _head_rms_rope(zq_ref[...], qg_ref[...], cos, sin, ones) * (HEAD_DIM ** -0.5)
    ks_out[...] = _head_rms_rope(ks_ref[...], kg_ref[1:2], cos, sin, ones)
    kw_out[...] = _head_rms_rope(kw_ref[...], kg_ref[2:3], cos, sin, ones)


def _compress_kernel(xk_ref, xv_ref, pek_ref, pev_ref, w1k_ref, w2k_ref, w1v_ref, w2v_ref, kg_ref,
                     cos_ref, sin_ref, kc_ref, vc_ref):
    n_rows = xk_ref.shape[2]

    def comp(x, pe_ref, w1_ref, w2_ref):
        a = _dot(x + pe_ref[0:1], w1_ref[0])
        b = _dot(x + pe_ref[1:2], w1_ref[1])
        h = a + pltpu.roll(b, n_rows - 1, 0)
        return _dot(jax.nn.gelu(h), w2_ref[...])

    k2 = jnp.concatenate([comp(xk_ref[0, g], pek_ref, w1k_ref, w2k_ref) for g in range(NSA_KV_HEADS)], axis=1)
    k2 = _head_rms_rope(k2, kg_ref[0:1], cos_ref[0], sin_ref[0], _head_ones())
    for g in range(NSA_KV_HEADS):
        kc_ref[0, g] = k2[:, g * HEAD_DIM:(g + 1) * HEAD_DIM]
        vc_ref[0, g] = comp(xv_ref[0, g], pev_ref, w1v_ref, w2v_ref)


def _nsa_cmp_kernel(q_ref, kc_ref, vc_ref, oc_ref, selt_ref, *, n_sel, top_n):
    i = pl.program_id(2)
    qt = q_ref.shape[0]
    n_cmp = kc_ref.shape[2]
    qs = _stack_heads(q_ref[...]).astype(BF16)
    kc = kc_ref[0, 0].astype(BF16)
    vc = vc_ref[0, 0].astype(BF16)

    s = _dot_nt(qs, kc)
    t = i * qt + (lax.broadcasted_iota(jnp.int32, s.shape, 0) & (qt - 1))
    n = lax.broadcasted_iota(jnp.int32, s.shape, 1)
    p = _softmax_rows(s, n * CMP_STRIDE + (CMP_LEN - 1) <= t)
    oc_ref[...] = _unstack_heads(_dot(p, vc), qt)

    st = _dot_nt(kc, qs)
    tt = i * qt + (lax.broadcasted_iota(jnp.int32, st.shape, 1) & (qt - 1))
    nn = lax.broadcasted_iota(jnp.int32, st.shape, 0)
    valid = nn * CMP_STRIDE + (CMP_LEN - 1) <= tt
    sm = jnp.where(valid, st, NEG_INF)
    e = jnp.exp(sm - jnp.max(sm, axis=0, keepdims=True))
    pt = jnp.where(valid, e / jnp.sum(e, axis=0, keepdims=True), 0.0)
    psum = pt[:, 0:qt]
    for r in range(1, NSA_GROUP):
        psum = psum + pt[:, r * qt:(r + 1) * qt]
    jo = lax.broadcasted_iota(jnp.int32, (n_sel, n_cmp), 0) * SEL_BLOCK
    no = lax.broadcasted_iota(jnp.int32, (n_sel, n_cmp), 1) * CMP_STRIDE
    overlap = jnp.where((no < jo + SEL_BLOCK) & (no + CMP_LEN > jo), 1.0, 0.0).astype(BF16)
    imp = _dot_split_rhs(overlap, psum)

    j = lax.broadcasted_iota(jnp.int32, (n_sel, qt), 0)
    tq = i * qt + lax.broadcasted_iota(jnp.int32, (n_sel, qt), 1)
    cur = tq // SEL_BLOCK
    forced = (j == 0) | (j == cur) | (j == cur - 1)
    score = jnp.where(forced, FORCE_SCORE, jnp.where(j * SEL_BLOCK <= tq, imp, -1.0))
    rank = jnp.zeros((n_sel, qt), F32)
    for jp in range(n_sel):
        other = score[jp:jp + 1, :]
        before = (other > score) | ((other == score) & (j > jp))
        rank = rank + jnp.where(before, 1.0, 0.0)
    selt_ref[0, 0] = jnp.where(rank < top_n, 1.0, 0.0)


def _nsa_sel_win_kernel(q_ref, sel_ref, kst_ref, vs_ref, kwt_ref, vw_ref, os_ref, ow_ref, *, n_sel):
    i = pl.program_id(2)
    qt = q_ref.shape[0]
    rows = NSA_GROUP * qt
    sub = KEY_CHUNK // LANES
    qs = _stack_heads(q_ref[...]).astype(BF16)
    selb = sel_ref[0, 0].astype(BF16)

    def body(c, carry):
        m, l, acc = carry
        k0 = c * KEY_CHUNK
        kt = jnp.concatenate([kst_ref[0, 0, c * sub + u] for u in range(sub)], axis=1)
        s = _dot(qs, kt)
        blk = (k0 + lax.broadcasted_iota(jnp.int32, (n_sel, KEY_CHUNK), 1)) // SEL_BLOCK
        expand = jnp.where(blk == lax.broadcasted_iota(jnp.int32, (n_sel, KEY_CHUNK), 0), 1.0, 0.0)
        chosen = jnp.dot(selb, expand.astype(BF16), preferred_element_type=F32)
        chosen = jnp.concatenate([chosen] * NSA_GROUP, axis=0)
        spos = k0 + lax.broadcasted_iota(jnp.int32, (rows, KEY_CHUNK), 1)
        t = i * qt + (lax.broadcasted_iota(jnp.int32, (rows, KEY_CHUNK), 0) & (qt - 1))
        valid = (chosen > 0.5) & (spos <= t)
        sm = jnp.where(valid, s, NEG_INF)
        m_new = jnp.maximum(m, jnp.max(sm, axis=-1, keepdims=True))
        alpha = jnp.exp(m - m_new)
        p = jnp.where(valid, jnp.exp(sm - m_new), 0.0)
        l = alpha * l + jnp.sum(p, axis=-1, keepdims=True)
        acc = alpha * acc + _dot(p, vs_ref[0, 0, pl.ds(pl.multiple_of(k0, KEY_CHUNK), KEY_CHUNK), :])
        return m_new, l, acc

    n_chunks = (i * qt + qt + KEY_CHUNK - 1) // KEY_CHUNK
    init = (jnp.full((rows, 1), NEG_INF, F32), jnp.zeros((rows, 1), F32), jnp.zeros((rows, HEAD_DIM), F32))
    _, l, acc = lax.fori_loop(0, n_chunks, body, init)
    os_ref[...] = _unstack_heads(acc / l, qt)

    wsub = (WINDOW + qt) // LANES
    sblk = jnp.maximum(i * (qt // LANES) - WINDOW // LANES, 0)
    start = pl.multiple_of(sblk * LANES, LANES)
    kt = jnp.concatenate([kwt_ref[0, 0, sblk + u] for u in range(wsub)], axis=1)
    s = _dot(qs, kt)
    spos = start + lax.broadcasted_iota(jnp.int32, s.shape, 1)
    t = i * qt + (lax.broadcasted_iota(jnp.int32, s.shape, 0) & (qt - 1))
    p = _softmax_rows(s, (spos <= t) & (t - spos < WINDOW))
    ow_ref[...] = _unstack_heads(_dot(p, vw_ref[0, 0, pl.ds(start, WINDOW + qt), :]), qt)


def _out_proj_kernel(yret_ref, ygm_ref, oc_ref, os_ref, ow_ref, gl_ref, x_ref, mod_ref, gexp_ref, wout_ref,
                     g2_ref, rwt_ref, rb_ref, xn_ref, h2_ref, gt_ref, *, n_experts):
    mod = mod_ref[0]
    gates = jax.nn.sigmoid(gl_ref[...])
    gx = _dot_split_lhs(gates, gexp_ref[...])
    ynsa = (gx[:, 0:NSA_W] * oc_ref[...] + gx[:, NSA_W:2 * NSA_W] * os_ref[...]
            + gx[:, 2 * NSA_W:3 * NSA_W] * ow_ref[...])
    y = (_dot(yret_ref[...], wout_ref[0:RET_W]) + _dot(ygm_ref[...], wout_ref[RET_W:RET_W + GMLP_W])
         + _dot(ynsa, wout_ref[RET_W + GMLP_W:]))
    xn = x_ref[...] + mod[2:3] * y
    xn_ref[...] = xn
    h2 = _rms_mod(xn, g2_ref[...], mod[4:5], mod[3:4])
    h2_ref[...] = h2.astype(BF16)

    hh, hl = _split(h2)
    wh, wl = rwt_ref[0], rwt_ref[1]
    nt = lambda a, b: lax.dot_general(a, b, (((1,), (1,)), ((), ())), preferred_element_type=F32)
    logits = nt(wh, hh) + nt(wh, hl) + nt(wl, hh)
    ex = jnp.exp(logits - jnp.max(logits, axis=0, keepdims=True))
    probs = ex / jnp.sum(ex, axis=0, keepdims=True)
    sel = probs + rb_ref[...]
    per = n_experts // N_EXPERT_GROUPS
    srow = [sel[e:e + 1, :] for e in range(n_experts)]
    prow = [probs[e:e + 1, :] for e in range(n_experts)]
    gscore = []
    for gi in range(N_EXPERT_GROUPS):
        a = srow[gi * per:(gi + 1) * per]
        best = None
        for u in range(per):
            for w in range(u + 1, per):
                pair = a[u] + a[w]
                best = pair if best is None else jnp.maximum(best, pair)
        gscore.append(best)
    bg = jnp.zeros_like(gscore[0], dtype=jnp.int32)
    bs = gscore[0]
    for gi in range(1, N_EXPERT_GROUPS):
        upd = gscore[gi] > bs
        bg = jnp.where(upd, gi, bg)
        bs = jnp.where(upd, gscore[gi], bs)
    cs, cp = [], []
    for u in range(per):
        su, pu = srow[u], prow[u]
        for gi in range(1, N_EXPERT_GROUPS):
            su = jnp.where(bg == gi, srow[gi * per + u], su)
            pu = jnp.where(bg == gi, prow[gi * per + u], pu)
        cs.append(su)
        cp.append(pu)
    chosen = []
    for u in range(per):
        rank = jnp.zeros_like(cs[u])
        for w in range(per):
            if w == u:
                continue
            before = (cs[w] > cs[u]) | (cs[w] == cs[u]) if w < u else (cs[w] > cs[u])
            rank = rank + jnp.where(before, 1.0, 0.0)
        chosen.append(rank < EXPERT_TOPK)
    denom = jnp.zeros_like(cp[0])
    for u in range(per):
        denom = denom + jnp.where(chosen[u], cp[u], 0.0)
    rows_out = []
    for e in range(n_experts):
        gi, u = divmod(e, per)
        rows_out.append(jnp.where((bg == gi) & chosen[u], cp[u] / denom, 0.0))
    gt_ref[...] = jnp.concatenate(rows_out, axis=0)


def _moe_kernel(h2_ref, gate_ref, xn_ref, mod_ref, wg_ref, wu_ref, wd_ref, o_ref, acc_ref):
    e = pl.program_id(1)

    @pl.when(e == 0)
    def _():
        acc_ref[...] = jnp.zeros_like(acc_ref)

    h = h2_ref[...]
    gates = gate_ref[...]
    lane = lax.broadcasted_iota(jnp.int32, gates.shape, 1)
    gcol = jnp.sum(jnp.where(lane == e, gates, 0.0), axis=-1, keepdims=True)
    hg = jnp.dot(h, wg_ref[0], preferred_element_type=F32)
    hu = jnp.dot(h, wu_ref[0], preferred_element_type=F32)
    a = (hg * jax.nn.sigmoid(hg)) * hu
    acc_ref[...] += _dot(gcol * a, wd_ref[0])

    @pl.when(e == pl.num_programs(1) - 1)
    def _():
        o_ref[...] = xn_ref[...] + mod_ref[0][5:6] * acc_ref[...]


def _decay_tables():
    H, C, d = RET_HEADS, CHUNK, HEAD_DIM
    log_gamma = jnp.log1p(-jnp.power(2.0, -5.0 - jnp.arange(H, dtype=F32)))
    idx = jnp.arange(C, dtype=F32)
    diff = idx[:, None] - idx[None, :]
    d_in = jnp.where(diff >= 0, jnp.exp(jnp.maximum(diff, 0.0)[None] * log_gamma[:, None, None]), 0.0).astype(F32)
    d_q = jnp.exp((idx + 1.0)[:, None] * log_gamma[None]).astype(F32)
    d_k = jnp.exp((C - 1.0 - idx)[:, None] * log_gamma[None]).astype(F32)
    d_c = jnp.exp(C * log_gamma).astype(F32)
    bcast = lambda a: jnp.broadcast_to(a.T[:, :, None], (H, C, d))
    return d_in, bcast(d_q), bcast(d_k), jnp.broadcast_to(d_c[:, None, None], (H, d, d))


def _gate_expand():
    m = np.zeros((LANES, N_BRANCH * NSA_W), np.float32)
    for h in range(NSA_HEADS):
        for br in range(N_BRANCH):
            m[h * N_BRANCH + br, br * NSA_W + h * HEAD_DIM: br * NSA_W + (h + 1) * HEAD_DIM] = 1.0
    return jnp.asarray(m, BF16)


def kernel(x, c, positions, ada_w, ada_b, norm_mix_g, norm_ffn_g, w_in, w_out, ret_norm_g, gmlp_ln_g, gmlp_ws,
           gmlp_b, nsa_q_norm_g, nsa_k_norm_g, cmp_pe_k, cmp_pe_v, cmp_w1_k, cmp_w2_k, cmp_w1_v, cmp_w2_v,
           router_w, router_b, moe_w_gate, moe_w_up, moe_w_down):
    B, S, D = x.shape
    L = ada_w.shape[0]
    E = router_w.shape[1]
    DE = moe_w_gate.shape[-1]
    T = B * S
    G = NSA_KV_HEADS
    n_chunk = S // CHUNK
    n_cmp = S // CMP_STRIDE
    n_sel = S // SEL_BLOCK
    top_n = min(SEL_TOPK, n_sel)
    n_qt = S // Q_TILE
    tm = min(512, S)
    tm_moe = min(1024, T)
    assert S % KEY_CHUNK == 0 and S >= WINDOW + Q_TILE and D == RET_W + GMLP_W + NSA_W
    assert w_in.shape[-1] == N_IN and T % tm_moe == 0

    half = HEAD_DIM // 2
    inv_freq = jnp.power(ROPE_THETA, -jnp.arange(half, dtype=F32) / half)
    invf = jnp.tile(inv_freq, LANES // half)[None, :]
    cos_t, sin_t = pl.pallas_call(
        _rope_table_kernel, name="rope_tables", grid=(B,),
        in_specs=[pl.BlockSpec((1, S, 1), lambda b: (b, 0, 0)), pl.BlockSpec((1, LANES), lambda b: (0, 0))],
        out_specs=[pl.BlockSpec((1, S, LANES), lambda b: (b, 0, 0))] * 2,
        out_shape=[jax.ShapeDtypeStruct((B, S, LANES), F32)] * 2,
        compiler_params=_cparams("parallel"),
    )(positions[:, :, None], invf)
    pad_c = lambda a: jnp.pad(a[:, CMP_LEN - 1::CMP_STRIDE], ((0, 0), (0, 1), (0, 0)))
    cos_c, sin_c = pad_c(cos_t), pad_c(sin_t)
    cos_f, sin_f = cos_t.reshape(T, LANES), sin_t.reshape(T, LANES)

    mod = pl.pallas_call(
        _ada_kernel, name="ada_mod", grid=(L, 6),
        in_specs=[pl.BlockSpec((B, D), lambda l, j: (0, 0)),
                  pl.BlockSpec((1, D, D), lambda l, j: (l, 0, j)),
                  pl.BlockSpec((1, 1, 1, D), lambda l, j: (l, j, 0, 0))],
        out_specs=pl.BlockSpec((1, 1, B, D), lambda l, j: (l, j, 0, 0)),
        out_shape=jax.ShapeDtypeStruct((L, 6, B, D), F32),
        compiler_params=_cparams("parallel", "parallel"),
    )(c, ada_w, ada_b.reshape(L, 6, 1, D))
    mod = mod.transpose(0, 2, 1, 3)

    d_in, d_q, d_k, d_c = _decay_tables()
    gexp = _gate_expand()
    rw_hi = router_w.T.astype(BF16)
    rw_lo = (router_w.T - rw_hi.astype(F32)).astype(BF16)
    rwt = jnp.stack([rw_hi, rw_lo])
    rb = router_b.reshape(E, 1)
    x2 = x.reshape(T, D)
    tiles_per_batch = S // tm
    row_spec = lambda w: pl.BlockSpec((tm, w), lambda i: (i, 0))
    full = lambda shape: pl.BlockSpec(shape, lambda *_: (0,) * len(shape))

    for l in range(L):
        mod_l = mod[l]
        w_in_p = jnp.pad(w_in[l], ((0, 0), (0, N_IN_PAD - N_IN))).astype(BF16)
        zr, zg, zq, zkv, zgl = pl.pallas_call(
            _in_proj_kernel, name="in_proj", grid=(T // tm,),
            in_specs=[row_spec(D), pl.BlockSpec((1, 6, D), lambda i: (i // tiles_per_batch, 0, 0)),
                      full((1, D)), full((D, N_IN_PAD))],
            out_specs=[row_spec(4 * RET_W), row_spec(2 * GMLP_W), row_spec(NSA_W), row_spec(6 * KV_W),
                       row_spec(LANES)],
            out_shape=[jax.ShapeDtypeStruct((T, w), F32) for w in (4 * RET_W, 2 * GMLP_W, NSA_W, 6 * KV_W, LANES)],
            compiler_params=_cparams("parallel"),
        )(x2, mod_l, norm_mix_g[l][None, :], w_in_p)

        crow = lambda w: pl.BlockSpec((CHUNK, w), lambda b, cc: (b * n_chunk + cc, 0))
        y_ret, y_gm = pl.pallas_call(
            _mixer_kernel, name="ret_gmlp", grid=(B, n_chunk),
            in_specs=[crow(4 * RET_W), crow(2 * GMLP_W),
                      pl.BlockSpec((1, CHUNK, LANES), lambda b, cc: (b, cc, 0)),
                      pl.BlockSpec((1, CHUNK, LANES), lambda b, cc: (b, cc, 0)),
                      full((RET_HEADS, CHUNK, CHUNK)), full((RET_HEADS, CHUNK, HEAD_DIM)),
                      full((RET_HEADS, CHUNK, HEAD_DIM)), full((RET_HEADS, HEAD_DIM, HEAD_DIM)),
                      full((RET_HEADS, 1, HEAD_DIM)), full((1, GMLP_W)),
                      full((GMLP_GROUPS, CHUNK, CHUNK)), full((GMLP_GROUPS, CHUNK, HEAD_DIM))],
            out_specs=[crow(RET_W), crow(GMLP_W)],
            out_shape=[jax.ShapeDtypeStruct((T, RET_W), F32), jax.ShapeDtypeStruct((T, GMLP_W), F32)],
            scratch_shapes=[pltpu.VMEM((RET_HEADS, HEAD_DIM, HEAD_DIM), F32)],
            compiler_params=_cparams("parallel", "arbitrary"),
        )(zr, zg, cos_t, sin_t, d_in, d_q, d_k, d_c, ret_norm_g[l][:, None, :], gmlp_ln_g[l][None, :],
          gmlp_ws[l], jnp.broadcast_to(gmlp_b[l][:, :, None], (GMLP_GROUPS, CHUNK, HEAD_DIM)))

        kcol = lambda j: pl.BlockSpec((tm, KV_W), lambda i: (i, j))
        q_r, ks_r, kw_r = pl.pallas_call(
            _nsa_prep_kernel, name="nsa_prep", grid=(T // tm,),
            in_specs=[row_spec(NSA_W), kcol(2), kcol(4), row_spec(LANES), row_spec(LANES),
                      full((1, NSA_W)), full((N_BRANCH, KV_W))],
            out_specs=[row_spec(NSA_W), row_spec(KV_W), row_spec(KV_W)],
            out_shape=[jax.ShapeDtypeStruct((T, NSA_W), F32), jax.ShapeDtypeStruct((T, KV_W), F32),
                       jax.ShapeDtypeStruct((T, KV_W), F32)],
            compiler_params=_cparams("parallel"),
        )(zq, zkv, zkv, cos_f, sin_f, jnp.tile(nsa_q_norm_g[l], NSA_HEADS)[None, :],
          jnp.tile(nsa_k_norm_g[l], (1, NSA_KV_HEADS)))

        def chunked(col):
            a = zkv[:, col * KV_W:(col + 1) * KV_W].reshape(B, n_cmp, CMP_STRIDE, G, HEAD_DIM)
            return a.transpose(0, 3, 1, 2, 4).reshape(B, G, n_cmp, CMP_STRIDE * HEAD_DIM)
        pe2 = lambda pe: pe.reshape(2, CMP_STRIDE * HEAD_DIM)
        w12 = lambda w: w.reshape(2, CMP_STRIDE * HEAD_DIM, w.shape[-1]).astype(BF16)
        cblk = pl.BlockSpec((1, G, n_cmp, CMP_STRIDE * HEAD_DIM), lambda b: (b, 0, 0, 0))
        hid = cmp_w1_k.shape[-1]
        kc, vc = pl.pallas_call(
            _compress_kernel, name="nsa_compress", grid=(B,),
            in_specs=[cblk, cblk, full((2, CMP_STRIDE * HEAD_DIM)), full((2, CMP_STRIDE * HEAD_DIM)),
                      full((2, CMP_STRIDE * HEAD_DIM, hid)), full((hid, HEAD_DIM)),
                      full((2, CMP_STRIDE * HEAD_DIM, hid)), full((hid, HEAD_DIM)),
                      full((N_BRANCH, KV_W)),
                      pl.BlockSpec((1, n_cmp, LANES), lambda b: (b, 0, 0)),
                      pl.BlockSpec((1, n_cmp, LANES), lambda b: (b, 0, 0))],
            out_specs=[pl.BlockSpec((1, G, n_cmp, HEAD_DIM), lambda b: (b, 0, 0, 0))] * 2,
            out_shape=[jax.ShapeDtypeStruct((B, G, n_cmp, HEAD_DIM), F32)] * 2,
            compiler_params=_cparams("parallel"),
        )(chunked(0), chunked(1), pe2(cmp_pe_k[l]), pe2(cmp_pe_v[l]), w12(cmp_w1_k[l]),
          cmp_w2_k[l].astype(BF16), w12(cmp_w1_v[l]), cmp_w2_v[l].astype(BF16),
          jnp.tile(nsa_k_norm_g[l], (1, NSA_KV_HEADS)), cos_c, sin_c)

        qblk = pl.BlockSpec((Q_TILE, NSA_GROUP * HEAD_DIM), lambda b, g, i: (b * n_qt + i, g))
        cmp_blk = pl.BlockSpec((1, 1, n_cmp, HEAD_DIM), lambda b, g, i: (b, g, 0, 0))
        o_c, sel_t = pl.pallas_call(
            functools.partial(_nsa_cmp_kernel, n_sel=n_sel, top_n=top_n), name="nsa_cmp_select",
            grid=(B, G, n_qt),
            in_specs=[qblk, cmp_blk, cmp_blk],
            out_specs=[qblk, pl.BlockSpec((1, 1, n_sel, Q_TILE), lambda b, g, i: (b, g, 0, i))],
            out_shape=[jax.ShapeDtypeStruct((T, NSA_W), F32), jax.ShapeDtypeStruct((B, G, n_sel, S), F32)],
            compiler_params=_cparams("parallel", "parallel", "parallel"),
        )(q_r, kc, vc)

        def keys_t(a):
            return a.reshape(B, S // LANES, LANES, G, HEAD_DIM).transpose(0, 3, 1, 4, 2)
        def vals(col):
            return zkv[:, col * KV_W:(col + 1) * KV_W].reshape(B, S, G, HEAD_DIM).transpose(0, 2, 1, 3)
        kt_blk = pl.BlockSpec((1, 1, S // LANES, HEAD_DIM, LANES), lambda b, g, i: (b, g, 0, 0, 0))
        v_blk = pl.BlockSpec((1, 1, S, HEAD_DIM), lambda b, g, i: (b, g, 0, 0))
        o_s, o_w = pl.pallas_call(
            functools.partial(_nsa_sel_win_kernel, n_sel=n_sel), name="nsa_select_window",
            grid=(B, G, n_qt),
            in_specs=[qblk, pl.BlockSpec((1, 1, Q_TILE, n_sel), lambda b, g, i: (b, g, i, 0)),
                      kt_blk, v_blk, kt_blk, v_blk],
            out_specs=[qblk, qblk],
            out_shape=[jax.ShapeDtypeStruct((T, NSA_W), F32)] * 2,
            compiler_params=_cparams("parallel", "parallel", "parallel"),
        )(q_r, sel_t.transpose(0, 1, 3, 2), keys_t(ks_r), vals(3), keys_t(kw_r), vals(5))

        x_new, h2, gates_t = pl.pallas_call(
            functools.partial(_out_proj_kernel, n_experts=E), name="out_proj_router", grid=(T // tm,),
            in_specs=[row_spec(RET_W), row_spec(GMLP_W), row_spec(NSA_W), row_spec(NSA_W), row_spec(NSA_W),
                      row_spec(LANES), row_spec(D),
                      pl.BlockSpec((1, 6, D), lambda i: (i // tiles_per_batch, 0, 0)),
                      full((LANES, N_BRANCH * NSA_W)), full((D, D)), full((1, D)), full((2, E, D)),
                      full((E, 1))],
            out_specs=[row_spec(D), row_spec(D), pl.BlockSpec((E, tm), lambda i: (0, i))],
            out_shape=[jax.ShapeDtypeStruct((T, D), F32), jax.ShapeDtypeStruct((T, D), BF16),
                       jax.ShapeDtypeStruct((E, T), F32)],
            compiler_params=_cparams("parallel"),
        )(y_ret, y_gm, o_c, o_s, o_w, zgl, x2, mod_l, gexp, w_out[l].astype(BF16), norm_ffn_g[l][None, :],
          rwt, rb)

        tpb = S // tm_moe if S >= tm_moe else None
        mod_idx = (lambda i, e: (i // tpb, 0, 0)) if tpb else None
        assert tpb, "MoE tile must not straddle batches"
        x2 = pl.pallas_call(
            _moe_kernel, name="moe", grid=(T // tm_moe, E),
            in_specs=[pl.BlockSpec((tm_moe, D), lambda i, e: (i, 0)),
                      pl.BlockSpec((tm_moe, E), lambda i, e: (i, 0)),
                      pl.BlockSpec((tm_moe, D), lambda i, e: (i, 0)),
                      pl.BlockSpec((1, 6, D), mod_idx),
                      pl.BlockSpec((1, D, DE), lambda i, e: (e, 0, 0)),
                      pl.BlockSpec((1, D, DE), lambda i, e: (e, 0, 0)),
                      pl.BlockSpec((1, DE, D), lambda i, e: (e, 0, 0))],
            out_specs=pl.BlockSpec((tm_moe, D), lambda i, e: (i, 0)),
            out_shape=jax.ShapeDtypeStruct((T, D), F32),
            scratch_shapes=[pltpu.VMEM((tm_moe, D), F32)],
            compiler_params=_cparams("parallel", "arbitrary"),
        )(h2, gates_t.T, x_new, mod_l, moe_w_gate[l].astype(BF16), moe_w_up[l].astype(BF16),
          moe_w_down[l].astype(BF16))

    return x2.reshape(B, S, D)
```

```python
import functools

import numpy as np
import jax
import jax.numpy as jnp
from jax import lax
from jax.experimental import pallas as pl
from jax.experimental.pallas import tpu as pltpu

F32 = jnp.float32
BF16 = jnp.bfloat16

HEAD_DIM = 64
LANES = 128
RET_HEADS = 4
GMLP_GROUPS = 4
NSA_HEADS = 8
NSA_KV_HEADS = 2
NSA_GROUP = NSA_HEADS // NSA_KV_HEADS
N_BRANCH = 3
CHUNK = 128
CMP_LEN = 32
CMP_STRIDE = 16
SEL_BLOCK = 64
SEL_TOPK = 8
WINDOW = 512
ROPE_THETA = 10000.0
N_EXPERT_GROUPS = 4
EXPERT_TOPK = 2
NORM_EPS = 1e-6
NEG_INF = -1e30
FORCE_SCORE = 1e4

RET_W = RET_HEADS * HEAD_DIM
GMLP_W = GMLP_GROUPS * HEAD_DIM
NSA_W = NSA_HEADS * HEAD_DIM
KV_W = NSA_KV_HEADS * HEAD_DIM
GATE_W = NSA_HEADS * N_BRANCH
CMP_ROW = CMP_STRIDE * KV_W
OFF_RET = 0
OFF_GMLP = 4 * RET_W
OFF_Q = OFF_GMLP + 2 * GMLP_W
OFF_KV = OFF_Q + NSA_W
OFF_GL = OFF_KV + 6 * KV_W
N_IN = OFF_GL + GATE_W
N_IN_PAD = OFF_GL + LANES

Q_TILE = 128
KEY_CHUNK = 512
VMEM_LIMIT = 48 * 1024 * 1024


def _cparams(*sem):
    return pltpu.CompilerParams(dimension_semantics=sem, vmem_limit_bytes=VMEM_LIMIT)


def _dot(a, b):
    return jnp.dot(a.astype(BF16), b.astype(BF16), preferred_element_type=F32)


def _dot_nt(a, b):
    return lax.dot_general(a.astype(BF16), b.astype(BF16), (((1,), (1,)), ((), ())),
                           preferred_element_type=F32)


def _split(a):
    hi = a.astype(BF16)
    lo = (a - hi.astype(F32)).astype(BF16)
    return hi, lo


def _dot_split_lhs(a, b):
    hi, lo = _split(a)
    return (jnp.dot(hi, b, preferred_element_type=F32) + jnp.dot(lo, b, preferred_element_type=F32))


def _dot_split_rhs(a, b):
    hi, lo = _split(b)
    return (jnp.dot(a, hi, preferred_element_type=F32) + jnp.dot(a, lo, preferred_element_type=F32))


def _head_ones(width=LANES):
    r = lax.broadcasted_iota(jnp.int32, (width, width), 0) // HEAD_DIM
    c = lax.broadcasted_iota(jnp.int32, (width, width), 1) // HEAD_DIM
    return jnp.where(r == c, 1.0, 0.0).astype(BF16)


def _rotate_half(y):
    lane = lax.broadcasted_iota(jnp.int32, y.shape, 1)
    first = (lane & (HEAD_DIM - 1)) < (HEAD_DIM // 2)
    return jnp.where(first, -pltpu.roll(y, LANES - HEAD_DIM // 2, 1), pltpu.roll(y, HEAD_DIM // 2, 1))


def _rope_slab(y, cos, sin):
    return y * cos + _rotate_half(y) * sin


def _rope(x, cos, sin):
    return jnp.concatenate(
        [_rope_slab(x[:, k * LANES:(k + 1) * LANES], cos, sin) for k in range(x.shape[1] // LANES)], axis=1)


def _head_rms_rope(x, gain, cos, sin, ones):
    outs = []
    for k in range(x.shape[1] // LANES):
        xs = x[:, k * LANES:(k + 1) * LANES]
        ssq = _dot_split_lhs(xs * xs, ones)
        y = xs * lax.rsqrt(ssq * (1.0 / HEAD_DIM) + NORM_EPS) * gain[:, k * LANES:(k + 1) * LANES]
        outs.append(_rope_slab(y, cos, sin))
    return outs[0] if len(outs) == 1 else jnp.concatenate(outs, axis=1)


def _rms_mod(x, gain, scale, shift):
    y = x * lax.rsqrt(jnp.mean(x * x, axis=-1, keepdims=True) + NORM_EPS)
    return (y * gain) * (1.0 + scale) + shift


def _softmax_rows(s):
    e = jnp.exp(s - jnp.max(s, axis=-1, keepdims=True))
    return e / jnp.sum(e, axis=-1, keepdims=True)


def _mask_bias(keep):
    return jnp.where(keep, 0.0, NEG_INF)


def _rope_table_kernel(pos_ref, invf_ref, cos_ref, sin_ref):
    ang = pos_ref[0].astype(F32) * invf_ref[...]
    cos_ref[0] = jnp.cos(ang)
    sin_ref[0] = jnp.sin(ang)


def _ada_kernel(c_ref, w_ref, b_ref, o_ref):
    c = c_ref[...]
    a = c * jax.nn.sigmoid(c)
    o_ref[0, 0] = jnp.dot(a, w_ref[0], preferred_element_type=F32,
                          precision=lax.Precision.HIGHEST) + b_ref[0, 0]


_IN_SPLITS = ((OFF_RET, OFF_GMLP), (OFF_GMLP, OFF_Q), (OFF_Q, OFF_KV), (OFF_KV, OFF_KV + KV_W),
              (OFF_KV + KV_W, OFF_KV + 2 * KV_W), (OFF_KV + 2 * KV_W, OFF_GL), (OFF_GL, N_IN_PAD))


def _in_proj_kernel(x_ref, mod_ref, g_ref, w_ref, *out_refs):
    mod = mod_ref[0]
    h = _rms_mod(x_ref[...], g_ref[...], mod[1:2], mod[0:1]).astype(BF16)
    for ref, (lo, hi) in zip(out_refs, _IN_SPLITS):
        ref[...] = jnp.dot(h, w_ref[:, lo:hi], preferred_element_type=F32)


def _mixer_kernel(zr_ref, zg_ref, cos_ref, sin_ref, din_ref, dq_ref, dk_ref, dc_ref, rg_ref,
                  lng_ref, ws_ref, bs_ref, yret_ref, ygm_ref, state_ref):
    @pl.when(pl.program_id(1) == 0)
    def _():
        state_ref[...] = jnp.zeros_like(state_ref)

    cos, sin = cos_ref[0], sin_ref[0]
    q = _rope(zr_ref[:, 0:RET_W], cos, sin)
    k = _rope(zr_ref[:, RET_W:2 * RET_W], cos, sin) * (HEAD_DIM ** -0.5)
    v = zr_ref[:, 2 * RET_W:3 * RET_W]
    g = zr_ref[:, 3 * RET_W:4 * RET_W]
    outs = []
    for h in range(RET_HEADS):
        sl = slice(h * HEAD_DIM, (h + 1) * HEAD_DIM)
        qh, kh, vh = q[:, sl], k[:, sl], v[:, sl]
        st = state_ref[h]
        inner = _dot_nt(qh, kh) * din_ref[h]
        o = _dot(inner, vh) + _dot(qh, st) * dq_ref[h]
        state_ref[h] = st * dc_ref[h] + _dot((kh * dk_ref[h]).T, vh)
        o = o * lax.rsqrt(jnp.mean(o * o, axis=-1, keepdims=True) + NORM_EPS) * rg_ref[h]
        outs.append(o)
    yret_ref[...] = (jnp.concatenate(outs, axis=1) * (g * jax.nn.sigmoid(g))).astype(BF16)

    u = jax.nn.gelu(zg_ref[:, 0:GMLP_W])
    vv = jax.nn.gelu(zg_ref[:, GMLP_W:2 * GMLP_W])
    mu = jnp.mean(vv, axis=-1, keepdims=True)
    var = jnp.mean(jnp.square(vv - mu), axis=-1, keepdims=True)
    vn = (vv - mu) * lax.rsqrt(var + NORM_EPS) * lng_ref[...]
    row = lax.broadcasted_iota(jnp.int32, (CHUNK, CHUNK), 0)
    col = lax.broadcasted_iota(jnp.int32, (CHUNK, CHUNK), 1)
    ss = []
    for gi in range(GMLP_GROUPS):
        w = jnp.where(row >= col, ws_ref[gi], 0.0)
        ss.append(_dot(w, vn[:, gi * HEAD_DIM:(gi + 1) * HEAD_DIM]) + bs_ref[gi])
    ygm_ref[...] = (u * jnp.concatenate(ss, axis=1)).astype(BF16)


def _nsa_prep_kernel(zq_ref, ks_ref, vs_ref, kw_ref, vw_ref, cos_ref, sin_ref, qg_ref, kg_ref,
                     q_out, kst_out, vs_out, kwt_out, vw_out):
    ones = _head_ones()
    cos, sin = cos_ref[...], sin_ref[...]
    q_out[...] = (_head_rms_rope(zq_ref[...], qg_ref[...], cos, sin, ones) * (HEAD_DIM ** -0.5)).astype(BF16)
    kst_out[0] = _head_rms_rope(ks_ref[...], kg_ref[1:2], cos, sin, ones).T.astype(BF16)
    kwt_out[0] = _head_rms_rope(kw_ref[...], kg_ref[2:3], cos, sin, ones).T.astype(BF16)
    vs_out[...] = vs_ref[...].astype(BF16)
    vw_out[...] = vw_ref[...].astype(BF16)


def _compress_kernel(xk_ref, xv_ref, pek_ref, pev_ref, w1k_ref, w2k_ref, w1v_ref, w2v_ref, kg_ref,
                     cos_ref, sin_ref, kc_ref, vc_ref):
    n_rows = xk_ref.shape[0]

    def comp(x, pe_ref, w1_ref, w2_ref):
        a = _dot(x + pe_ref[0:1], w1_ref[0])
        b = _dot(x + pe_ref[1:2], w1_ref[1])
        h = a + pltpu.roll(b, n_rows - 1, 0)
        return _dot(jax.nn.gelu(h), w2_ref[...])

    kc_ref[0] = _head_rms_rope(comp(xk_ref[...], pek_ref, w1k_ref, w2k_ref), kg_ref[0:1],
                               cos_ref[0], sin_ref[0], _head_ones())
    vc_ref[0] = comp(xv_ref[...], pev_ref, w1v_ref, w2v_ref)


def _nsa_kernel(q_ref, gl_ref, kc_ref, vc_ref, kst_ref, vs_ref, kwt_ref, vw_ref, gexp_ref, expand_ref, o_ref,
                *, n_sel, top_n):
    i = pl.program_id(1)
    qt = q_ref.shape[0]
    n_cmp = kc_ref.shape[1]
    t0 = i * qt
    n_chunks = (t0 + qt + KEY_CHUNK - 1) // KEY_CHUNK
    wk = WINDOW + qt
    wstart = pl.multiple_of(jnp.maximum(t0 - WINDOW, 0), LANES)

    t_r = t0 + lax.broadcasted_iota(jnp.int32, (qt, n_cmp), 0)
    n_c = lax.broadcasted_iota(jnp.int32, (qt, n_cmp), 1)
    keep_c = jnp.where(n_c * CMP_STRIDE + (CMP_LEN - 1) <= t_r, 1.0, 0.0)
    bias_c = (keep_c - 1.0) * (-NEG_INF)
    t_l = t0 + lax.broadcasted_iota(jnp.int32, (n_cmp, qt), 1)
    n_s = lax.broadcasted_iota(jnp.int32, (n_cmp, qt), 0)
    keep_ct = jnp.where(n_s * CMP_STRIDE + (CMP_LEN - 1) <= t_l, 1.0, 0.0)
    keep_ct4 = jnp.concatenate([keep_ct] * NSA_GROUP, axis=1)
    bias_ct4 = (keep_ct4 - 1.0) * (-NEG_INF)
    jo = lax.broadcasted_iota(jnp.int32, (n_sel, n_cmp), 0) * SEL_BLOCK
    no = lax.broadcasted_iota(jnp.int32, (n_sel, n_cmp), 1) * CMP_STRIDE
    overlap = jnp.where((no < jo + SEL_BLOCK) & (no + CMP_LEN > jo), 1.0, 0.0).astype(BF16)
    j = lax.broadcasted_iota(jnp.int32, (n_sel, qt), 0)
    tq = t0 + lax.broadcasted_iota(jnp.int32, (n_sel, qt), 1)
    cur = tq // SEL_BLOCK
    forced = (j == 0) | (j == cur) | (j == cur - 1)
    allowed = j * SEL_BLOCK <= tq
    t_w = t0 + lax.broadcasted_iota(jnp.int32, (qt, wk), 0)
    s_w = wstart + lax.broadcasted_iota(jnp.int32, (qt, wk), 1)
    bias_w = _mask_bias((s_w <= t_w) & (t_w - s_w < WINDOW))
    t_k = t0 + lax.broadcasted_iota(jnp.int32, (qt, KEY_CHUNK), 0)
    s_k = lax.broadcasted_iota(jnp.int32, (qt, KEY_CHUNK), 1)

    rep = lambda a: jnp.concatenate([a] * NSA_GROUP, axis=0)
    keep_c4, bias_c4, bias_w4 = rep(keep_c), rep(bias_c), rep(bias_w)
    unstack = lambda o, gs: [o[r * qt:(r + 1) * qt, gs] for r in range(NSA_GROUP)]

    o_cmp, o_win, qs_g, sel_g = [], [], [], []
    for g in range(NSA_KV_HEADS):
        gs = slice(g * HEAD_DIM, (g + 1) * HEAD_DIM)
        qs = jnp.concatenate([q_ref[:, (g * NSA_GROUP + r) * HEAD_DIM:(g * NSA_GROUP + r + 1) * HEAD_DIM]
                              for r in range(NSA_GROUP)], axis=0)
        qs_g.append(qs)
        kc = kc_ref[0][:, gs].astype(BF16)
        vc = vc_ref[0].astype(BF16)

        p = _softmax_rows(_dot_nt(qs, kc) + bias_c4) * keep_c4
        o_cmp += unstack(_dot(p, vc), gs)

        st = _dot_nt(kc, qs) + bias_ct4
        e = jnp.exp(st - jnp.max(st, axis=0, keepdims=True))
        pt = e / jnp.sum(e, axis=0, keepdims=True) * keep_ct4
        psum = pt[:, 0:qt]
        for r in range(1, NSA_GROUP):
            psum = psum + pt[:, r * qt:(r + 1) * qt]
        imp = _dot_split_rhs(overlap, psum)
        score = jnp.where(forced, FORCE_SCORE, jnp.where(allowed, imp, -1.0))
        rank = jnp.zeros((n_sel, qt), F32)
        for jp in range(n_sel):
            other = score[jp:jp + 1, :]
            tie = jnp.where(j > jp, 1.0, 0.0)
            rank = rank + jnp.where(other > score, 1.0, jnp.where(other == score, tie, 0.0))
        sel_t = jnp.where(rank < top_n, 1.0, 0.0)
        if n_sel < LANES:
            sel_t = jnp.concatenate([sel_t, jnp.zeros((LANES - n_sel, qt), F32)], axis=0)
        sel_g.append(sel_t.T.astype(BF16))

        kt = kwt_ref[0, gs, pl.ds(wstart, wk)]
        p = _softmax_rows(jnp.dot(qs, kt, preferred_element_type=F32) + bias_w4)
        o_win += unstack(jnp.dot(p.astype(BF16), vw_ref[pl.ds(wstart, wk), :], preferred_element_type=F32), gs)

    def body(c, carry):
        k0 = pl.multiple_of(c * KEY_CHUNK, KEY_CHUNK)
        vv = vs_ref[pl.ds(k0, KEY_CHUNK), :]
        ex = expand_ref[:, pl.ds(k0, KEY_CHUNK)]
        causal = k0 + s_k <= t_k
        new = []
        for g in range(NSA_KV_HEADS):
            m, l, acc = carry[g]
            kt = kst_ref[0, g * HEAD_DIM:(g + 1) * HEAD_DIM, pl.ds(k0, KEY_CHUNK)]
            chosen = jnp.dot(sel_g[g], ex, preferred_element_type=F32)
            bias = rep(_mask_bias((chosen > 0.5) & causal))
            s = jnp.dot(qs_g[g], kt, preferred_element_type=F32) + bias
            m_new = jnp.maximum(m, jnp.max(s, axis=-1, keepdims=True))
            alpha = jnp.exp(m - m_new)
            p = jnp.exp(s - m_new)
            l = alpha * l + jnp.sum(p, axis=-1, keepdims=True)
            acc = alpha * acc + jnp.dot(p.astype(BF16), vv, preferred_element_type=F32)
            new.append((m_new, l, acc))
        return tuple(new)

    rows = NSA_GROUP * qt
    init = tuple((jnp.full((rows, 1), NEG_INF, F32), jnp.zeros((rows, 1), F32), jnp.zeros((rows, KV_W), F32))
                 for _ in range(NSA_KV_HEADS))
    fin = lax.fori_loop(0, n_chunks, body, init)
    o_sel = []
    for g in range(NSA_KV_HEADS):
        _, l, acc = fin[g]
        o_sel += unstack(acc / l, slice(g * HEAD_DIM, (g + 1) * HEAD_DIM))

    gx = _dot_split_lhs(jax.nn.sigmoid(gl_ref[...]), gexp_ref[...])
    cat = lambda parts: jnp.concatenate(parts, axis=1)
    o_ref[...] = (gx[:, 0:NSA_W] * cat(o_cmp) + gx[:, NSA_W:2 * NSA_W] * cat(o_sel)
                  + gx[:, 2 * NSA_W:3 * NSA_W] * cat(o_win)).astype(BF16)


def _out_proj_kernel(yret_ref, ygm_ref, ynsa_ref, x_ref, mod_ref, wout_ref,
                     g2_ref, rwt_ref, rb_ref, xn_ref, h2_ref, gt_ref, *, n_experts):
    mod = mod_ref[0]
    y = (jnp.dot(yret_ref[...], wout_ref[0:RET_W], preferred_element_type=F32)
         + jnp.dot(ygm_ref[...], wout_ref[RET_W:RET_W + GMLP_W], preferred_element_type=F32)
         + jnp.dot(ynsa_ref[...], wout_ref[RET_W + GMLP_W:], preferred_element_type=F32))
    xn = x_ref[...] + mod[2:3] * y
    xn_ref[...] = xn
    h2 = _rms_mod(xn, g2_ref[...], mod[4:5], mod[3:4])
    h2_ref[...] = h2.astype(BF16)

    hh, hl = _split(h2)
    wh, wl = rwt_ref[0], rwt_ref[1]
    nt = lambda a, b: lax.dot_general(a, b, (((1,), (1,)), ((), ())), preferred_element_type=F32)
    logits = nt(wh, hh) + nt(wh, hl) + nt(wl, hh)
    ex = jnp.exp(logits - jnp.max(logits, axis=0, keepdims=True))
    probs = ex / jnp.sum(ex, axis=0, keepdims=True)
    sel = probs + rb_ref[...]
    per = n_experts // N_EXPERT_GROUPS
    srow = [sel[e:e + 1, :] for e in range(n_experts)]
    prow = [probs[e:e + 1, :] for e in range(n_experts)]
    gscore = []
    for gi in range(N_EXPERT_GROUPS):
        a = srow[gi * per:(gi + 1) * per]
        best = None
        for u in range(per):
            for w in range(u + 1, per):
                pair = a[u] + a[w]
                best = pair if best is None else jnp.maximum(best, pair)
        gscore.append(best)
    bg = jnp.zeros_like(gscore[0], dtype=jnp.int32)
    bs = gscore[0]
    for gi in range(1, N_EXPERT_GROUPS):
        upd = gscore[gi] > bs
        bg = jnp.where(upd, gi, bg)
        bs = jnp.where(upd, gscore[gi], bs)
    cs, cp = [], []
    for u in range(per):
        su, pu = srow[u], prow[u]
        for gi in range(1, N_EXPERT_GROUPS):
            su = jnp.where(bg == gi, srow[gi * per + u], su)
            pu = jnp.where(bg == gi, prow[gi * per + u], pu)
        cs.append(su)
        cp.append(pu)
    chosen = []
    for u in range(per):
        rank = jnp.zeros_like(cs[u])
        for w in range(per):
            if w == u:
                continue
            before = (cs[w] >= cs[u]) if w < u else (cs[w] > cs[u])
            rank = rank + jnp.where(before, 1.0, 0.0)
        chosen.append(jnp.where(rank < EXPERT_TOPK, 1.0, 0.0))
    denom = jnp.zeros_like(cp[0])
    for u in range(per):
        denom = denom + chosen[u] * cp[u]
    rows_out = []
    for e in range(n_experts):
        gi, u = divmod(e, per)
        rows_out.append(jnp.where(bg == gi, chosen[u] * cp[u] / denom, 0.0))
    gt_ref[...] = jnp.concatenate(rows_out, axis=0)


def _moe_kernel(h2_ref, gate_ref, xn_ref, mod_ref, wg_ref, wu_ref, wd_ref, o_ref, acc_ref):
    e = pl.program_id(1)

    @pl.when(e == 0)
    def _():
        acc_ref[...] = jnp.zeros_like(acc_ref)

    h = h2_ref[...]
    gates = gate_ref[...]
    lane = lax.broadcasted_iota(jnp.int32, gates.shape, 1)
    gcol = jnp.sum(jnp.where(lane == e, gates, 0.0), axis=-1, keepdims=True)
    hg = jnp.dot(h, wg_ref[0], preferred_element_type=F32)
    hu = jnp.dot(h, wu_ref[0], preferred_element_type=F32)
    a = (hg * jax.nn.sigmoid(hg)) * hu
    acc_ref[...] += _dot(gcol * a, wd_ref[0])

    @pl.when(e == pl.num_programs(1) - 1)
    def _():
        o_ref[...] = xn_ref[...] + mod_ref[0][5:6] * acc_ref[...]


def _decay_tables():
    H, C, d = RET_HEADS, CHUNK, HEAD_DIM
    log_gamma = jnp.log1p(-jnp.power(2.0, -5.0 - jnp.arange(H, dtype=F32)))
    idx = jnp.arange(C, dtype=F32)
    diff = idx[:, None] - idx[None, :]
    d_in = jnp.where(diff >= 0, jnp.exp(jnp.maximum(diff, 0.0)[None] * log_gamma[:, None, None]), 0.0).astype(F32)
    d_q = jnp.exp((idx + 1.0)[:, None] * log_gamma[None]).astype(F32)
    d_k = jnp.exp((C - 1.0 - idx)[:, None] * log_gamma[None]).astype(F32)
    d_c = jnp.exp(C * log_gamma).astype(F32)
    bcast = lambda a: jnp.broadcast_to(a.T[:, :, None], (H, C, d))
    return d_in, bcast(d_q), bcast(d_k), jnp.broadcast_to(d_c[:, None, None], (H, d, d))


def _gate_expand():
    m = np.zeros((LANES, N_BRANCH * NSA_W), np.float32)
    for h in range(NSA_HEADS):
        for br in range(N_BRANCH):
            m[h * N_BRANCH + br, br * NSA_W + h * HEAD_DIM: br * NSA_W + (h + 1) * HEAD_DIM] = 1.0
    return jnp.asarray(m, BF16)


def _block_expand(seq):
    m = (np.arange(LANES)[:, None] == (np.arange(seq) // SEL_BLOCK)[None, :]).astype(np.float32)
    return jnp.asarray(m, BF16)


def _compress_weights(pe, w1, w2):
    G, d = NSA_KV_HEADS, HEAD_DIM
    hid = w1.shape[-1]
    eye = jnp.eye(G, dtype=F32)
    w1h = w1.reshape(2, CMP_STRIDE, d, hid)
    w1b = jnp.einsum('pldh,ge->plgdeh', w1h, eye).reshape(2, CMP_ROW, G * hid)
    w2b = jnp.einsum('hd,ge->ghed', w2, eye).reshape(G * hid, G * d)
    peb = jnp.broadcast_to(pe.reshape(2, CMP_STRIDE, 1, d), (2, CMP_STRIDE, G, d)).reshape(2, CMP_ROW)
    return peb, w1b.astype(BF16), w2b.astype(BF16)


def kernel(x, c, positions, ada_w, ada_b, norm_mix_g, norm_ffn_g, w_in, w_out, ret_norm_g, gmlp_ln_g, gmlp_ws,
           gmlp_b, nsa_q_norm_g, nsa_k_norm_g, cmp_pe_k, cmp_pe_v, cmp_w1_k, cmp_w2_k, cmp_w1_v, cmp_w2_v,
           router_w, router_b, moe_w_gate, moe_w_up, moe_w_down):
    B, S, D = x.shape
    L = ada_w.shape[0]
    E = router_w.shape[1]
    DE = moe_w_gate.shape[-1]
    T = B * S
    n_chunk = S // CHUNK
    n_cmp = S // CMP_STRIDE
    n_sel = S // SEL_BLOCK
    top_n = min(SEL_TOPK, n_sel)
    n_qt = S // Q_TILE
    tm = min(512, S)
    tm_moe = min(1024, S)
    assert S % KEY_CHUNK == 0 and S >= WINDOW + Q_TILE and D == RET_W + GMLP_W + NSA_W
    assert w_in.shape[-1] == N_IN and S % tm_moe == 0 and n_sel <= LANES

    half = HEAD_DIM // 2
    inv_freq = jnp.power(ROPE_THETA, -jnp.arange(half, dtype=F32) / half)
    invf = jnp.tile(inv_freq, LANES // half)[None, :]
    cos_t, sin_t = pl.pallas_call(
        _rope_table_kernel, name="rope_tables", grid=(B,),
        in_specs=[pl.BlockSpec((1, S, 1), lambda b: (b, 0, 0)), pl.BlockSpec((1, LANES), lambda b: (0, 0))],
        out_specs=[pl.BlockSpec((1, S, LANES), lambda b: (b, 0, 0))] * 2,
        out_shape=[jax.ShapeDtypeStruct((B, S, LANES), F32)] * 2,
        compiler_params=_cparams("parallel"),
    )(positions[:, :, None], invf)
    pad_c = lambda a: jnp.pad(a[:, CMP_LEN - 1::CMP_STRIDE], ((0, 0), (0, 1), (0, 0)))
    cos_c, sin_c = pad_c(cos_t), pad_c(sin_t)
    cos_f, sin_f = cos_t.reshape(T, LANES), sin_t.reshape(T, LANES)

    mod = pl.pallas_call(
        _ada_kernel, name="ada_mod", grid=(L, 6),
        in_specs=[pl.BlockSpec((B, D), lambda l, j: (0, 0)),
                  pl.BlockSpec((1, D, D), lambda l, j: (l, 0, j)),
                  pl.BlockSpec((1, 1, 1, D), lambda l, j: (l, j, 0, 0))],
        out_specs=pl.BlockSpec((1, 1, B, D), lambda l, j: (l, j, 0, 0)),
        out_shape=jax.ShapeDtypeStruct((L, 6, B, D), F32),
        compiler_params=_cparams("parallel", "parallel"),
    )(c, ada_w, ada_b.reshape(L, 6, 1, D))
    mod = mod.transpose(0, 2, 1, 3)

    d_in, d_q, d_k, d_c = _decay_tables()
    gexp = _gate_expand()
    expand = _block_expand(S)
    rw_hi = router_w.T.astype(BF16)
    rw_lo = (router_w.T - rw_hi.astype(F32)).astype(BF16)
    rwt = jnp.stack([rw_hi, rw_lo])
    rb = router_b.reshape(E, 1)
    x2 = x.reshape(T, D)
    tiles_per_batch = S // tm
    row_spec = lambda w: pl.BlockSpec((tm, w), lambda i: (i, 0))
    full = lambda shape: pl.BlockSpec(shape, lambda *_: (0,) * len(shape))

    for l in range(L):
        mod_l = mod[l]
        w_in_p = jnp.pad(w_in[l], ((0, 0), (0, N_IN_PAD - N_IN))).astype(BF16)
        in_widths = tuple(hi - lo for lo, hi in _IN_SPLITS)
        zr, zg, zq, zkc, zvc, zkv, zgl = pl.pallas_call(
            _in_proj_kernel, name="in_proj", grid=(T // tm,),
            in_specs=[row_spec(D), pl.BlockSpec((1, 6, D), lambda i: (i // tiles_per_batch, 0, 0)),
                      full((1, D)), full((D, N_IN_PAD))],
            out_specs=[row_spec(w) for w in in_widths],
            out_shape=[jax.ShapeDtypeStruct((T, w), F32) for w in in_widths],
            compiler_params=_cparams("parallel"),
        )(x2, mod_l, norm_mix_g[l][None, :], w_in_p)

        crow = lambda w: pl.BlockSpec((CHUNK, w), lambda b, cc: (b * n_chunk + cc, 0))
        y_ret, y_gm = pl.pallas_call(
            _mixer_kernel, name="ret_gmlp", grid=(B, n_chunk),
            in_specs=[crow(4 * RET_W), crow(2 * GMLP_W),
                      pl.BlockSpec((1, CHUNK, LANES), lambda b, cc: (b, cc, 0)),
                      pl.BlockSpec((1, CHUNK, LANES), lambda b, cc: (b, cc, 0)),
                      full((RET_HEADS, CHUNK, CHUNK)), full((RET_HEADS, CHUNK, HEAD_DIM)),
                      full((RET_HEADS, CHUNK, HEAD_DIM)), full((RET_HEADS, HEAD_DIM, HEAD_DIM)),
                      full((RET_HEADS, 1, HEAD_DIM)), full((1, GMLP_W)),
                      full((GMLP_GROUPS, CHUNK, CHUNK)), full((GMLP_GROUPS, CHUNK, HEAD_DIM))],
            out_specs=[crow(RET_W), crow(GMLP_W)],
            out_shape=[jax.ShapeDtypeStruct((T, RET_W), BF16), jax.ShapeDtypeStruct((T, GMLP_W), BF16)],
            scratch_shapes=[pltpu.VMEM((RET_HEADS, HEAD_DIM, HEAD_DIM), F32)],
            compiler_params=_cparams("parallel", "arbitrary"),
        )(zr, zg, cos_t, sin_t, d_in, d_q, d_k, d_c, ret_norm_g[l][:, None, :], gmlp_ln_g[l][None, :],
          gmlp_ws[l], jnp.broadcast_to(gmlp_b[l][:, :, None], (GMLP_GROUPS, CHUNK, HEAD_DIM)))

        kcol = lambda j: pl.BlockSpec((tm, KV_W), lambda i: (i, j))
        kt_out = pl.BlockSpec((1, KV_W, tm), lambda i: (i // tiles_per_batch, 0, i % tiles_per_batch))
        kgain = jnp.tile(nsa_k_norm_g[l], (1, NSA_KV_HEADS))
        q_r, ks_t, vs_b, kw_t, vw_b = pl.pallas_call(
            _nsa_prep_kernel, name="nsa_prep", grid=(T // tm,),
            in_specs=[row_spec(NSA_W), kcol(0), kcol(1), kcol(2), kcol(3), row_spec(LANES), row_spec(LANES),
                      full((1, NSA_W)), full((N_BRANCH, KV_W))],
            out_specs=[row_spec(NSA_W), kt_out, row_spec(KV_W), kt_out, row_spec(KV_W)],
            out_shape=[jax.ShapeDtypeStruct((T, NSA_W), BF16), jax.ShapeDtypeStruct((B, KV_W, S), BF16),
                       jax.ShapeDtypeStruct((T, KV_W), BF16), jax.ShapeDtypeStruct((B, KV_W, S), BF16),
                       jax.ShapeDtypeStruct((T, KV_W), BF16)],
            compiler_params=_cparams("parallel"),
        )(zq, zkv, zkv, zkv, zkv, cos_f, sin_f, jnp.tile(nsa_q_norm_g[l], NSA_HEADS)[None, :], kgain)

        pek, w1k, w2k = _compress_weights(cmp_pe_k[l], cmp_w1_k[l], cmp_w2_k[l])
        pev, w1v, w2v = _compress_weights(cmp_pe_v[l], cmp_w1_v[l], cmp_w2_v[l])
        cblk = pl.BlockSpec((n_cmp, CMP_ROW), lambda b: (b, 0))
        ghid = w1k.shape[-1]
        ctab = pl.BlockSpec((1, n_cmp, LANES), lambda b: (b, 0, 0))
        kc, vc = pl.pallas_call(
            _compress_kernel, name="nsa_compress", grid=(B,),
            in_specs=[cblk, cblk, full((2, CMP_ROW)), full((2, CMP_ROW)),
                      full((2, CMP_ROW, ghid)), full((ghid, KV_W)), full((2, CMP_ROW, ghid)), full((ghid, KV_W)),
                      full((N_BRANCH, KV_W)), ctab, ctab],
            out_specs=[ctab, ctab],
            out_shape=[jax.ShapeDtypeStruct((B, n_cmp, KV_W), F32)] * 2,
            compiler_params=_cparams("parallel"),
        )(zkc.reshape(T // CMP_STRIDE, CMP_ROW), zvc.reshape(T // CMP_STRIDE, CMP_ROW), pek, pev, w1k, w2k, w1v, w2v,
          kgain, cos_c, sin_c)

        qrow = lambda w: pl.BlockSpec((Q_TILE, w), lambda b, i: (b * n_qt + i, 0))
        kt_in = pl.BlockSpec((1, KV_W, S), lambda b, i: (b, 0, 0))
        v_in = pl.BlockSpec((S, KV_W), lambda b, i: (b, 0))
        ctab2 = pl.BlockSpec((1, n_cmp, KV_W), lambda b, i: (b, 0, 0))
        y_nsa = pl.pallas_call(
            functools.partial(_nsa_kernel, n_sel=n_sel, top_n=top_n), name="nsa_attn", grid=(B, n_qt),
            in_specs=[qrow(NSA_W), qrow(LANES), ctab2, ctab2, kt_in, v_in, kt_in, v_in,
                      full((LANES, N_BRANCH * NSA_W)), full((LANES, S))],
            out_specs=qrow(NSA_W),
            out_shape=jax.ShapeDtypeStruct((T, NSA_W), BF16),
            compiler_params=_cparams("parallel", "parallel"),
        )(q_r, zgl, kc, vc, ks_t, vs_b, kw_t, vw_b, gexp, expand)

        x_new, h2, gates_t = pl.pallas_call(
            functools.partial(_out_proj_kernel, n_experts=E), name="out_proj_router", grid=(T // tm,),
            in_specs=[row_spec(RET_W), row_spec(GMLP_W), row_spec(NSA_W), row_spec(D),
                      pl.BlockSpec((1, 6, D), lambda i: (i // tiles_per_batch, 0, 0)),
                      full((D, D)), full((1, D)), full((2, E, D)), full((E, 1))],
            out_specs=[row_spec(D), row_spec(D), pl.BlockSpec((E, tm), lambda i: (0, i))],
            out_shape=[jax.ShapeDtypeStruct((T, D), F32), jax.ShapeDtypeStruct((T, D), BF16),
                       jax.ShapeDtypeStruct((E, T), F32)],
            compiler_params=_cparams("parallel"),
        )(y_ret, y_gm, y_nsa, x2, mod_l, w_out[l].astype(BF16), norm_ffn_g[l][None, :], rwt, rb)

        tpb = S // tm_moe
        x2 = pl.pallas_call(
            _moe_kernel, name="moe", grid=(T // tm_moe, E),
            in_specs=[pl.BlockSpec((tm_moe, D), lambda i, e: (i, 0)),
                      pl.BlockSpec((tm_moe, E), lambda i, e: (i, 0)),
                      pl.BlockSpec((tm_moe, D), lambda i, e: (i, 0)),
                      pl.BlockSpec((1, 6, D), lambda i, e: (i // tpb, 0, 0)),
                      pl.BlockSpec((1, D, DE), lambda i, e: (e, 0, 0)),
                      pl.BlockSpec((1, D, DE), lambda i, e: (e, 0, 0)),
                      pl.BlockSpec((1, DE, D), lambda i, e: (e, 0, 0))],
            out_specs=pl.BlockSpec((tm_moe, D), lambda i, e: (i, 0)),
            out_shape=jax.ShapeDtypeStruct((T, D), F32),
            scratch_shapes=[pltpu.VMEM((tm_moe, D), F32)],
            compiler_params=_cparams("parallel", "arbitrary"),
        )(h2, gates_t.T, x_new, mod_l, moe_w_gate[l].astype(BF16), moe_w_up[l].astype(BF16),
          moe_w_down[l].astype(BF16))

    return x2.reshape(B, S, D)
```

```python
import functools

import numpy as np
import jax
import jax.numpy as jnp
from jax import lax
from jax.experimental import pallas as pl
from jax.experimental.pallas import tpu as pltpu

F32 = jnp.float32
BF16 = jnp.bfloat16

HEAD_DIM = 64
LANES = 128
RET_HEADS = 4
GMLP_GROUPS = 4
NSA_HEADS = 8
NSA_KV_HEADS = 2
NSA_GROUP = NSA_HEADS // NSA_KV_HEADS
N_BRANCH = 3
CHUNK = 128
CMP_LEN = 32
CMP_STRIDE = 16
SEL_BLOCK = 64
SEL_TOPK = 8
WINDOW = 512
ROPE_THETA = 10000.0
N_EXPERT_GROUPS = 4
EXPERT_TOPK = 2
NORM_EPS = 1e-6
NEG_INF = -1e30
FORCE_SCORE = 1e4

RET_W = RET_HEADS * HEAD_DIM
GMLP_W = GMLP_GROUPS * HEAD_DIM
NSA_W = NSA_HEADS * HEAD_DIM
KV_W = NSA_KV_HEADS * HEAD_DIM
GATE_W = NSA_HEADS * N_BRANCH
CMP_ROW = CMP_STRIDE * KV_W
OFF_RET = 0
OFF_GMLP = 4 * RET_W
OFF_Q = OFF_GMLP + 2 * GMLP_W
OFF_KV = OFF_Q + NSA_W
OFF_GL = OFF_KV + 6 * KV_W
N_IN = OFF_GL + GATE_W
N_IN_PAD = OFF_GL + LANES

Q_TILE = 128
KEY_CHUNK = 512
ROW_BLOCK = 32
ROW_UNROLL = True
VMEM_LIMIT = 48 * 1024 * 1024


def _cparams(*sem):
    return pltpu.CompilerParams(dimension_semantics=sem, vmem_limit_bytes=VMEM_LIMIT)


def _dot(a, b):
    return jnp.dot(a.astype(BF16), b.astype(BF16), preferred_element_type=F32)


def _dot_nt(a, b):
    return lax.dot_general(a.astype(BF16), b.astype(BF16), (((1,), (1,)), ((), ())),
                           preferred_element_type=F32)


def _split(a):
    hi = a.astype(BF16)
    lo = (a - hi.astype(F32)).astype(BF16)
    return hi, lo


def _dot_split_lhs(a, b):
    hi, lo = _split(a)
    return (jnp.dot(hi, b, preferred_element_type=F32) + jnp.dot(lo, b, preferred_element_type=F32))


def _dot_split_rhs(a, b):
    hi, lo = _split(b)
    return (jnp.dot(a, hi, preferred_element_type=F32) + jnp.dot(a, lo, preferred_element_type=F32))


def _head_ones(width=LANES):
    r = lax.broadcasted_iota(jnp.int32, (width, width), 0) // HEAD_DIM
    c = lax.broadcasted_iota(jnp.int32, (width, width), 1) // HEAD_DIM
    return jnp.where(r == c, 1.0, 0.0).astype(BF16)


def _rotate_half(y):
    lane = lax.broadcasted_iota(jnp.int32, y.shape, 1)
    first = (lane & (HEAD_DIM - 1)) < (HEAD_DIM // 2)
    return jnp.where(first, -pltpu.roll(y, LANES - HEAD_DIM // 2, 1), pltpu.roll(y, HEAD_DIM // 2, 1))


def _rope_slab(y, cos, sin):
    return y * cos + _rotate_half(y) * sin


def _rope(x, cos, sin):
    return jnp.concatenate(
        [_rope_slab(x[:, k * LANES:(k + 1) * LANES], cos, sin) for k in range(x.shape[1] // LANES)], axis=1)


def _head_rms_rope(x, gain, cos, sin, ones):
    outs = []
    for k in range(x.shape[1] // LANES):
        xs = x[:, k * LANES:(k + 1) * LANES]
        ssq = _dot_split_lhs(xs * xs, ones)
        y = xs * lax.rsqrt(ssq * (1.0 / HEAD_DIM) + NORM_EPS) * gain[:, k * LANES:(k + 1) * LANES]
        outs.append(_rope_slab(y, cos, sin))
    return outs[0] if len(outs) == 1 else jnp.concatenate(outs, axis=1)


def _rms_mod(x, gain, scale, shift):
    y = x * lax.rsqrt(jnp.mean(x * x, axis=-1, keepdims=True) + NORM_EPS)
    return (y * gain) * (1.0 + scale) + shift


def _softmax_rows(s):
    e = jnp.exp(s - jnp.max(s, axis=-1, keepdims=True))
    return e / jnp.sum(e, axis=-1, keepdims=True)


def _mask_bias(keep):
    return jnp.where(keep, 0.0, NEG_INF)


def _rope_table_kernel(pos_ref, invf_ref, cos_ref, sin_ref):
    ang = pos_ref[0].astype(F32) * invf_ref[...]
    cos_ref[0] = jnp.cos(ang)
    sin_ref[0] = jnp.sin(ang)


def _ada_kernel(c_ref, w_ref, b_ref, o_ref):
    c = c_ref[...]
    a = c * jax.nn.sigmoid(c)
    o_ref[0, 0] = jnp.dot(a, w_ref[0], preferred_element_type=F32,
                          precision=lax.Precision.HIGHEST) + b_ref[0, 0]


_IN_SPLITS = ((OFF_RET, OFF_GMLP), (OFF_GMLP, OFF_Q), (OFF_Q, OFF_KV), (OFF_KV, OFF_KV + KV_W),
              (OFF_KV + KV_W, OFF_KV + 2 * KV_W), (OFF_KV + 2 * KV_W, OFF_GL), (OFF_GL, N_IN_PAD))


def _in_proj_kernel(x_ref, mod_ref, g_ref, w_ref, *out_refs):
    mod = mod_ref[0]
    h = _rms_mod(x_ref[...], g_ref[...], mod[1:2], mod[0:1]).astype(BF16)
    for ref, (lo, hi) in zip(out_refs, _IN_SPLITS):
        ref[...] = jnp.dot(h, w_ref[:, lo:hi], preferred_element_type=F32)


def _mixer_kernel(zr_ref, zg_ref, cos_ref, sin_ref, din_ref, dq_ref, dk_ref, dc_ref, rg_ref,
                  lng_ref, ws_ref, bs_ref, yret_ref, ygm_ref, state_ref):
    @pl.when(pl.program_id(1) == 0)
    def _():
        state_ref[...] = jnp.zeros_like(state_ref)

    cos, sin = cos_ref[0], sin_ref[0]
    q = _rope(zr_ref[:, 0:RET_W], cos, sin)
    k = _rope(zr_ref[:, RET_W:2 * RET_W], cos, sin) * (HEAD_DIM ** -0.5)
    v = zr_ref[:, 2 * RET_W:3 * RET_W]
    g = zr_ref[:, 3 * RET_W:4 * RET_W]
    outs = []
    for h in range(RET_HEADS):
        sl = slice(h * HEAD_DIM, (h + 1) * HEAD_DIM)
        qh, kh, vh = q[:, sl], k[:, sl], v[:, sl]
        st = state_ref[h]
        inner = _dot_nt(qh, kh) * din_ref[h]
        o = _dot(inner, vh) + _dot(qh, st) * dq_ref[h]
        state_ref[h] = st * dc_ref[h] + _dot((kh * dk_ref[h]).T, vh)
        o = o * lax.rsqrt(jnp.mean(o * o, axis=-1, keepdims=True) + NORM_EPS) * rg_ref[h]
        outs.append(o)
    yret_ref[...] = (jnp.concatenate(outs, axis=1) * (g * jax.nn.sigmoid(g))).astype(BF16)

    u = jax.nn.gelu(zg_ref[:, 0:GMLP_W])
    vv = jax.nn.gelu(zg_ref[:, GMLP_W:2 * GMLP_W])
    mu = jnp.mean(vv, axis=-1, keepdims=True)
    var = jnp.mean(jnp.square(vv - mu), axis=-1, keepdims=True)
    vn = (vv - mu) * lax.rsqrt(var + NORM_EPS) * lng_ref[...]
    row = lax.broadcasted_iota(jnp.int32, (CHUNK, CHUNK), 0)
    col = lax.broadcasted_iota(jnp.int32, (CHUNK, CHUNK), 1)
    ss = []
    for gi in range(GMLP_GROUPS):
        w = jnp.where(row >= col, ws_ref[gi], 0.0)
        ss.append(_dot(w, vn[:, gi * HEAD_DIM:(gi + 1) * HEAD_DIM]) + bs_ref[gi])
    ygm_ref[...] = (u * jnp.concatenate(ss, axis=1)).astype(BF16)


def _nsa_prep_kernel(zq_ref, ks_ref, vs_ref, kw_ref, vw_ref, cos_ref, sin_ref, qg_ref, kg_ref,
                     q_out, kst_out, vs_out, kwt_out, vw_out):
    ones = _head_ones()
    cos, sin = cos_ref[...], sin_ref[...]
    q_out[...] = (_head_rms_rope(zq_ref[...], qg_ref[...], cos, sin, ones) * (HEAD_DIM ** -0.5)).astype(BF16)
    kst_out[0] = _head_rms_rope(ks_ref[...], kg_ref[1:2], cos, sin, ones).T.astype(BF16)
    kwt_out[0] = _head_rms_rope(kw_ref[...], kg_ref[2:3], cos, sin, ones).T.astype(BF16)
    vs_out[...] = vs_ref[...].astype(BF16)
    vw_out[...] = vw_ref[...].astype(BF16)


def _compress_kernel(xk_ref, xv_ref, pek_ref, pev_ref, w1k_ref, w2k_ref, w1v_ref, w2v_ref, kg_ref,
                     cos_ref, sin_ref, kc_ref, vc_ref):
    n_rows = xk_ref.shape[0]

    def comp(x, pe_ref, w1_ref, w2_ref):
        a = _dot(x + pe_ref[0:1], w1_ref[0])
        b = _dot(x + pe_ref[1:2], w1_ref[1])
        h = a + pltpu.roll(b, n_rows - 1, 0)
        return _dot(jax.nn.gelu(h), w2_ref[...])

    kc_ref[0] = _head_rms_rope(comp(xk_ref[...], pek_ref, w1k_ref, w2k_ref), kg_ref[0:1],
                               cos_ref[0], sin_ref[0], _head_ones())
    vc_ref[0] = comp(xv_ref[...], pev_ref, w1v_ref, w2v_ref)


def _nsa_kernel(q_ref, gl_ref, kc_ref, vc_ref, kst_ref, vs_ref, kwt_ref, vw_ref, gexp_ref, expand_ref, o_ref,
                s_scr, p_scr, bias_scr, m_scr, a_scr, acc_scr, *, n_sel, top_n):
    i = pl.program_id(1)
    qt = q_ref.shape[0]
    n_cmp = kc_ref.shape[1]
    t0 = i * qt
    n_chunks = (t0 + qt + KEY_CHUNK - 1) // KEY_CHUNK
    wk = WINDOW + qt
    wstart = pl.multiple_of(jnp.maximum(t0 - WINDOW, 0), LANES)

    t_r = t0 + lax.broadcasted_iota(jnp.int32, (qt, n_cmp), 0)
    n_c = lax.broadcasted_iota(jnp.int32, (qt, n_cmp), 1)
    keep_c = jnp.where(n_c * CMP_STRIDE + (CMP_LEN - 1) <= t_r, 1.0, 0.0)
    bias_c = (keep_c - 1.0) * (-NEG_INF)
    t_l = t0 + lax.broadcasted_iota(jnp.int32, (n_cmp, qt), 1)
    n_s = lax.broadcasted_iota(jnp.int32, (n_cmp, qt), 0)
    keep_ct = jnp.where(n_s * CMP_STRIDE + (CMP_LEN - 1) <= t_l, 1.0, 0.0)
    keep_ct4 = jnp.concatenate([keep_ct] * NSA_GROUP, axis=1)
    bias_ct4 = (keep_ct4 - 1.0) * (-NEG_INF)
    jo = lax.broadcasted_iota(jnp.int32, (n_sel, n_cmp), 0) * SEL_BLOCK
    no = lax.broadcasted_iota(jnp.int32, (n_sel, n_cmp), 1) * CMP_STRIDE
    overlap = jnp.where((no < jo + SEL_BLOCK) & (no + CMP_LEN > jo), 1.0, 0.0).astype(BF16)
    j = lax.broadcasted_iota(jnp.int32, (n_sel, qt), 0)
    tq = t0 + lax.broadcasted_iota(jnp.int32, (n_sel, qt), 1)
    cur = tq // SEL_BLOCK
    forced = (j == 0) | (j == cur) | (j == cur - 1)
    allowed = j * SEL_BLOCK <= tq
    t_w = t0 + lax.broadcasted_iota(jnp.int32, (qt, wk), 0)
    s_w = wstart + lax.broadcasted_iota(jnp.int32, (qt, wk), 1)
    bias_w = _mask_bias((s_w <= t_w) & (t_w - s_w < WINDOW))
    t_k = t0 + lax.broadcasted_iota(jnp.int32, (qt, KEY_CHUNK), 0)
    s_k = lax.broadcasted_iota(jnp.int32, (qt, KEY_CHUNK), 1)

    rep = lambda a: jnp.concatenate([a] * NSA_GROUP, axis=0)
    keep_c4, bias_c4 = rep(keep_c), rep(bias_c)
    unstack = lambda o, gs: [o[r * qt:(r + 1) * qt, gs] for r in range(NSA_GROUP)]
    rows = NSA_GROUP * qt
    lane_v = lax.broadcasted_iota(jnp.int32, (1, KV_W), 1)

    def attend(g, qs, kt, vv, width, first):
        own = (lane_v >= g * HEAD_DIM) & (lane_v < (g + 1) * HEAD_DIM)
        vv1 = jnp.where(own, vv, jnp.ones_like(vv))
        s_scr[:, 0:width] = jnp.dot(qs, kt, preferred_element_type=F32)

        def masked(b):
            r0 = pl.multiple_of(b * ROW_BLOCK, ROW_BLOCK)
            rq = pl.multiple_of(r0 & (qt - 1), ROW_BLOCK)
            return r0, s_scr[pl.ds(r0, ROW_BLOCK), 0:width] + bias_scr[pl.ds(rq, ROW_BLOCK), 0:width]

        def max_pass(b, carry):
            r0, s = masked(b)
            mx = jnp.max(s, axis=-1, keepdims=True)
            if first:
                m_new = jnp.broadcast_to(mx, (ROW_BLOCK, LANES))
            else:
                m_old = m_scr[g, pl.ds(r0, ROW_BLOCK), :]
                m_new = jnp.maximum(m_old, mx)
                a_scr[pl.ds(r0, ROW_BLOCK), :] = jnp.exp(m_old - m_new)
            m_scr[g, pl.ds(r0, ROW_BLOCK), :] = m_new
            return carry

        def exp_pass(b, carry):
            r0, s = masked(b)
            m_new = m_scr[g, pl.ds(r0, ROW_BLOCK), :]
            p = jnp.exp(s - jnp.concatenate([m_new] * (width // LANES), axis=1))
            p_scr[pl.ds(r0, ROW_BLOCK), 0:width] = p.astype(BF16)
            return carry

        lax.fori_loop(0, rows // ROW_BLOCK, max_pass, 0, unroll=ROW_UNROLL)
        lax.fori_loop(0, rows // ROW_BLOCK, exp_pass, 0, unroll=ROW_UNROLL)
        pv = jnp.dot(p_scr[:, 0:width], vv1, preferred_element_type=F32)
        acc_scr[g] = pv if first else a_scr[...] * acc_scr[g] + pv

    def finish(g):
        acc = acc_scr[g]
        num, den = acc[:, g * HEAD_DIM:(g + 1) * HEAD_DIM], acc[:, (1 - g) * HEAD_DIM:(2 - g) * HEAD_DIM]
        return unstack(num / den, slice(None))

    bias_scr[:, 0:wk] = bias_w
    o_cmp, o_win, qs_g, sel_g = [], [], [], []
    for g in range(NSA_KV_HEADS):
        gs = slice(g * HEAD_DIM, (g + 1) * HEAD_DIM)
        qs = jnp.concatenate([q_ref[:, (g * NSA_GROUP + r) * HEAD_DIM:(g * NSA_GROUP + r + 1) * HEAD_DIM]
                              for r in range(NSA_GROUP)], axis=0)
        qs_g.append(qs)
        kc = kc_ref[0][:, gs].astype(BF16)
        vc = vc_ref[0].astype(BF16)

        p = _softmax_rows(_dot_nt(qs, kc) + bias_c4) * keep_c4
        o_cmp += unstack(_dot(p, vc), gs)

        st = _dot_nt(kc, qs) + bias_ct4
        e = jnp.exp(st - jnp.max(st, axis=0, keepdims=True))
        pt = e / jnp.sum(e, axis=0, keepdims=True) * keep_ct4
        psum = pt[:, 0:qt]
        for r in range(1, NSA_GROUP):
            psum = psum + pt[:, r * qt:(r + 1) * qt]
        imp = _dot_split_rhs(overlap, psum)
        score = jnp.where(forced, FORCE_SCORE, jnp.where(allowed, imp, -1.0))
        rank = jnp.zeros((n_sel, qt), F32)
        for jp in range(n_sel):
            other = score[jp:jp + 1, :]
            tie = jnp.where(j > jp, 1.0, 0.0)
            rank = rank + jnp.where(other > score, 1.0, jnp.where(other == score, tie, 0.0))
        sel_t = jnp.where(rank < top_n, 1.0, 0.0)
        if n_sel < LANES:
            sel_t = jnp.concatenate([sel_t, jnp.zeros((LANES - n_sel, qt), F32)], axis=0)
        sel_g.append(sel_t.T.astype(BF16))

        attend(g, qs, kwt_ref[0, gs, pl.ds(wstart, wk)], vw_ref[pl.ds(wstart, wk), :], wk, True)
        o_win += finish(g)

    def chunk(c, first):
        k0 = pl.multiple_of(c * KEY_CHUNK, KEY_CHUNK)
        vv = vs_ref[pl.ds(k0, KEY_CHUNK), :]
        ex = expand_ref[:, pl.ds(k0, KEY_CHUNK)]
        causal = k0 + s_k <= t_k
        for g in range(NSA_KV_HEADS):
            chosen = jnp.dot(sel_g[g], ex, preferred_element_type=F32)
            bias_scr[:, 0:KEY_CHUNK] = _mask_bias((chosen > 0.5) & causal)
            attend(g, qs_g[g], kst_ref[0, g * HEAD_DIM:(g + 1) * HEAD_DIM, pl.ds(k0, KEY_CHUNK)], vv,
                   KEY_CHUNK, first)

    chunk(0, True)

    def body(c, carry):
        chunk(c, False)
        return carry

    lax.fori_loop(1, n_chunks, body, 0)
    o_sel = finish(0) + finish(1)

    gx = _dot_split_lhs(jax.nn.sigmoid(gl_ref[...]), gexp_ref[...])
    cat = lambda parts: jnp.concatenate(parts, axis=1)
    o_ref[...] = (gx[:, 0:NSA_W] * cat(o_cmp) + gx[:, NSA_W:2 * NSA_W] * cat(o_sel)
                  + gx[:, 2 * NSA_W:3 * NSA_W] * cat(o_win)).astype(BF16)


def _out_proj_kernel(yret_ref, ygm_ref, ynsa_ref, x_ref, mod_ref, wout_ref,
                     g2_ref, rwt_ref, rb_ref, xn_ref, hx_ref, bg_ref, cnt_ref, *, n_experts):
    mod = mod_ref[0]
    d_model = x_ref.shape[1]
    y = (jnp.dot(yret_ref[...], wout_ref[0:RET_W], preferred_element_type=F32)
         + jnp.dot(ygm_ref[...], wout_ref[RET_W:RET_W + GMLP_W], preferred_element_type=F32)
         + jnp.dot(ynsa_ref[...], wout_ref[RET_W + GMLP_W:], preferred_element_type=F32))
    xn = x_ref[...] + mod[2:3] * y
    xn_ref[...] = xn
    h2 = _rms_mod(xn, g2_ref[...], mod[4:5], mod[3:4])
    hx_ref[:, 0:d_model] = h2

    hh, hl = _split(h2)
    wh, wl = rwt_ref[0], rwt_ref[1]
    nt = lambda a, b: lax.dot_general(a, b, (((1,), (1,)), ((), ())), preferred_element_type=F32)
    logits = nt(wh, hh) + nt(wh, hl) + nt(wl, hh)
    ex = jnp.exp(logits - jnp.max(logits, axis=0, keepdims=True))
    probs = ex / jnp.sum(ex, axis=0, keepdims=True)
    sel = probs + rb_ref[...]
    per = n_experts // N_EXPERT_GROUPS
    srow = [sel[e:e + 1, :] for e in range(n_experts)]
    prow = [probs[e:e + 1, :] for e in range(n_experts)]
    gscore = []
    for gi in range(N_EXPERT_GROUPS):
        a = srow[gi * per:(gi + 1) * per]
        best = None
        for u in range(per):
            for w in range(u + 1, per):
                pair = a[u] + a[w]
                best = pair if best is None else jnp.maximum(best, pair)
        gscore.append(best)
    bg = jnp.zeros_like(gscore[0], dtype=jnp.int32)
    bs = gscore[0]
    for gi in range(1, N_EXPERT_GROUPS):
        upd = gscore[gi] > bs
        bg = jnp.where(upd, gi, bg)
        bs = jnp.where(upd, gscore[gi], bs)
    cs, cp = [], []
    for u in range(per):
        su, pu = srow[u], prow[u]
        for gi in range(1, N_EXPERT_GROUPS):
            su = jnp.where(bg == gi, srow[gi * per + u], su)
            pu = jnp.where(bg == gi, prow[gi * per + u], pu)
        cs.append(su)
        cp.append(pu)
    chosen = []
    for u in range(per):
        rank = jnp.zeros_like(cs[u])
        for w in range(per):
            if w == u:
                continue
            before = (cs[w] >= cs[u]) if w < u else (cs[w] > cs[u])
            rank = rank + jnp.where(before, 1.0, 0.0)
        chosen.append(jnp.where(rank < EXPERT_TOPK, 1.0, 0.0))
    denom = jnp.zeros_like(cp[0])
    for u in range(per):
        denom = denom + chosen[u] * cp[u]
    tm = bg.shape[1]
    grows = [chosen[u] * cp[u] / denom for u in range(per)]
    gpad = jnp.concatenate(grows + [jnp.zeros((LANES - per, tm), F32)], axis=0)
    hx_ref[:, d_model:d_model + LANES] = gpad.T
    bg_ref[...] = bg
    cnt_ref[0] = jnp.concatenate(
        [jnp.broadcast_to(jnp.sum(jnp.where(bg == gi, 1.0, 0.0), axis=1, keepdims=True), (1, LANES))
         for gi in range(N_EXPERT_GROUPS)], axis=0)


def _route_kernel(bg_ref, off_ref, tri_ref, dest_ref):
    bg = bg_ref[...]
    tm = bg.shape[1]
    member = [jnp.where(bg == gi, 1.0, 0.0) for gi in range(N_EXPERT_GROUPS)]
    pad = jnp.zeros((8 - N_EXPERT_GROUPS, tm), F32)
    before = jnp.dot(jnp.concatenate(member + [pad], axis=0).astype(BF16), tri_ref[...],
                     preferred_element_type=F32)
    off = off_ref[0]
    dest = jnp.zeros((1, tm), F32)
    for gi in range(N_EXPERT_GROUPS):
        start = jnp.concatenate([off[gi:gi + 1, :]] * (tm // LANES), axis=1)
        dest = dest + member[gi] * (start + before[gi:gi + 1, :])
    dest_ref[...] = dest.astype(jnp.int32)


def _dispatch_kernel(dest_ref, hx_ref, init_ref, xs_ref, sem):
    del init_ref
    n = hx_ref.shape[0]

    def row_copy(r, d):
        return pltpu.make_async_copy(hx_ref.at[pl.ds(r, 1)], xs_ref.at[pl.ds(d, 1)], sem)

    def start(r, carry):
        row_copy(r, dest_ref[0, 0, r]).start()
        return carry

    def wait(r, carry):
        row_copy(r, dest_ref[0, 0, r]).wait()
        return carry

    lax.fori_loop(0, n, start, 0, unroll=8)
    lax.fori_loop(0, n, wait, 0, unroll=8)


def _moe_kernel(grp_ref, xs_ref, wg_ref, wu_ref, wd_ref, y_ref, acc_ref):
    del grp_ref
    u = pl.program_id(1)
    d_model = y_ref.shape[1]

    @pl.when(u == 0)
    def _():
        acc_ref[...] = jnp.zeros_like(acc_ref)

    h = xs_ref[:, 0:d_model].astype(BF16)
    gates = xs_ref[:, d_model:d_model + LANES]
    lane = lax.broadcasted_iota(jnp.int32, gates.shape, 1)
    gcol = jnp.sum(jnp.where(lane == u, gates, 0.0), axis=-1, keepdims=True)
    hg = jnp.dot(h, wg_ref[0], preferred_element_type=F32)
    hu = jnp.dot(h, wu_ref[0], preferred_element_type=F32)
    a = (hg * jax.nn.sigmoid(hg)) * hu
    acc_ref[...] += _dot(gcol * a, wd_ref[0])

    @pl.when(u == pl.num_programs(1) - 1)
    def _():
        y_ref[...] = acc_ref[...]


def _combine_kernel(dest_ref, y_ref, xn_ref, mod_ref, o_ref, buf, sem):
    n = xn_ref.shape[0]

    def row_copy(r, d):
        return pltpu.make_async_copy(y_ref.at[pl.ds(d, 1)], buf.at[pl.ds(r, 1)], sem)

    def start(r, carry):
        row_copy(r, dest_ref[0, 0, r]).start()
        return carry

    def wait(r, carry):
        row_copy(r, dest_ref[0, 0, r]).wait()
        return carry

    lax.fori_loop(0, n, start, 0, unroll=8)
    lax.fori_loop(0, n, wait, 0, unroll=8)
    o_ref[...] = xn_ref[...] + mod_ref[0][5:6] * buf[...]


def _decay_tables():
    H, C, d = RET_HEADS, CHUNK, HEAD_DIM
    log_gamma = jnp.log1p(-jnp.power(2.0, -5.0 - jnp.arange(H, dtype=F32)))
    idx = jnp.arange(C, dtype=F32)
    diff = idx[:, None] - idx[None, :]
    d_in = jnp.where(diff >= 0, jnp.exp(jnp.maximum(diff, 0.0)[None] * log_gamma[:, None, None]), 0.0).astype(F32)
    d_q = jnp.exp((idx + 1.0)[:, None] * log_gamma[None]).astype(F32)
    d_k = jnp.exp((C - 1.0 - idx)[:, None] * log_gamma[None]).astype(F32)
    d_c = jnp.exp(C * log_gamma).astype(F32)
    bcast = lambda a: jnp.broadcast_to(a.T[:, :, None], (H, C, d))
    return d_in, bcast(d_q), bcast(d_k), jnp.broadcast_to(d_c[:, None, None], (H, d, d))


def _gate_expand():
    m = np.zeros((LANES, N_BRANCH * NSA_W), np.float32)
    for h in range(NSA_HEADS):
        for br in range(N_BRANCH):
            m[h * N_BRANCH + br, br * NSA_W + h * HEAD_DIM: br * NSA_W + (h + 1) * HEAD_DIM] = 1.0
    return jnp.asarray(m, BF16)


def _block_expand(seq):
    m = (np.arange(LANES)[:, None] == (np.arange(seq) // SEL_BLOCK)[None, :]).astype(np.float32)
    return jnp.asarray(m, BF16)


def _compress_weights(pe, w1, w2):
    G, d = NSA_KV_HEADS, HEAD_DIM
    hid = w1.shape[-1]
    eye = jnp.eye(G, dtype=F32)
    w1h = w1.reshape(2, CMP_STRIDE, d, hid)
    w1b = jnp.einsum('pldh,ge->plgdeh', w1h, eye).reshape(2, CMP_ROW, G * hid)
    w2b = jnp.einsum('hd,ge->ghed', w2, eye).reshape(G * hid, G * d)
    peb = jnp.broadcast_to(pe.reshape(2, CMP_STRIDE, 1, d), (2, CMP_STRIDE, G, d)).reshape(2, CMP_ROW)
    return peb, w1b.astype(BF16), w2b.astype(BF16)


def kernel(x, c, positions, ada_w, ada_b, norm_mix_g, norm_ffn_g, w_in, w_out, ret_norm_g, gmlp_ln_g, gmlp_ws,
           gmlp_b, nsa_q_norm_g, nsa_k_norm_g, cmp_pe_k, cmp_pe_v, cmp_w1_k, cmp_w2_k, cmp_w1_v, cmp_w2_v,
           router_w, router_b, moe_w_gate, moe_w_up, moe_w_down):
    B, S, D = x.shape
    L = ada_w.shape[0]
    E = router_w.shape[1]
    DE = moe_w_gate.shape[-1]
    T = B * S
    n_chunk = S // CHUNK
    n_cmp = S // CMP_STRIDE
    n_sel = S // SEL_BLOCK
    top_n = min(SEL_TOPK, n_sel)
    n_qt = S // Q_TILE
    tm = min(512, S)
    tm_moe = min(1024, S)
    assert S % KEY_CHUNK == 0 and S >= WINDOW + Q_TILE and D == RET_W + GMLP_W + NSA_W
    assert w_in.shape[-1] == N_IN and T % tm_moe == 0 and n_sel <= LANES and NSA_KV_HEADS == 2

    half = HEAD_DIM // 2
    inv_freq = jnp.power(ROPE_THETA, -jnp.arange(half, dtype=F32) / half)
    invf = jnp.tile(inv_freq, LANES // half)[None, :]
    cos_t, sin_t = pl.pallas_call(
        _rope_table_kernel, name="rope_tables", grid=(B,),
        in_specs=[pl.BlockSpec((1, S, 1), lambda b: (b, 0, 0)), pl.BlockSpec((1, LANES), lambda b: (0, 0))],
        out_specs=[pl.BlockSpec((1, S, LANES), lambda b: (b, 0, 0))] * 2,
        out_shape=[jax.ShapeDtypeStruct((B, S, LANES), F32)] * 2,
        compiler_params=_cparams("parallel"),
    )(positions[:, :, None], invf)
    pad_c = lambda a: jnp.pad(a[:, CMP_LEN - 1::CMP_STRIDE], ((0, 0), (0, 1), (0, 0)))
    cos_c, sin_c = pad_c(cos_t), pad_c(sin_t)
    cos_f, sin_f = cos_t.reshape(T, LANES), sin_t.reshape(T, LANES)

    mod = pl.pallas_call(
        _ada_kernel, name="ada_mod", grid=(L, 6),
        in_specs=[pl.BlockSpec((B, D), lambda l, j: (0, 0)),
                  pl.BlockSpec((1, D, D), lambda l, j: (l, 0, j)),
                  pl.BlockSpec((1, 1, 1, D), lambda l, j: (l, j, 0, 0))],
        out_specs=pl.BlockSpec((1, 1, B, D), lambda l, j: (l, j, 0, 0)),
        out_shape=jax.ShapeDtypeStruct((L, 6, B, D), F32),
        compiler_params=_cparams("parallel", "parallel"),
    )(c, ada_w, ada_b.reshape(L, 6, 1, D))
    mod = mod.transpose(0, 2, 1, 3)

    d_in, d_q, d_k, d_c = _decay_tables()
    gexp = _gate_expand()
    expand = _block_expand(S)
    rw_hi = router_w.T.astype(BF16)
    rw_lo = (router_w.T - rw_hi.astype(F32)).astype(BF16)
    rwt = jnp.stack([rw_hi, rw_lo])
    rb = router_b.reshape(E, 1)
    tri = jnp.asarray(np.triu(np.ones((tm, tm), np.float32), k=1), BF16)
    x2 = x.reshape(T, D)
    tiles_per_batch = S // tm
    row_spec = lambda w: pl.BlockSpec((tm, w), lambda i: (i, 0))
    full = lambda shape: pl.BlockSpec(shape, lambda *_: (0,) * len(shape))

    for l in range(L):
        mod_l = mod[l]
        w_in_p = jnp.pad(w_in[l], ((0, 0), (0, N_IN_PAD - N_IN))).astype(BF16)
        in_widths = tuple(hi - lo for lo, hi in _IN_SPLITS)
        zr, zg, zq, zkc, zvc, zkv, zgl = pl.pallas_call(
            _in_proj_kernel, name="in_proj", grid=(T // tm,),
            in_specs=[row_spec(D), pl.BlockSpec((1, 6, D), lambda i: (i // tiles_per_batch, 0, 0)),
                      full((1, D)), full((D, N_IN_PAD))],
            out_specs=[row_spec(w) for w in in_widths],
            out_shape=[jax.ShapeDtypeStruct((T, w), F32) for w in in_widths],
            compiler_params=_cparams("parallel"),
        )(x2, mod_l, norm_mix_g[l][None, :], w_in_p)

        crow = lambda w: pl.BlockSpec((CHUNK, w), lambda b, cc: (b * n_chunk + cc, 0))
        y_ret, y_gm = pl.pallas_call(
            _mixer_kernel, name="ret_gmlp", grid=(B, n_chunk),
            in_specs=[crow(4 * RET_W), crow(2 * GMLP_W),
                      pl.BlockSpec((1, CHUNK, LANES), lambda b, cc: (b, cc, 0)),
                      pl.BlockSpec((1, CHUNK, LANES), lambda b, cc: (b, cc, 0)),
                      full((RET_HEADS, CHUNK, CHUNK)), full((RET_HEADS, CHUNK, HEAD_DIM)),
                      full((RET_HEADS, CHUNK, HEAD_DIM)), full((RET_HEADS, HEAD_DIM, HEAD_DIM)),
                      full((RET_HEADS, 1, HEAD_DIM)), full((1, GMLP_W)),
                      full((GMLP_GROUPS, CHUNK, CHUNK)), full((GMLP_GROUPS, CHUNK, HEAD_DIM))],
            out_specs=[crow(RET_W), crow(GMLP_W)],
            out_shape=[jax.ShapeDtypeStruct((T, RET_W), BF16), jax.ShapeDtypeStruct((T, GMLP_W), BF16)],
            scratch_shapes=[pltpu.VMEM((RET_HEADS, HEAD_DIM, HEAD_DIM), F32)],
            compiler_params=_cparams("parallel", "arbitrary"),
        )(zr, zg, cos_t, sin_t, d_in, d_q, d_k, d_c, ret_norm_g[l][:, None, :], gmlp_ln_g[l][None, :],
          gmlp_ws[l], jnp.broadcast_to(gmlp_b[l][:, :, None], (GMLP_GROUPS, CHUNK, HEAD_DIM)))

        kcol = lambda j: pl.BlockSpec((tm, KV_W), lambda i: (i, j))
        kt_out = pl.BlockSpec((1, KV_W, tm), lambda i: (i // tiles_per_batch, 0, i % tiles_per_batch))
        kgain = jnp.tile(nsa_k_norm_g[l], (1, NSA_KV_HEADS))
        q_r, ks_t, vs_b, kw_t, vw_b = pl.pallas_call(
            _nsa_prep_kernel, name="nsa_prep", grid=(T // tm,),
            in_specs=[row_spec(NSA_W), kcol(0), kcol(1), kcol(2), kcol(3), row_spec(LANES), row_spec(LANES),
                      full((1, NSA_W)), full((N_BRANCH, KV_W))],
            out_specs=[row_spec(NSA_W), kt_out, row_spec(KV_W), kt_out, row_spec(KV_W)],
            out_shape=[jax.ShapeDtypeStruct((T, NSA_W), BF16), jax.ShapeDtypeStruct((B, KV_W, S), BF16),
                       jax.ShapeDtypeStruct((T, KV_W), BF16), jax.ShapeDtypeStruct((B, KV_W, S), BF16),
                       jax.ShapeDtypeStruct((T, KV_W), BF16)],
            compiler_params=_cparams("parallel"),
        )(zq, zkv, zkv, zkv, zkv, cos_f, sin_f, jnp.tile(nsa_q_norm_g[l], NSA_HEADS)[None, :], kgain)

        pek, w1k, w2k = _compress_weights(cmp_pe_k[l], cmp_w1_k[l], cmp_w2_k[l])
        pev, w1v, w2v = _compress_weights(cmp_pe_v[l], cmp_w1_v[l], cmp_w2_v[l])
        cblk = pl.BlockSpec((n_cmp, CMP_ROW), lambda b: (b, 0))
        ghid = w1k.shape[-1]
        ctab = pl.BlockSpec((1, n_cmp, LANES), lambda b: (b, 0, 0))
        kc, vc = pl.pallas_call(
            _compress_kernel, name="nsa_compress", grid=(B,),
            in_specs=[cblk, cblk, full((2, CMP_ROW)), full((2, CMP_ROW)),
                      full((2, CMP_ROW, ghid)), full((ghid, KV_W)), full((2, CMP_ROW, ghid)), full((ghid, KV_W)),
                      full((N_BRANCH, KV_W)), ctab, ctab],
            out_specs=[ctab, ctab],
            out_shape=[jax.ShapeDtypeStruct((B, n_cmp, KV_W), F32)] * 2,
            compiler_params=_cparams("parallel"),
        )(zkc.reshape(T // CMP_STRIDE, CMP_ROW), zvc.reshape(T // CMP_STRIDE, CMP_ROW), pek, pev, w1k, w2k, w1v, w2v,
          kgain, cos_c, sin_c)

        qrow = lambda w: pl.BlockSpec((Q_TILE, w), lambda b, i: (b * n_qt + i, 0))
        kt_in = pl.BlockSpec((1, KV_W, S), lambda b, i: (b, 0, 0))
        v_in = pl.BlockSpec((S, KV_W), lambda b, i: (b, 0))
        ctab2 = pl.BlockSpec((1, n_cmp, KV_W), lambda b, i: (b, 0, 0))
        y_nsa = pl.pallas_call(
            functools.partial(_nsa_kernel, n_sel=n_sel, top_n=top_n), name="nsa_attn", grid=(B, n_qt),
            in_specs=[qrow(NSA_W), qrow(LANES), ctab2, ctab2, kt_in, v_in, kt_in, v_in,
                      full((LANES, N_BRANCH * NSA_W)), full((LANES, S))],
            out_specs=qrow(NSA_W),
            out_shape=jax.ShapeDtypeStruct((T, NSA_W), BF16),
            scratch_shapes=[pltpu.VMEM((NSA_GROUP * Q_TILE, WINDOW + Q_TILE), F32),
                            pltpu.VMEM((NSA_GROUP * Q_TILE, WINDOW + Q_TILE), BF16),
                            pltpu.VMEM((Q_TILE, WINDOW + Q_TILE), F32),
                            pltpu.VMEM((NSA_KV_HEADS, NSA_GROUP * Q_TILE, LANES), F32),
                            pltpu.VMEM((NSA_GROUP * Q_TILE, LANES), F32),
                            pltpu.VMEM((NSA_KV_HEADS, NSA_GROUP * Q_TILE, KV_W), F32)],
            compiler_params=_cparams("parallel", "parallel"),
        )(q_r, zgl, kc, vc, ks_t, vs_b, kw_t, vw_b, gexp, expand)

        n_tok_tiles = T // tm
        DX = D + LANES
        x_new, hx, bgrp, cnt = pl.pallas_call(
            functools.partial(_out_proj_kernel, n_experts=E), name="out_proj_router", grid=(n_tok_tiles,),
            in_specs=[row_spec(RET_W), row_spec(GMLP_W), row_spec(NSA_W), row_spec(D),
                      pl.BlockSpec((1, 6, D), lambda i: (i // tiles_per_batch, 0, 0)),
                      full((D, D)), full((1, D)), full((2, E, D)), full((E, 1))],
            out_specs=[row_spec(D), row_spec(DX), pl.BlockSpec((1, tm), lambda i: (0, i)),
                       pl.BlockSpec((1, N_EXPERT_GROUPS, LANES), lambda i: (i, 0, 0))],
            out_shape=[jax.ShapeDtypeStruct((T, D), F32), jax.ShapeDtypeStruct((T, DX), F32),
                       jax.ShapeDtypeStruct((1, T), jnp.int32),
                       jax.ShapeDtypeStruct((n_tok_tiles, N_EXPERT_GROUPS, LANES), F32)],
            compiler_params=_cparams("parallel"),
        )(y_ret, y_gm, y_nsa, x2, mod_l, w_out[l].astype(BF16), norm_ffn_g[l][None, :], rwt, rb)

        tile_cnt = cnt[:, :, 0].astype(jnp.int32)
        seg = ((jnp.sum(tile_cnt, axis=0) + tm_moe - 1) // tm_moe) * tm_moe
        earlier = lambda n: jnp.arange(n)[:, None] > jnp.arange(n)[None, :]
        seg_start = jnp.sum(jnp.where(earlier(N_EXPERT_GROUPS), seg[None, :], 0), axis=1)
        tile_off = seg_start[None, :] + jnp.sum(
            jnp.where(earlier(n_tok_tiles)[:, :, None], tile_cnt[None, :, :], 0), axis=1)
        n_row_tiles = T // tm_moe + N_EXPERT_GROUPS
        row0 = jnp.arange(n_row_tiles) * tm_moe
        tile_grp = jnp.minimum(jnp.sum(((seg_start + seg)[None, :] <= row0[:, None]).astype(jnp.int32), axis=1),
                               N_EXPERT_GROUPS - 1)
        dest = pl.pallas_call(
            _route_kernel, name="moe_route", grid=(n_tok_tiles,),
            in_specs=[pl.BlockSpec((1, tm), lambda i: (0, i)),
                      pl.BlockSpec((1, N_EXPERT_GROUPS, LANES), lambda i: (i, 0, 0)), full((tm, tm))],
            out_specs=pl.BlockSpec((1, tm), lambda i: (0, i)),
            out_shape=jax.ShapeDtypeStruct((1, T), jnp.int32),
            compiler_params=_cparams("parallel"),
        )(bgrp, jnp.broadcast_to(tile_off.astype(F32)[:, :, None], (n_tok_tiles, N_EXPERT_GROUPS, LANES)), tri)
        dest3 = dest.reshape(n_tok_tiles, 1, tm)
        dest_spec = pl.BlockSpec((1, 1, tm), lambda i: (i, 0, 0), memory_space=pltpu.SMEM)

        n_rows = n_row_tiles * tm_moe
        xs = pl.pallas_call(
            _dispatch_kernel, name="moe_dispatch", grid=(n_tok_tiles,),
            in_specs=[dest_spec, row_spec(DX), pl.BlockSpec(memory_space=pl.ANY)],
            out_specs=pl.BlockSpec(memory_space=pl.ANY),
            out_shape=jax.ShapeDtypeStruct((n_rows, DX), F32),
            scratch_shapes=[pltpu.SemaphoreType.DMA(())],
            input_output_aliases={2: 0},
            compiler_params=_cparams("arbitrary"),
        )(dest3, hx, jnp.zeros((n_rows, DX), F32))

        per = E // N_EXPERT_GROUPS
        ys = pl.pallas_call(
            _moe_kernel, name="moe",
            grid_spec=pltpu.PrefetchScalarGridSpec(
                num_scalar_prefetch=1, grid=(n_row_tiles, per),
                in_specs=[pl.BlockSpec((tm_moe, DX), lambda k, u, grp: (k, 0)),
                          pl.BlockSpec((1, D, DE), lambda k, u, grp: (grp[k] * per + u, 0, 0)),
                          pl.BlockSpec((1, D, DE), lambda k, u, grp: (grp[k] * per + u, 0, 0)),
                          pl.BlockSpec((1, DE, D), lambda k, u, grp: (grp[k] * per + u, 0, 0))],
                out_specs=pl.BlockSpec((tm_moe, D), lambda k, u, grp: (k, 0)),
                scratch_shapes=[pltpu.VMEM((tm_moe, D), F32)]),
            out_shape=jax.ShapeDtypeStruct((n_rows, D), F32),
            compiler_params=_cparams("parallel", "arbitrary"),
        )(tile_grp, xs, moe_w_gate[l].astype(BF16), moe_w_up[l].astype(BF16), moe_w_down[l].astype(BF16))

        x2 = pl.pallas_call(
            _combine_kernel, name="moe_combine", grid=(n_tok_tiles,),
            in_specs=[dest_spec, pl.BlockSpec(memory_space=pl.ANY), row_spec(D),
                      pl.BlockSpec((1, 6, D), lambda i: (i // tiles_per_batch, 0, 0))],
            out_specs=row_spec(D),
            out_shape=jax.ShapeDtypeStruct((T, D), F32),
            scratch_shapes=[pltpu.VMEM((tm, D), F32), pltpu.SemaphoreType.DMA(())],
            compiler_params=_cparams("arbitrary"),
        )(dest3, ys, x_new, mod_l)

    return x2.reshape(B, S, D)
```

```python
import functools

import numpy as np
import jax
import jax.numpy as jnp
from jax import lax
from jax.experimental import pallas as pl
from jax.experimental.pallas import tpu as pltpu

F32 = jnp.float32
BF16 = jnp.bfloat16

HEAD_DIM = 64
LANES = 128
RET_HEADS = 4
GMLP_GROUPS = 4
NSA_HEADS = 8
NSA_KV_HEADS = 2
NSA_GROUP = NSA_HEADS // NSA_KV_HEADS
N_BRANCH = 3
CHUNK = 128
CMP_LEN = 32
CMP_STRIDE = 16
SEL_BLOCK = 64
SEL_TOPK = 8
WINDOW = 512
ROPE_THETA = 10000.0
N_EXPERT_GROUPS = 4
EXPERT_TOPK = 2
NORM_EPS = 1e-6
NEG_INF = -1e30
FORCE_SCORE = 1e4

RET_W = RET_HEADS * HEAD_DIM
GMLP_W = GMLP_GROUPS * HEAD_DIM
NSA_W = NSA_HEADS * HEAD_DIM
KV_W = NSA_KV_HEADS * HEAD_DIM
GATE_W = NSA_HEADS * N_BRANCH
CMP_ROW = CMP_STRIDE * KV_W
OFF_RET = 0
OFF_GMLP = 4 * RET_W
OFF_Q = OFF_GMLP + 2 * GMLP_W
OFF_KV = OFF_Q + NSA_W
OFF_GL = OFF_KV + 6 * KV_W
N_IN = OFF_GL + GATE_W
N_IN_PAD = OFF_GL + LANES

Q_TILE = 128
KEY_CHUNK = 512
MIX_BATCH = 2
ROW_BLOCK = 32
ROW_UNROLL = True
VMEM_LIMIT = 48 * 1024 * 1024


def _cparams(*sem):
    return pltpu.CompilerParams(dimension_semantics=sem, vmem_limit_bytes=VMEM_LIMIT)


def _dot(a, b):
    return jnp.dot(a.astype(BF16), b.astype(BF16), preferred_element_type=F32)


def _dot_nt(a, b):
    return lax.dot_general(a.astype(BF16), b.astype(BF16), (((1,), (1,)), ((), ())),
                           preferred_element_type=F32)


def _split(a):
    hi = a.astype(BF16)
    lo = (a - hi.astype(F32)).astype(BF16)
    return hi, lo


def _dot_split_lhs(a, b):
    hi, lo = _split(a)
    return (jnp.dot(hi, b, preferred_element_type=F32) + jnp.dot(lo, b, preferred_element_type=F32))


def _dot_split_rhs(a, b):
    hi, lo = _split(b)
    return (jnp.dot(a, hi, preferred_element_type=F32) + jnp.dot(a, lo, preferred_element_type=F32))


def _head_ones(width=LANES):
    r = lax.broadcasted_iota(jnp.int32, (width, width), 0) // HEAD_DIM
    c = lax.broadcasted_iota(jnp.int32, (width, width), 1) // HEAD_DIM
    return jnp.where(r == c, 1.0, 0.0).astype(BF16)


def _rotate_half(y):
    lane = lax.broadcasted_iota(jnp.int32, y.shape, 1)
    first = (lane & (HEAD_DIM - 1)) < (HEAD_DIM // 2)
    return jnp.where(first, -pltpu.roll(y, LANES - HEAD_DIM // 2, 1), pltpu.roll(y, HEAD_DIM // 2, 1))


def _rope_slab(y, cos, sin):
    return y * cos + _rotate_half(y) * sin


def _rope(x, cos, sin):
    return jnp.concatenate(
        [_rope_slab(x[:, k * LANES:(k + 1) * LANES], cos, sin) for k in range(x.shape[1] // LANES)], axis=1)


def _head_rms_rope(x, gain, cos, sin, ones):
    outs = []
    for k in range(x.shape[1] // LANES):
        xs = x[:, k * LANES:(k + 1) * LANES]
        ssq = _dot_split_lhs(xs * xs, ones)
        y = xs * lax.rsqrt(ssq * (1.0 / HEAD_DIM) + NORM_EPS) * gain[:, k * LANES:(k + 1) * LANES]
        outs.append(_rope_slab(y, cos, sin))
    return outs[0] if len(outs) == 1 else jnp.concatenate(outs, axis=1)


def _rms_mod(x, gain, scale, shift):
    y = x * lax.rsqrt(jnp.mean(x * x, axis=-1, keepdims=True) + NORM_EPS)
    return (y * gain) * (1.0 + scale) + shift


def _softmax_rows(s):
    e = jnp.exp(s - jnp.max(s, axis=-1, keepdims=True))
    return e / jnp.sum(e, axis=-1, keepdims=True)


def _mask_bias(keep):
    return jnp.where(keep, 0.0, NEG_INF)


def _rope_table_kernel(pos_ref, invf_ref, cos_ref, sin_ref):
    ang = pos_ref[0].astype(F32) * invf_ref[...]
    cos_ref[0] = jnp.cos(ang)
    sin_ref[0] = jnp.sin(ang)


def _ada_kernel(c_ref, w_ref, b_ref, o_ref):
    c = c_ref[...]
    a = c * jax.nn.sigmoid(c)
    o_ref[0, 0] = jnp.dot(a, w_ref[0], preferred_element_type=F32,
                          precision=lax.Precision.HIGHEST) + b_ref[0, 0]


_IN_RAW = ((OFF_RET, OFF_GMLP), (OFF_GMLP, OFF_Q), (OFF_KV, OFF_KV + KV_W), (OFF_KV + KV_W, OFF_KV + 2 * KV_W),
           (OFF_GL, N_IN_PAD))


def _in_proj_kernel(*refs, fuse_residual):
    if fuse_residual:
        xn_ref, y_ref, modp_ref, *refs = refs
        x = xn_ref[...] + modp_ref[0][5:6] * y_ref[...]
    else:
        x_ref, *refs = refs
        x = x_ref[...]
    mod_ref, g_ref, w_ref, cos_ref, sin_ref, qg_ref, kg_ref, *out_refs = refs
    if fuse_residual:
        out_refs[0][...] = x
        out_refs = out_refs[1:]
    *raw_refs, q_out, kst_out, vs_out, kwt_out, vw_out = out_refs
    mod = mod_ref[0]
    h = _rms_mod(x, g_ref[...], mod[1:2], mod[0:1]).astype(BF16)
    proj = lambda lo, hi: jnp.dot(h, w_ref[:, lo:hi], preferred_element_type=F32)
    for ref, (lo, hi) in zip(raw_refs, _IN_RAW):
        ref[...] = proj(lo, hi)

    ones = _head_ones()
    cos, sin = cos_ref[...], sin_ref[...]
    q_out[...] = (_head_rms_rope(proj(OFF_Q, OFF_KV), qg_ref[...], cos, sin, ones) * (HEAD_DIM ** -0.5)).astype(BF16)
    sel_kv = OFF_KV + 2 * KV_W
    kst_out[0] = _head_rms_rope(proj(sel_kv, sel_kv + KV_W), kg_ref[1:2], cos, sin, ones).T.astype(BF16)
    vs_out[...] = proj(sel_kv + KV_W, sel_kv + 2 * KV_W).astype(BF16)
    kwt_out[0] = _head_rms_rope(proj(sel_kv + 2 * KV_W, sel_kv + 3 * KV_W), kg_ref[2:3], cos, sin, ones).T.astype(BF16)
    vw_out[...] = proj(sel_kv + 3 * KV_W, OFF_GL).astype(BF16)


def _mixer_kernel(zr_ref, zg_ref, cos_ref, sin_ref, din_ref, dq_ref, dk_ref, dc_ref, rg_ref,
                  lng_ref, ws_ref, bs_ref, yret_ref, ygm_ref, state_ref):
    @pl.when(pl.program_id(1) == 0)
    def _():
        state_ref[...] = jnp.zeros_like(state_ref)

    row = lax.broadcasted_iota(jnp.int32, (CHUNK, CHUNK), 0)
    col = lax.broadcasted_iota(jnp.int32, (CHUNK, CHUNK), 1)
    ws = [jnp.where(row >= col, ws_ref[gi], 0.0).astype(BF16) for gi in range(GMLP_GROUPS)]

    for bb in range(zr_ref.shape[0]):
        cos, sin = cos_ref[bb], sin_ref[bb]
        q = _rope(zr_ref[bb, :, 0:RET_W], cos, sin)
        k = _rope(zr_ref[bb, :, RET_W:2 * RET_W], cos, sin) * (HEAD_DIM ** -0.5)
        v = zr_ref[bb, :, 2 * RET_W:3 * RET_W]
        g = zr_ref[bb, :, 3 * RET_W:4 * RET_W]
        outs = []
        for h in range(RET_HEADS):
            sl = slice(h * HEAD_DIM, (h + 1) * HEAD_DIM)
            qh, kh, vh = q[:, sl], k[:, sl], v[:, sl]
            st = state_ref[bb, h]
            inner = _dot_nt(qh, kh) * din_ref[h]
            o = _dot(inner, vh) + _dot(qh, st) * dq_ref[h]
            state_ref[bb, h] = st * dc_ref[h] + _dot((kh * dk_ref[h]).T, vh)
            o = o * lax.rsqrt(jnp.mean(o * o, axis=-1, keepdims=True) + NORM_EPS) * rg_ref[h]
            outs.append(o)
        yret_ref[bb] = (jnp.concatenate(outs, axis=1) * (g * jax.nn.sigmoid(g))).astype(BF16)

        u = jax.nn.gelu(zg_ref[bb, :, 0:GMLP_W])
        vv = jax.nn.gelu(zg_ref[bb, :, GMLP_W:2 * GMLP_W])
        mu = jnp.mean(vv, axis=-1, keepdims=True)
        var = jnp.mean(jnp.square(vv - mu), axis=-1, keepdims=True)
        vn = (vv - mu) * lax.rsqrt(var + NORM_EPS) * lng_ref[...]
        ss = [_dot(ws[gi], vn[:, gi * HEAD_DIM:(gi + 1) * HEAD_DIM]) + bs_ref[gi] for gi in range(GMLP_GROUPS)]
        ygm_ref[bb] = (u * jnp.concatenate(ss, axis=1)).astype(BF16)


def _compress_kernel(xk_ref, xv_ref, pek_ref, pev_ref, w1k_ref, w2k_ref, w1v_ref, w2v_ref, kg_ref,
                     cos_ref, sin_ref, kc_ref, vc_ref):
    n_rows = xk_ref.shape[0]

    def comp(x, pe_ref, w1_ref, w2_ref):
        a = _dot(x + pe_ref[0:1], w1_ref[0])
        b = _dot(x + pe_ref[1:2], w1_ref[1])
        h = a + pltpu.roll(b, n_rows - 1, 0)
        return _dot(jax.nn.gelu(h), w2_ref[...])

    kc_ref[0] = _head_rms_rope(comp(xk_ref[...], pek_ref, w1k_ref, w2k_ref), kg_ref[0:1],
                               cos_ref[0], sin_ref[0], _head_ones())
    vc_ref[0] = comp(xv_ref[...], pev_ref, w1v_ref, w2v_ref)


def _nsa_kernel(q_ref, gl_ref, kc_ref, vc_ref, kst_ref, vs_ref, kwt_ref, vw_ref, gexp_ref, expand_ref, o_ref,
                s_scr, p_scr, m_scr, acc_scr, *, n_sel, top_n):
    i = pl.program_id(1)
    qt = q_ref.shape[0]
    n_cmp = kc_ref.shape[1]
    t0 = i * qt
    n_chunks = (t0 + qt + KEY_CHUNK - 1) // KEY_CHUNK
    wk = WINDOW + qt
    wstart = pl.multiple_of(jnp.maximum(t0 - WINDOW, 0), LANES)

    t_r = t0 + lax.broadcasted_iota(jnp.int32, (qt, n_cmp), 0)
    n_c = lax.broadcasted_iota(jnp.int32, (qt, n_cmp), 1)
    keep_c = jnp.where(n_c * CMP_STRIDE + (CMP_LEN - 1) <= t_r, 1.0, 0.0)
    bias_c = (keep_c - 1.0) * (-NEG_INF)
    t_l = t0 + lax.broadcasted_iota(jnp.int32, (n_cmp, qt), 1)
    n_s = lax.broadcasted_iota(jnp.int32, (n_cmp, qt), 0)
    keep_ct = jnp.where(n_s * CMP_STRIDE + (CMP_LEN - 1) <= t_l, 1.0, 0.0)
    keep_ct4 = jnp.concatenate([keep_ct] * NSA_GROUP, axis=1)
    bias_ct4 = (keep_ct4 - 1.0) * (-NEG_INF)
    jo = lax.broadcasted_iota(jnp.int32, (n_sel, n_cmp), 0) * SEL_BLOCK
    no = lax.broadcasted_iota(jnp.int32, (n_sel, n_cmp), 1) * CMP_STRIDE
    overlap = jnp.where((no < jo + SEL_BLOCK) & (no + CMP_LEN > jo), 1.0, 0.0).astype(BF16)
    j = lax.broadcasted_iota(jnp.int32, (n_sel, qt), 0)
    tq = t0 + lax.broadcasted_iota(jnp.int32, (n_sel, qt), 1)
    cur = tq // SEL_BLOCK
    forced = (j == 0) | (j == cur) | (j == cur - 1)
    allowed = j * SEL_BLOCK <= tq
    t_w = t0 + lax.broadcasted_iota(jnp.int32, (qt, wk), 0)
    s_w = wstart + lax.broadcasted_iota(jnp.int32, (qt, wk), 1)
    bias_w = _mask_bias((s_w <= t_w) & (t_w - s_w < WINDOW))
    t_k = t0 + lax.broadcasted_iota(jnp.int32, (qt, KEY_CHUNK), 0)
    s_k = lax.broadcasted_iota(jnp.int32, (qt, KEY_CHUNK), 1)

    rep = lambda a: jnp.concatenate([a] * NSA_GROUP, axis=0)
    keep_c4, bias_c4 = rep(keep_c), rep(bias_c)
    unstack = lambda o, gs: [o[r * qt:(r + 1) * qt, gs] for r in range(NSA_GROUP)]
    rows = NSA_GROUP * qt
    lane_v = lax.broadcasted_iota(jnp.int32, (1, KV_W), 1)

    def attend(g, qs, kt, vv, bias, first):
        width = bias.shape[1]
        own = (lane_v >= g * HEAD_DIM) & (lane_v < (g + 1) * HEAD_DIM)
        vv1 = jnp.where(own, vv, jnp.ones_like(vv))
        s = jnp.dot(qs, kt, preferred_element_type=F32) + rep(bias)
        s_scr[g, :, 0:width] = s
        mx = jnp.max(s, axis=-1, keepdims=True)
        if first:
            m_scr[g] = jnp.broadcast_to(mx, (rows, LANES))
        else:
            m_old = m_scr[g]
            m_new = jnp.maximum(m_old, mx)
            alpha = jnp.exp(m_old - m_new)
            m_scr[g] = m_new

        def exp_pass(b, carry):
            r0 = pl.multiple_of(b * ROW_BLOCK, ROW_BLOCK)
            m_new = m_scr[g, pl.ds(r0, ROW_BLOCK), :]
            p = jnp.exp(s_scr[g, pl.ds(r0, ROW_BLOCK), 0:width]
                        - jnp.concatenate([m_new] * (width // LANES), axis=1))
            p_scr[g, pl.ds(r0, ROW_BLOCK), 0:width] = p.astype(BF16)
            return carry

        lax.fori_loop(0, rows // ROW_BLOCK, exp_pass, 0, unroll=ROW_UNROLL)
        pv = jnp.dot(p_scr[g, :, 0:width], vv1, preferred_element_type=F32)
        acc_scr[g] = pv if first else alpha * acc_scr[g] + pv

    def finish(g):
        acc = acc_scr[g]
        num, den = acc[:, g * HEAD_DIM:(g + 1) * HEAD_DIM], acc[:, (1 - g) * HEAD_DIM:(2 - g) * HEAD_DIM]
        return unstack(num / den, slice(None))

    o_cmp, o_win, qs_g, sel_g = [], [], [], []
    for g in range(NSA_KV_HEADS):
        gs = slice(g * HEAD_DIM, (g + 1) * HEAD_DIM)
        qs = jnp.concatenate([q_ref[:, (g * NSA_GROUP + r) * HEAD_DIM:(g * NSA_GROUP + r + 1) * HEAD_DIM]
                              for r in range(NSA_GROUP)], axis=0)
        qs_g.append(qs)
        kc = kc_ref[0][:, gs].astype(BF16)
        vc = vc_ref[0].astype(BF16)

        p = _softmax_rows(_dot_nt(qs, kc) + bias_c4) * keep_c4
        o_cmp += unstack(_dot(p, vc), gs)

        st = _dot_nt(kc, qs) + bias_ct4
        e = jnp.exp(st - jnp.max(st, axis=0, keepdims=True))
        pt = e / jnp.sum(e, axis=0, keepdims=True) * keep_ct4
        psum = pt[:, 0:qt]
        for r in range(1, NSA_GROUP):
            psum = psum + pt[:, r * qt:(r + 1) * qt]
        imp = _dot_split_rhs(overlap, psum)
        score = jnp.where(forced, FORCE_SCORE, jnp.where(allowed, imp, -1.0))
        rank = jnp.zeros((n_sel, qt), F32)
        for jp in range(n_sel):
            other = score[jp:jp + 1, :]
            tie = jnp.where(j > jp, 1.0, 0.0)
            rank = rank + jnp.where(other > score, 1.0, jnp.where(other == score, tie, 0.0))
        sel_t = jnp.where(rank < top_n, 1.0, 0.0)
        if n_sel < LANES:
            sel_t = jnp.concatenate([sel_t, jnp.zeros((LANES - n_sel, qt), F32)], axis=0)
        sel_g.append(sel_t.T.astype(BF16))

        attend(g, qs, kwt_ref[0, gs, pl.ds(wstart, wk)], vw_ref[pl.ds(wstart, wk), :], bias_w, True)
        o_win += finish(g)

    def chunk(c, first):
        k0 = pl.multiple_of(c * KEY_CHUNK, KEY_CHUNK)
        vv = vs_ref[pl.ds(k0, KEY_CHUNK), :]
        ex = expand_ref[:, pl.ds(k0, KEY_CHUNK)]
        causal = k0 + s_k <= t_k
        for g in range(NSA_KV_HEADS):
            chosen = jnp.dot(sel_g[g], ex, preferred_element_type=F32)
            attend(g, qs_g[g], kst_ref[0, g * HEAD_DIM:(g + 1) * HEAD_DIM, pl.ds(k0, KEY_CHUNK)], vv,
                   _mask_bias((chosen > 0.5) & causal), first)

    chunk(0, True)

    def body(c, carry):
        chunk(c, False)
        return carry

    lax.fori_loop(1, n_chunks, body, 0)
    o_sel = finish(0) + finish(1)

    gx = _dot_split_lhs(jax.nn.sigmoid(gl_ref[...]), gexp_ref[...])
    cat = lambda parts: jnp.concatenate(parts, axis=1)
    o_ref[...] = (gx[:, 0:NSA_W] * cat(o_cmp) + gx[:, NSA_W:2 * NSA_W] * cat(o_sel)
                  + gx[:, 2 * NSA_W:3 * NSA_W] * cat(o_win)).astype(BF16)


def _out_proj_kernel(yret_ref, ygm_ref, ynsa_ref, x_ref, mod_ref, wout_ref,
                     g2_ref, rwt_ref, rb_ref, xn_ref, hx_ref, bg_ref, cnt_ref, *, n_experts):
    mod = mod_ref[0]
    d_model = x_ref.shape[1]
    y = (jnp.dot(yret_ref[...], wout_ref[0:RET_W], preferred_element_type=F32)
         + jnp.dot(ygm_ref[...], wout_ref[RET_W:RET_W + GMLP_W], preferred_element_type=F32)
         + jnp.dot(ynsa_ref[...], wout_ref[RET_W + GMLP_W:], preferred_element_type=F32))
    xn = x_ref[...] + mod[2:3] * y
    xn_ref[...] = xn
    h2 = _rms_mod(xn, g2_ref[...], mod[4:5], mod[3:4])
    hx_ref[:, 0:d_model] = h2

    hh, hl = _split(h2)
    wh, wl = rwt_ref[0], rwt_ref[1]
    nt = lambda a, b: lax.dot_general(a, b, (((1,), (1,)), ((), ())), preferred_element_type=F32)
    logits = nt(wh, hh) + nt(wh, hl) + nt(wl, hh)
    ex = jnp.exp(logits - jnp.max(logits, axis=0, keepdims=True))
    probs = ex / jnp.sum(ex, axis=0, keepdims=True)
    sel = probs + rb_ref[...]
    per = n_experts // N_EXPERT_GROUPS
    srow = [sel[e:e + 1, :] for e in range(n_experts)]
    prow = [probs[e:e + 1, :] for e in range(n_experts)]
    gscore = []
    for gi in range(N_EXPERT_GROUPS):
        a = srow[gi * per:(gi + 1) * per]
        best = None
        for u in range(per):
            for w in range(u + 1, per):
                pair = a[u] + a[w]
                best = pair if best is None else jnp.maximum(best, pair)
        gscore.append(best)
    bg = jnp.zeros_like(gscore[0], dtype=jnp.int32)
    bs = gscore[0]
    for gi in range(1, N_EXPERT_GROUPS):
        upd = gscore[gi] > bs
        bg = jnp.where(upd, gi, bg)
        bs = jnp.where(upd, gscore[gi], bs)
    cs, cp = [], []
    for u in range(per):
        su, pu = srow[u], prow[u]
        for gi in range(1, N_EXPERT_GROUPS):
            su = jnp.where(bg == gi, srow[gi * per + u], su)
            pu = jnp.where(bg == gi, prow[gi * per + u], pu)
        cs.append(su)
        cp.append(pu)
    chosen = []
    for u in range(per):
        rank = jnp.zeros_like(cs[u])
        for w in range(per):
            if w == u:
                continue
            before = (cs[w] >= cs[u]) if w < u else (cs[w] > cs[u])
            rank = rank + jnp.where(before, 1.0, 0.0)
        chosen.append(jnp.where(rank < EXPERT_TOPK, 1.0, 0.0))
    denom = jnp.zeros_like(cp[0])
    for u in range(per):
        denom = denom + chosen[u] * cp[u]
    tm = bg.shape[1]
    grows = [chosen[u] * cp[u] / denom for u in range(per)]
    gpad = jnp.concatenate(grows + [jnp.zeros((LANES - per, tm), F32)], axis=0)
    hx_ref[:, d_model:d_model + LANES] = gpad.T
    bg_ref[...] = bg
    cnt_ref[0] = jnp.concatenate(
        [jnp.broadcast_to(jnp.sum(jnp.where(bg == gi, 1.0, 0.0), axis=1, keepdims=True), (1, LANES))
         for gi in range(N_EXPERT_GROUPS)], axis=0)


def _route_kernel(bg_ref, off_ref, tri_ref, dest_ref):
    bg = bg_ref[...]
    tm = bg.shape[1]
    member = [jnp.where(bg == gi, 1.0, 0.0) for gi in range(N_EXPERT_GROUPS)]
    pad = jnp.zeros((8 - N_EXPERT_GROUPS, tm), F32)
    before = jnp.dot(jnp.concatenate(member + [pad], axis=0).astype(BF16), tri_ref[...],
                     preferred_element_type=F32)
    off = off_ref[0]
    dest = jnp.zeros((1, tm), F32)
    for gi in range(N_EXPERT_GROUPS):
        start = jnp.concatenate([off[gi:gi + 1, :]] * (tm // LANES), axis=1)
        dest = dest + member[gi] * (start + before[gi:gi + 1, :])
    dest_ref[...] = dest.astype(jnp.int32)


def _row_move_kernel(cur_ref, src_ref, *rest, scatter):
    dst_ref, sems = rest[-2], rest[-1]
    i = pl.program_id(0)
    n = cur_ref.shape[2]

    def start(r, carry):
        tok, row = i * n + r, cur_ref[0, 0, r]
        s, d = (tok, row) if scatter else (row, tok)
        pltpu.make_async_copy(src_ref.at[pl.ds(s, 1)], dst_ref.at[pl.ds(d, 1)], sems.at[i % 2]).start()
        return carry

    def wait_tile(tile):
        def wait(r, carry):
            pltpu.make_async_copy(src_ref.at[pl.ds(0, 1)], dst_ref.at[pl.ds(0, 1)], sems.at[tile % 2]).wait()
            return carry
        lax.fori_loop(0, n, wait, 0, unroll=8)

    lax.fori_loop(0, n, start, 0, unroll=8)

    @pl.when(i > 0)
    def _():
        wait_tile(i - 1)

    @pl.when(i == pl.num_programs(0) - 1)
    def _():
        wait_tile(i)


def _moe_kernel(grp_ref, xs_ref, wg_ref, wu_ref, wd_ref, y_ref, acc_ref):
    del grp_ref
    u = pl.program_id(1)
    d_model = y_ref.shape[1]

    @pl.when(u == 0)
    def _():
        acc_ref[...] = jnp.zeros_like(acc_ref)

    h = xs_ref[:, 0:d_model].astype(BF16)
    gates = xs_ref[:, d_model:d_model + LANES]
    lane = lax.broadcasted_iota(jnp.int32, gates.shape, 1)
    gcol = jnp.sum(jnp.where(lane == u, gates, 0.0), axis=-1, keepdims=True)
    hg = jnp.dot(h, wg_ref[0].astype(BF16), preferred_element_type=F32)
    hu = jnp.dot(h, wu_ref[0].astype(BF16), preferred_element_type=F32)
    a = (hg * jax.nn.sigmoid(hg)) * hu
    acc_ref[...] += _dot(gcol * a, wd_ref[0])

    @pl.when(u == pl.num_programs(1) - 1)
    def _():
        y_ref[...] = acc_ref[...]


def _residual_kernel(xn_ref, y_ref, mod_ref, o_ref):
    o_ref[...] = xn_ref[...] + mod_ref[0][5:6] * y_ref[...]


def _decay_tables():
    H, C, d = RET_HEADS, CHUNK, HEAD_DIM
    log_gamma = jnp.log1p(-jnp.power(2.0, -5.0 - jnp.arange(H, dtype=F32)))
    idx = jnp.arange(C, dtype=F32)
    diff = idx[:, None] - idx[None, :]
    d_in = jnp.where(diff >= 0, jnp.exp(jnp.maximum(diff, 0.0)[None] * log_gamma[:, None, None]), 0.0).astype(F32)
    d_q = jnp.exp((idx + 1.0)[:, None] * log_gamma[None]).astype(F32)
    d_k = jnp.exp((C - 1.0 - idx)[:, None] * log_gamma[None]).astype(F32)
    d_c = jnp.exp(C * log_gamma).astype(F32)
    bcast = lambda a: jnp.broadcast_to(a.T[:, :, None], (H, C, d))
    return d_in, bcast(d_q), bcast(d_k), jnp.broadcast_to(d_c[:, None, None], (H, d, d))


def _gate_expand():
    m = np.zeros((LANES, N_BRANCH * NSA_W), np.float32)
    for h in range(NSA_HEADS):
        for br in range(N_BRANCH):
            m[h * N_BRANCH + br, br * NSA_W + h * HEAD_DIM: br * NSA_W + (h + 1) * HEAD_DIM] = 1.0
    return jnp.asarray(m, BF16)


def _block_expand(seq):
    m = (np.arange(LANES)[:, None] == (np.arange(seq) // SEL_BLOCK)[None, :]).astype(np.float32)
    return jnp.asarray(m, BF16)


def _compress_weights(pe, w1, w2):
    G, d = NSA_KV_HEADS, HEAD_DIM
    hid = w1.shape[-1]
    eye = jnp.eye(G, dtype=F32)
    w1h = w1.reshape(2, CMP_STRIDE, d, hid)
    w1b = jnp.einsum('pldh,ge->plgdeh', w1h, eye).reshape(2, CMP_ROW, G * hid)
    w2b = jnp.einsum('hd,ge->ghed', w2, eye).reshape(G * hid, G * d)
    peb = jnp.broadcast_to(pe.reshape(2, CMP_STRIDE, 1, d), (2, CMP_STRIDE, G, d)).reshape(2, CMP_ROW)
    return peb, w1b.astype(BF16), w2b.astype(BF16)


def kernel(x, c, positions, ada_w, ada_b, norm_mix_g, norm_ffn_g, w_in, w_out, ret_norm_g, gmlp_ln_g, gmlp_ws,
           gmlp_b, nsa_q_norm_g, nsa_k_norm_g, cmp_pe_k, cmp_pe_v, cmp_w1_k, cmp_w2_k, cmp_w1_v, cmp_w2_v,
           router_w, router_b, moe_w_gate, moe_w_up, moe_w_down):
    B, S, D = x.shape
    L = ada_w.shape[0]
    E = router_w.shape[1]
    DE = moe_w_gate.shape[-1]
    T = B * S
    n_chunk = S // CHUNK
    n_cmp = S // CMP_STRIDE
    n_sel = S // SEL_BLOCK
    top_n = min(SEL_TOPK, n_sel)
    n_qt = S // Q_TILE
    tm = min(512, S)
    tm_moe = min(1024, S)
    assert S % KEY_CHUNK == 0 and S >= WINDOW + Q_TILE and D == RET_W + GMLP_W + NSA_W
    assert w_in.shape[-1] == N_IN and T % tm_moe == 0 and n_sel <= LANES and NSA_KV_HEADS == 2

    half = HEAD_DIM // 2
    inv_freq = jnp.power(ROPE_THETA, -jnp.arange(half, dtype=F32) / half)
    invf = jnp.tile(inv_freq, LANES // half)[None, :]
    cos_t, sin_t = pl.pallas_call(
        _rope_table_kernel, name="rope_tables", grid=(B,),
        in_specs=[pl.BlockSpec((1, S, 1), lambda b: (b, 0, 0)), pl.BlockSpec((1, LANES), lambda b: (0, 0))],
        out_specs=[pl.BlockSpec((1, S, LANES), lambda b: (b, 0, 0))] * 2,
        out_shape=[jax.ShapeDtypeStruct((B, S, LANES), F32)] * 2,
        compiler_params=_cparams("parallel"),
    )(positions[:, :, None], invf)
    pad_c = lambda a: jnp.pad(a[:, CMP_LEN - 1::CMP_STRIDE], ((0, 0), (0, 1), (0, 0)))
    cos_c, sin_c = pad_c(cos_t), pad_c(sin_t)
    cos_f, sin_f = cos_t.reshape(T, LANES), sin_t.reshape(T, LANES)

    mod = pl.pallas_call(
        _ada_kernel, name="ada_mod", grid=(L, 6),
        in_specs=[pl.BlockSpec((B, D), lambda l, j: (0, 0)),
                  pl.BlockSpec((1, D, D), lambda l, j: (l, 0, j)),
                  pl.BlockSpec((1, 1, 1, D), lambda l, j: (l, j, 0, 0))],
        out_specs=pl.BlockSpec((1, 1, B, D), lambda l, j: (l, j, 0, 0)),
        out_shape=jax.ShapeDtypeStruct((L, 6, B, D), F32),
        compiler_params=_cparams("parallel", "parallel"),
    )(c, ada_w, ada_b.reshape(L, 6, 1, D))
    mod = mod.transpose(0, 2, 1, 3)

    d_in, d_q, d_k, d_c = _decay_tables()
    gexp = _gate_expand()
    expand = _block_expand(S)
    rw_hi = router_w.T.astype(BF16)
    rw_lo = (router_w.T - rw_hi.astype(F32)).astype(BF16)
    rwt = jnp.stack([rw_hi, rw_lo])
    rb = router_b.reshape(E, 1)
    tri = jnp.asarray(np.triu(np.ones((tm, tm), np.float32), k=1), BF16)
    x2 = x.reshape(T, D)
    tiles_per_batch = S // tm
    row_spec = lambda w: pl.BlockSpec((tm, w), lambda i: (i, 0))
    full = lambda shape: pl.BlockSpec(shape, lambda *_: (0,) * len(shape))

    for l in range(L):
        mod_l = mod[l]
        w_in_p = jnp.pad(w_in[l], ((0, 0), (0, N_IN_PAD - N_IN))).astype(BF16)
        raw_widths = tuple(hi - lo for lo, hi in _IN_RAW)
        mod_spec = pl.BlockSpec((1, 6, D), lambda i: (i // tiles_per_batch, 0, 0))
        kt_out = pl.BlockSpec((1, KV_W, tm), lambda i: (i // tiles_per_batch, 0, i % tiles_per_batch))
        kgain = jnp.tile(nsa_k_norm_g[l], (1, NSA_KV_HEADS))
        fuse = l > 0
        x_in = (x_mid, y_tok, mod[l - 1]) if fuse else (x2,)
        outs = pl.pallas_call(
            functools.partial(_in_proj_kernel, fuse_residual=fuse), name="in_proj", grid=(T // tm,),
            in_specs=([row_spec(D), row_spec(D), mod_spec] if fuse else [row_spec(D)])
            + [mod_spec, full((1, D)), full((D, N_IN_PAD)), row_spec(LANES), row_spec(LANES),
               full((1, NSA_W)), full((N_BRANCH, KV_W))],
            out_specs=([row_spec(D)] if fuse else []) + [row_spec(w) for w in raw_widths]
            + [row_spec(NSA_W), kt_out, row_spec(KV_W), kt_out, row_spec(KV_W)],
            out_shape=([jax.ShapeDtypeStruct((T, D), F32)] if fuse else [])
            + [jax.ShapeDtypeStruct((T, w), F32) for w in raw_widths]
            + [jax.ShapeDtypeStruct((T, NSA_W), BF16), jax.ShapeDtypeStruct((B, KV_W, S), BF16),
               jax.ShapeDtypeStruct((T, KV_W), BF16), jax.ShapeDtypeStruct((B, KV_W, S), BF16),
               jax.ShapeDtypeStruct((T, KV_W), BF16)],
            compiler_params=_cparams("parallel"),
        )(*x_in, mod_l, norm_mix_g[l][None, :], w_in_p, cos_f, sin_f,
          jnp.tile(nsa_q_norm_g[l], NSA_HEADS)[None, :], kgain)
        if fuse:
            x2, outs = outs[0], outs[1:]
        zr, zg, zkc, zvc, zgl, q_r, ks_t, vs_b, kw_t, vw_b = outs

        crow = lambda w: pl.BlockSpec((MIX_BATCH, CHUNK, w), lambda b, cc: (b, cc, 0))
        y_ret, y_gm = pl.pallas_call(
            _mixer_kernel, name="ret_gmlp", grid=(B // MIX_BATCH, n_chunk),
            in_specs=[crow(4 * RET_W), crow(2 * GMLP_W), crow(LANES), crow(LANES),
                      full((RET_HEADS, CHUNK, CHUNK)), full((RET_HEADS, CHUNK, HEAD_DIM)),
                      full((RET_HEADS, CHUNK, HEAD_DIM)), full((RET_HEADS, HEAD_DIM, HEAD_DIM)),
                      full((RET_HEADS, 1, HEAD_DIM)), full((1, GMLP_W)),
                      full((GMLP_GROUPS, CHUNK, CHUNK)), full((GMLP_GROUPS, CHUNK, HEAD_DIM))],
            out_specs=[crow(RET_W), crow(GMLP_W)],
            out_shape=[jax.ShapeDtypeStruct((B, S, RET_W), BF16), jax.ShapeDtypeStruct((B, S, GMLP_W), BF16)],
            scratch_shapes=[pltpu.VMEM((MIX_BATCH, RET_HEADS, HEAD_DIM, HEAD_DIM), F32)],
            compiler_params=_cparams("parallel", "arbitrary"),
        )(zr.reshape(B, S, 4 * RET_W), zg.reshape(B, S, 2 * GMLP_W), cos_t, sin_t, d_in, d_q, d_k, d_c,
          ret_norm_g[l][:, None, :], gmlp_ln_g[l][None, :], gmlp_ws[l],
          jnp.broadcast_to(gmlp_b[l][:, :, None], (GMLP_GROUPS, CHUNK, HEAD_DIM)))
        y_ret, y_gm = y_ret.reshape(T, RET_W), y_gm.reshape(T, GMLP_W)

        pek, w1k, w2k = _compress_weights(cmp_pe_k[l], cmp_w1_k[l], cmp_w2_k[l])
        pev, w1v, w2v = _compress_weights(cmp_pe_v[l], cmp_w1_v[l], cmp_w2_v[l])
        cblk = pl.BlockSpec((n_cmp, CMP_ROW), lambda b: (b, 0))
        ghid = w1k.shape[-1]
        ctab = pl.BlockSpec((1, n_cmp, LANES), lambda b: (b, 0, 0))
        kc, vc = pl.pallas_call(
            _compress_kernel, name="nsa_compress", grid=(B,),
            in_specs=[cblk, cblk, full((2, CMP_ROW)), full((2, CMP_ROW)),
                      full((2, CMP_ROW, ghid)), full((ghid, KV_W)), full((2, CMP_ROW, ghid)), full((ghid, KV_W)),
                      full((N_BRANCH, KV_W)), ctab, ctab],
            out_specs=[ctab, ctab],
            out_shape=[jax.ShapeDtypeStruct((B, n_cmp, KV_W), F32)] * 2,
            compiler_params=_cparams("parallel"),
        )(zkc.reshape(T // CMP_STRIDE, CMP_ROW), zvc.reshape(T // CMP_STRIDE, CMP_ROW), pek, pev, w1k, w2k, w1v, w2v,
          kgain, cos_c, sin_c)

        qrow = lambda w: pl.BlockSpec((Q_TILE, w), lambda b, i: (b * n_qt + i, 0))
        kt_in = pl.BlockSpec((1, KV_W, S), lambda b, i: (b, 0, 0))
        v_in = pl.BlockSpec((S, KV_W), lambda b, i: (b, 0))
        ctab2 = pl.BlockSpec((1, n_cmp, KV_W), lambda b, i: (b, 0, 0))
        y_nsa = pl.pallas_call(
            functools.partial(_nsa_kernel, n_sel=n_sel, top_n=top_n), name="nsa_attn", grid=(B, n_qt),
            in_specs=[qrow(NSA_W), qrow(LANES), ctab2, ctab2, kt_in, v_in, kt_in, v_in,
                      full((LANES, N_BRANCH * NSA_W)), full((LANES, S))],
            out_specs=qrow(NSA_W),
            out_shape=jax.ShapeDtypeStruct((T, NSA_W), BF16),
            scratch_shapes=[
                pltpu.VMEM((NSA_KV_HEADS, NSA_GROUP * Q_TILE, WINDOW + Q_TILE), F32),
                pltpu.VMEM((NSA_KV_HEADS, NSA_GROUP * Q_TILE, WINDOW + Q_TILE), BF16),
                pltpu.VMEM((NSA_KV_HEADS, NSA_GROUP * Q_TILE, LANES), F32),
                pltpu.VMEM((NSA_KV_HEADS, NSA_GROUP * Q_TILE, KV_W), F32)],
            compiler_params=_cparams("parallel", "parallel"),
        )(q_r, zgl, kc, vc, ks_t, vs_b, kw_t, vw_b, gexp, expand)

        n_tok_tiles = T // tm
        DX = D + LANES
        x_new, hx, bgrp, cnt = pl.pallas_call(
            functools.partial(_out_proj_kernel, n_experts=E), name="out_proj_router", grid=(n_tok_tiles,),
            in_specs=[row_spec(RET_W), row_spec(GMLP_W), row_spec(NSA_W), row_spec(D),
                      pl.BlockSpec((1, 6, D), lambda i: (i // tiles_per_batch, 0, 0)),
                      full((D, D)), full((1, D)), full((2, E, D)), full((E, 1))],
            out_specs=[row_spec(D), row_spec(DX), pl.BlockSpec((1, tm), lambda i: (0, i)),
                       pl.BlockSpec((1, N_EXPERT_GROUPS, LANES), lambda i: (i, 0, 0))],
            out_shape=[jax.ShapeDtypeStruct((T, D), F32), jax.ShapeDtypeStruct((T, DX), F32),
                       jax.ShapeDtypeStruct((1, T), jnp.int32),
                       jax.ShapeDtypeStruct((n_tok_tiles, N_EXPERT_GROUPS, LANES), F32)],
            compiler_params=_cparams("parallel"),
        )(y_ret, y_gm, y_nsa, x2, mod_l, w_out[l].astype(BF16), norm_ffn_g[l][None, :], rwt, rb)

        tile_cnt = cnt[:, :, 0].astype(jnp.int32)
        seg = ((jnp.sum(tile_cnt, axis=0) + tm_moe - 1) // tm_moe) * tm_moe
        earlier = lambda n: jnp.arange(n)[:, None] > jnp.arange(n)[None, :]
        seg_start = jnp.sum(jnp.where(earlier(N_EXPERT_GROUPS), seg[None, :], 0), axis=1)
        tile_off = seg_start[None, :] + jnp.sum(
            jnp.where(earlier(n_tok_tiles)[:, :, None], tile_cnt[None, :, :], 0), axis=1)
        n_row_tiles = T // tm_moe + N_EXPERT_GROUPS
        row0 = jnp.arange(n_row_tiles) * tm_moe
        tile_grp = jnp.minimum(jnp.sum(((seg_start + seg)[None, :] <= row0[:, None]).astype(jnp.int32), axis=1),
                               N_EXPERT_GROUPS - 1)
        dest = pl.pallas_call(
            _route_kernel, name="moe_route", grid=(n_tok_tiles,),
            in_specs=[pl.BlockSpec((1, tm), lambda i: (0, i)),
                      pl.BlockSpec((1, N_EXPERT_GROUPS, LANES), lambda i: (i, 0, 0)), full((tm, tm))],
            out_specs=pl.BlockSpec((1, tm), lambda i: (0, i)),
            out_shape=jax.ShapeDtypeStruct((1, T), jnp.int32),
            compiler_params=_cparams("parallel"),
        )(bgrp, jnp.broadcast_to(tile_off.astype(F32)[:, :, None], (n_tok_tiles, N_EXPERT_GROUPS, LANES)), tri)
        dest3 = dest.reshape(n_tok_tiles, 1, tm)
        dest_spec = pl.BlockSpec((1, 1, tm), lambda i: (i, 0, 0), memory_space=pltpu.SMEM)
        any_spec = pl.BlockSpec(memory_space=pl.ANY)

        n_rows = n_row_tiles * tm_moe
        xs = pl.pallas_call(
            functools.partial(_row_move_kernel, scatter=True), name="moe_dispatch", grid=(n_tok_tiles,),
            in_specs=[dest_spec, any_spec, any_spec],
            out_specs=any_spec,
            out_shape=jax.ShapeDtypeStruct((n_rows, DX), F32),
            scratch_shapes=[pltpu.SemaphoreType.DMA((2,))],
            input_output_aliases={2: 0},
            compiler_params=_cparams("arbitrary"),
        )(dest3, hx, jnp.zeros((n_rows, DX), F32))

        per = E // N_EXPERT_GROUPS
        ys = pl.pallas_call(
            _moe_kernel, name="moe",
            grid_spec=pltpu.PrefetchScalarGridSpec(
                num_scalar_prefetch=1, grid=(n_row_tiles, per),
                in_specs=[pl.BlockSpec((tm_moe, DX), lambda k, u, grp: (k, 0)),
                          pl.BlockSpec((1, D, DE), lambda k, u, grp: (grp[k] * per + u, 0, 0)),
                          pl.BlockSpec((1, D, DE), lambda k, u, grp: (grp[k] * per + u, 0, 0)),
                          pl.BlockSpec((1, DE, D), lambda k, u, grp: (grp[k] * per + u, 0, 0))],
                out_specs=pl.BlockSpec((tm_moe, D), lambda k, u, grp: (k, 0)),
                scratch_shapes=[pltpu.VMEM((tm_moe, D), F32)]),
            out_shape=jax.ShapeDtypeStruct((n_rows, D), F32),
            compiler_params=_cparams("parallel", "arbitrary"),
        )(tile_grp, xs, moe_w_gate[l], moe_w_up[l], moe_w_down[l])

        y_tok = pl.pallas_call(
            functools.partial(_row_move_kernel, scatter=False), name="moe_combine", grid=(n_tok_tiles,),
            in_specs=[dest_spec, any_spec],
            out_specs=any_spec,
            out_shape=jax.ShapeDtypeStruct((T, D), F32),
            scratch_shapes=[pltpu.SemaphoreType.DMA((2,))],
            compiler_params=_cparams("arbitrary"),
        )(dest3, ys)
        x_mid = x_new

    x_out = pl.pallas_call(
        _residual_kernel, name="moe_residual", grid=(T // tm,),
        in_specs=[row_spec(D), row_spec(D), pl.BlockSpec((1, 6, D), lambda i: (i // tiles_per_batch, 0, 0))],
        out_specs=row_spec(D),
        out_shape=jax.ShapeDtypeStruct((T, D), F32),
        compiler_params=_cparams("parallel"),
    )(x_mid, y_tok, mod[L - 1])
    return x_out.reshape(B, S, D)
```

```python
import functools

import numpy as np
import jax
import jax.numpy as jnp
from jax import lax
from jax.experimental import pallas as pl
from jax.experimental.pallas import tpu as pltpu

F32 = jnp.float32
BF16 = jnp.bfloat16

HEAD_DIM = 64
LANES = 128
RET_HEADS = 4
GMLP_GROUPS = 4
NSA_HEADS = 8
NSA_KV_HEADS = 2
NSA_GROUP = NSA_HEADS // NSA_KV_HEADS
N_BRANCH = 3
CHUNK = 128
CMP_LEN = 32
CMP_STRIDE = 16
SEL_BLOCK = 64
SEL_TOPK = 8
WINDOW = 512
ROPE_THETA = 10000.0
N_EXPERT_GROUPS = 4
EXPERT_TOPK = 2
NORM_EPS = 1e-6
NEG_INF = -1e30
FORCE_SCORE = 1e4

RET_W = RET_HEADS * HEAD_DIM
GMLP_W = GMLP_GROUPS * HEAD_DIM
NSA_W = NSA_HEADS * HEAD_DIM
KV_W = NSA_KV_HEADS * HEAD_DIM
GATE_W = NSA_HEADS * N_BRANCH
CMP_ROW = CMP_STRIDE * KV_W
OFF_RET = 0
OFF_GMLP = 4 * RET_W
OFF_Q = OFF_GMLP + 2 * GMLP_W
OFF_KV = OFF_Q + NSA_W
OFF_GL = OFF_KV + 6 * KV_W
N_IN = OFF_GL + GATE_W
N_IN_PAD = OFF_GL + LANES

Q_TILE = 128
KEY_CHUNK = 512
MIX_BATCH = 2
ROW_BLOCK = 32
ROW_UNROLL = True
VMEM_LIMIT = 48 * 1024 * 1024


def _cparams(*sem):
    return pltpu.CompilerParams(dimension_semantics=sem, vmem_limit_bytes=VMEM_LIMIT)


def _dot(a, b):
    return jnp.dot(a.astype(BF16), b.astype(BF16), preferred_element_type=F32)


def _dot_nt(a, b):
    return lax.dot_general(a.astype(BF16), b.astype(BF16), (((1,), (1,)), ((), ())),
                           preferred_element_type=F32)


def _split(a):
    hi = a.astype(BF16)
    lo = (a - hi.astype(F32)).astype(BF16)
    return hi, lo


def _dot_split_lhs(a, b):
    hi, lo = _split(a)
    return (jnp.dot(hi, b, preferred_element_type=F32) + jnp.dot(lo, b, preferred_element_type=F32))


def _dot_split_rhs(a, b):
    hi, lo = _split(b)
    return (jnp.dot(a, hi, preferred_element_type=F32) + jnp.dot(a, lo, preferred_element_type=F32))


def _head_ones(width=LANES):
    r = lax.broadcasted_iota(jnp.int32, (width, width), 0) // HEAD_DIM
    c = lax.broadcasted_iota(jnp.int32, (width, width), 1) // HEAD_DIM
    return jnp.where(r == c, 1.0, 0.0).astype(BF16)


def _rotate_half(y):
    lane = lax.broadcasted_iota(jnp.int32, y.shape, 1)
    first = (lane & (HEAD_DIM - 1)) < (HEAD_DIM // 2)
    return jnp.where(first, -pltpu.roll(y, LANES - HEAD_DIM // 2, 1), pltpu.roll(y, HEAD_DIM // 2, 1))


def _rope_slab(y, cos, sin):
    return y * cos + _rotate_half(y) * sin


def _rope(x, cos, sin):
    return jnp.concatenate(
        [_rope_slab(x[:, k * LANES:(k + 1) * LANES], cos, sin) for k in range(x.shape[1] // LANES)], axis=1)


def _head_rms_rope(x, gain, cos, sin, ones):
    outs = []
    for k in range(x.shape[1] // LANES):
        xs = x[:, k * LANES:(k + 1) * LANES]
        ssq = _dot_split_lhs(xs * xs, ones)
        y = xs * lax.rsqrt(ssq * (1.0 / HEAD_DIM) + NORM_EPS) * gain[:, k * LANES:(k + 1) * LANES]
        outs.append(_rope_slab(y, cos, sin))
    return outs[0] if len(outs) == 1 else jnp.concatenate(outs, axis=1)


def _rms_mod(x, gain, scale, shift):
    y = x * lax.rsqrt(jnp.mean(x * x, axis=-1, keepdims=True) + NORM_EPS)
    return (y * gain) * (1.0 + scale) + shift


def _softmax_rows(s):
    e = jnp.exp(s - jnp.max(s, axis=-1, keepdims=True))
    return e / jnp.sum(e, axis=-1, keepdims=True)


def _mask_bias(keep):
    return jnp.where(keep, 0.0, NEG_INF)


def _rope_table_kernel(pos_ref, invf_ref, cos_ref, sin_ref):
    ang = pos_ref[0].astype(F32) * invf_ref[...]
    cos_ref[0] = jnp.cos(ang)
    sin_ref[0] = jnp.sin(ang)


def _ada_kernel(c_ref, w_ref, b_ref, o_ref):
    c = c_ref[...]
    a = c * jax.nn.sigmoid(c)
    o_ref[0, 0] = jnp.dot(a, w_ref[0], preferred_element_type=F32,
                          precision=lax.Precision.HIGHEST) + b_ref[0, 0]


_IN_RAW = ((OFF_RET, OFF_GMLP), (OFF_GMLP, OFF_Q), (OFF_KV, OFF_KV + KV_W), (OFF_KV + KV_W, OFF_KV + 2 * KV_W),
           (OFF_GL, N_IN_PAD))


def _in_proj_kernel(x_ref, mod_ref, g_ref, w_ref, cos_ref, sin_ref, qg_ref, kg_ref, *out_refs):
    *raw_refs, q_out, kst_out, vs_out, kwt_out, vw_out = out_refs
    mod = mod_ref[0]
    h = _rms_mod(x_ref[...], g_ref[...], mod[1:2], mod[0:1]).astype(BF16)
    proj = lambda lo, hi: jnp.dot(h, w_ref[:, lo:hi], preferred_element_type=F32)
    for ref, (lo, hi) in zip(raw_refs, _IN_RAW):
        ref[...] = proj(lo, hi)

    ones = _head_ones()
    cos, sin = cos_ref[...], sin_ref[...]
    q_out[...] = (_head_rms_rope(proj(OFF_Q, OFF_KV), qg_ref[...], cos, sin, ones) * (HEAD_DIM ** -0.5)).astype(BF16)
    sel_kv = OFF_KV + 2 * KV_W
    kst_out[0] = _head_rms_rope(proj(sel_kv, sel_kv + KV_W), kg_ref[1:2], cos, sin, ones).T.astype(BF16)
    vs_out[...] = proj(sel_kv + KV_W, sel_kv + 2 * KV_W).astype(BF16)
    kwt_out[0] = _head_rms_rope(proj(sel_kv + 2 * KV_W, sel_kv + 3 * KV_W), kg_ref[2:3], cos, sin, ones).T.astype(BF16)
    vw_out[...] = proj(sel_kv + 3 * KV_W, OFF_GL).astype(BF16)


def _mixer_kernel(zr_ref, zg_ref, cos_ref, sin_ref, din_ref, dq_ref, dk_ref, dc_ref, rg_ref,
                  lng_ref, ws_ref, bs_ref, yret_ref, ygm_ref, state_ref):
    @pl.when(pl.program_id(1) == 0)
    def _():
        state_ref[...] = jnp.zeros_like(state_ref)

    row = lax.broadcasted_iota(jnp.int32, (CHUNK, CHUNK), 0)
    col = lax.broadcasted_iota(jnp.int32, (CHUNK, CHUNK), 1)
    ws = [jnp.where(row >= col, ws_ref[gi], 0.0).astype(BF16) for gi in range(GMLP_GROUPS)]

    for bb in range(zr_ref.shape[0]):
        cos, sin = cos_ref[bb], sin_ref[bb]
        q = _rope(zr_ref[bb, :, 0:RET_W], cos, sin)
        k = _rope(zr_ref[bb, :, RET_W:2 * RET_W], cos, sin) * (HEAD_DIM ** -0.5)
        v = zr_ref[bb, :, 2 * RET_W:3 * RET_W]
        g = zr_ref[bb, :, 3 * RET_W:4 * RET_W]
        outs = []
        for h in range(RET_HEADS):
            sl = slice(h * HEAD_DIM, (h + 1) * HEAD_DIM)
            qh, kh, vh = q[:, sl], k[:, sl], v[:, sl]
            st = state_ref[bb, h]
            inner = _dot_nt(qh, kh) * din_ref[h]
            o = _dot(inner, vh) + _dot(qh, st) * dq_ref[h]
            state_ref[bb, h] = st * dc_ref[h] + _dot((kh * dk_ref[h]).T, vh)
            o = o * lax.rsqrt(jnp.mean(o * o, axis=-1, keepdims=True) + NORM_EPS) * rg_ref[h]
            outs.append(o)
        yret_ref[bb] = (jnp.concatenate(outs, axis=1) * (g * jax.nn.sigmoid(g))).astype(BF16)

        u = jax.nn.gelu(zg_ref[bb, :, 0:GMLP_W])
        vv = jax.nn.gelu(zg_ref[bb, :, GMLP_W:2 * GMLP_W])
        mu = jnp.mean(vv, axis=-1, keepdims=True)
        var = jnp.mean(jnp.square(vv - mu), axis=-1, keepdims=True)
        vn = (vv - mu) * lax.rsqrt(var + NORM_EPS) * lng_ref[...]
        ss = [_dot(ws[gi], vn[:, gi * HEAD_DIM:(gi + 1) * HEAD_DIM]) + bs_ref[gi] for gi in range(GMLP_GROUPS)]
        ygm_ref[bb] = (u * jnp.concatenate(ss, axis=1)).astype(BF16)


def _compress_kernel(xk_ref, xv_ref, pek_ref, pev_ref, w1k_ref, w2k_ref, w1v_ref, w2v_ref, kg_ref,
                     cos_ref, sin_ref, kc_ref, vc_ref):
    n_rows = xk_ref.shape[0]

    def comp(x, pe_ref, w1_ref, w2_ref):
        a = _dot(x + pe_ref[0:1], w1_ref[0])
        b = _dot(x + pe_ref[1:2], w1_ref[1])
        h = a + pltpu.roll(b, n_rows - 1, 0)
        return _dot(jax.nn.gelu(h), w2_ref[...])

    kc_ref[0] = _head_rms_rope(comp(xk_ref[...], pek_ref, w1k_ref, w2k_ref), kg_ref[0:1],
                               cos_ref[0], sin_ref[0], _head_ones())
    vc_ref[0] = comp(xv_ref[...], pev_ref, w1v_ref, w2v_ref)


def _nsa_kernel(q_ref, gl_ref, kc_ref, vc_ref, kst_ref, vs_ref, kwt_ref, vw_ref, gexp_ref, expand_ref, o_ref,
                s_scr, p_scr, m_scr, acc_scr, *, n_sel, top_n):
    i = pl.program_id(1)
    qt = q_ref.shape[0]
    n_cmp = kc_ref.shape[1]
    t0 = i * qt
    n_chunks = (t0 + qt + KEY_CHUNK - 1) // KEY_CHUNK
    wk = WINDOW + qt
    wstart = pl.multiple_of(jnp.maximum(t0 - WINDOW, 0), LANES)

    t_r = t0 + lax.broadcasted_iota(jnp.int32, (qt, n_cmp), 0)
    n_c = lax.broadcasted_iota(jnp.int32, (qt, n_cmp), 1)
    keep_c = jnp.where(n_c * CMP_STRIDE + (CMP_LEN - 1) <= t_r, 1.0, 0.0)
    bias_c = (keep_c - 1.0) * (-NEG_INF)
    t_l = t0 + lax.broadcasted_iota(jnp.int32, (n_cmp, qt), 1)
    n_s = lax.broadcasted_iota(jnp.int32, (n_cmp, qt), 0)
    keep_ct = jnp.where(n_s * CMP_STRIDE + (CMP_LEN - 1) <= t_l, 1.0, 0.0)
    keep_ct4 = jnp.concatenate([keep_ct] * NSA_GROUP, axis=1)
    bias_ct4 = (keep_ct4 - 1.0) * (-NEG_INF)
    jo = lax.broadcasted_iota(jnp.int32, (n_sel, n_cmp), 0) * SEL_BLOCK
    no = lax.broadcasted_iota(jnp.int32, (n_sel, n_cmp), 1) * CMP_STRIDE
    overlap = jnp.where((no < jo + SEL_BLOCK) & (no + CMP_LEN > jo), 1.0, 0.0).astype(BF16)
    j = lax.broadcasted_iota(jnp.int32, (n_sel, qt), 0)
    tq = t0 + lax.broadcasted_iota(jnp.int32, (n_sel, qt), 1)
    cur = tq // SEL_BLOCK
    forced = (j == 0) | (j == cur) | (j == cur - 1)
    allowed = j * SEL_BLOCK <= tq
    t_w = t0 + lax.broadcasted_iota(jnp.int32, (qt, wk), 0)
    s_w = wstart + lax.broadcasted_iota(jnp.int32, (qt, wk), 1)
    bias_w = _mask_bias((s_w <= t_w) & (t_w - s_w < WINDOW))
    t_k = t0 + lax.broadcasted_iota(jnp.int32, (qt, KEY_CHUNK), 0)
    s_k = lax.broadcasted_iota(jnp.int32, (qt, KEY_CHUNK), 1)

    rep = lambda a: jnp.concatenate([a] * NSA_GROUP, axis=0)
    keep_c4, bias_c4 = rep(keep_c), rep(bias_c)
    unstack = lambda o, gs: [o[r * qt:(r + 1) * qt, gs] for r in range(NSA_GROUP)]
    rows = NSA_GROUP * qt
    lane_v = lax.broadcasted_iota(jnp.int32, (1, KV_W), 1)

    def attend(g, qs, kt, vv, bias, first):
        width = bias.shape[1]
        own = (lane_v >= g * HEAD_DIM) & (lane_v < (g + 1) * HEAD_DIM)
        vv1 = jnp.where(own, vv, jnp.ones_like(vv))
        s = jnp.dot(qs, kt, preferred_element_type=F32) + rep(bias)
        s_scr[g, :, 0:width] = s
        mx = jnp.max(s, axis=-1, keepdims=True)
        if first:
            m_scr[g] = jnp.broadcast_to(mx, (rows, LANES))
        else:
            m_old = m_scr[g]
            m_new = jnp.maximum(m_old, mx)
            alpha = jnp.exp(m_old - m_new)
            m_scr[g] = m_new

        def exp_pass(b, carry):
            r0 = pl.multiple_of(b * ROW_BLOCK, ROW_BLOCK)
            m_new = m_scr[g, pl.ds(r0, ROW_BLOCK), :]
            p = jnp.exp(s_scr[g, pl.ds(r0, ROW_BLOCK), 0:width]
                        - jnp.concatenate([m_new] * (width // LANES), axis=1))
            p_scr[g, pl.ds(r0, ROW_BLOCK), 0:width] = p.astype(BF16)
            return carry

        lax.fori_loop(0, rows // ROW_BLOCK, exp_pass, 0, unroll=ROW_UNROLL)
        pv = jnp.dot(p_scr[g, :, 0:width], vv1, preferred_element_type=F32)
        acc_scr[g] = pv if first else alpha * acc_scr[g] + pv

    def finish(g):
        acc = acc_scr[g]
        num, den = acc[:, g * HEAD_DIM:(g + 1) * HEAD_DIM], acc[:, (1 - g) * HEAD_DIM:(2 - g) * HEAD_DIM]
        return unstack(num / den, slice(None))

    o_cmp, o_win, qs_g, sel_g = [], [], [], []
    for g in range(NSA_KV_HEADS):
        gs = slice(g * HEAD_DIM, (g + 1) * HEAD_DIM)
        qs = jnp.concatenate([q_ref[:, (g * NSA_GROUP + r) * HEAD_DIM:(g * NSA_GROUP + r + 1) * HEAD_DIM]
                              for r in range(NSA_GROUP)], axis=0)
        qs_g.append(qs)
        kc = kc_ref[0][:, gs].astype(BF16)
        vc = vc_ref[0].astype(BF16)

        p = _softmax_rows(_dot_nt(qs, kc) + bias_c4) * keep_c4
        o_cmp += unstack(_dot(p, vc), gs)

        st = _dot_nt(kc, qs) + bias_ct4
        e = jnp.exp(st - jnp.max(st, axis=0, keepdims=True))
        pt = e / jnp.sum(e, axis=0, keepdims=True) * keep_ct4
        psum = pt[:, 0:qt]
        for r in range(1, NSA_GROUP):
            psum = psum + pt[:, r * qt:(r + 1) * qt]
        imp = _dot_split_rhs(overlap, psum)
        score = jnp.where(forced, FORCE_SCORE, jnp.where(allowed, imp, -1.0))
        rank = jnp.zeros((n_sel, qt), F32)
        for jp in range(n_sel):
            other = score[jp:jp + 1, :]
            tie = jnp.where(j > jp, 1.0, 0.0)
            rank = rank + jnp.where(other > score, 1.0, jnp.where(other == score, tie, 0.0))
        sel_t = jnp.where(rank < top_n, 1.0, 0.0)
        if n_sel < LANES:
            sel_t = jnp.concatenate([sel_t, jnp.zeros((LANES - n_sel, qt), F32)], axis=0)
        sel_g.append(sel_t.T.astype(BF16))

        attend(g, qs, kwt_ref[0, gs, pl.ds(wstart, wk)], vw_ref[pl.ds(wstart, wk), :], bias_w, True)
        o_win += finish(g)

    def chunk(c, first):
        k0 = pl.multiple_of(c * KEY_CHUNK, KEY_CHUNK)
        vv = vs_ref[pl.ds(k0, KEY_CHUNK), :]
        ex = expand_ref[:, pl.ds(k0, KEY_CHUNK)]
        causal = k0 + s_k <= t_k
        for g in range(NSA_KV_HEADS):
            chosen = jnp.dot(sel_g[g], ex, preferred_element_type=F32)
            attend(g, qs_g[g], kst_ref[0, g * HEAD_DIM:(g + 1) * HEAD_DIM, pl.ds(k0, KEY_CHUNK)], vv,
                   _mask_bias((chosen > 0.5) & causal), first)

    chunk(0, True)

    def body(c, carry):
        chunk(c, False)
        return carry

    lax.fori_loop(1, n_chunks, body, 0)
    o_sel = finish(0) + finish(1)

    gx = _dot_split_lhs(jax.nn.sigmoid(gl_ref[...]), gexp_ref[...])
    cat = lambda parts: jnp.concatenate(parts, axis=1)
    o_ref[...] = (gx[:, 0:NSA_W] * cat(o_cmp) + gx[:, NSA_W:2 * NSA_W] * cat(o_sel)
                  + gx[:, 2 * NSA_W:3 * NSA_W] * cat(o_win)).astype(BF16)


def _out_proj_kernel(yret_ref, ygm_ref, ynsa_ref, x_ref, mod_ref, wout_ref,
                     g2_ref, rwt_ref, rb_ref, xn_ref, hx_ref, bg_ref, cnt_ref, *, n_experts):
    mod = mod_ref[0]
    d_model = x_ref.shape[1]
    y = (jnp.dot(yret_ref[...], wout_ref[0:RET_W], preferred_element_type=F32)
         + jnp.dot(ygm_ref[...], wout_ref[RET_W:RET_W + GMLP_W], preferred_element_type=F32)
         + jnp.dot(ynsa_ref[...], wout_ref[RET_W + GMLP_W:], preferred_element_type=F32))
    xn = x_ref[...] + mod[2:3] * y
    xn_ref[...] = xn
    h2 = _rms_mod(xn, g2_ref[...], mod[4:5], mod[3:4])
    hx_ref[:, 0:d_model] = h2

    hh, hl = _split(h2)
    wh, wl = rwt_ref[0], rwt_ref[1]
    nt = lambda a, b: lax.dot_general(a, b, (((1,), (1,)), ((), ())), preferred_element_type=F32)
    logits = nt(wh, hh) + nt(wh, hl) + nt(wl, hh)
    ex = jnp.exp(logits - jnp.max(logits, axis=0, keepdims=True))
    probs = ex / jnp.sum(ex, axis=0, keepdims=True)
    sel = probs + rb_ref[...]
    per = n_experts // N_EXPERT_GROUPS
    srow = [sel[e:e + 1, :] for e in range(n_experts)]
    prow = [probs[e:e + 1, :] for e in range(n_experts)]
    gscore = []
    for gi in range(N_EXPERT_GROUPS):
        a = srow[gi * per:(gi + 1) * per]
        best = None
        for u in range(per):
            for w in range(u + 1, per):
                pair = a[u] + a[w]
                best = pair if best is None else jnp.maximum(best, pair)
        gscore.append(best)
    bg = jnp.zeros_like(gscore[0], dtype=jnp.int32)
    bs = gscore[0]
    for gi in range(1, N_EXPERT_GROUPS):
        upd = gscore[gi] > bs
        bg = jnp.where(upd, gi, bg)
        bs = jnp.where(upd, gscore[gi], bs)
    cs, cp = [], []
    for u in range(per):
        su, pu = srow[u], prow[u]
        for gi in range(1, N_EXPERT_GROUPS):
            su = jnp.where(bg == gi, srow[gi * per + u], su)
            pu = jnp.where(bg == gi, prow[gi * per + u], pu)
        cs.append(su)
        cp.append(pu)
    chosen = []
    for u in range(per):
        rank = jnp.zeros_like(cs[u])
        for w in range(per):
            if w == u:
                continue
            before = (cs[w] >= cs[u]) if w < u else (cs[w] > cs[u])
            rank = rank + jnp.where(before, 1.0, 0.0)
        chosen.append(jnp.where(rank < EXPERT_TOPK, 1.0, 0.0))
    denom = jnp.zeros_like(cp[0])
    for u in range(per):
        denom = denom + chosen[u] * cp[u]
    tm = bg.shape[1]
    grows = [chosen[u] * cp[u] / denom for u in range(per)]
    gpad = jnp.concatenate(grows + [jnp.zeros((LANES - per, tm), F32)], axis=0)
    hx_ref[:, d_model:d_model + LANES] = gpad.T
    bg_ref[...] = bg
    cnt_ref[0] = jnp.concatenate(
        [jnp.broadcast_to(jnp.sum(jnp.where(bg == gi, 1.0, 0.0), axis=1, keepdims=True), (1, LANES))
         for gi in range(N_EXPERT_GROUPS)], axis=0)


def _route_kernel(bg_ref, off_ref, tri_ref, dest_ref):
    bg = bg_ref[...]
    tm = bg.shape[1]
    member = [jnp.where(bg == gi, 1.0, 0.0) for gi in range(N_EXPERT_GROUPS)]
    pad = jnp.zeros((8 - N_EXPERT_GROUPS, tm), F32)
    before = jnp.dot(jnp.concatenate(member + [pad], axis=0).astype(BF16), tri_ref[...],
                     preferred_element_type=F32)
    off = off_ref[0]
    dest = jnp.zeros((1, tm), F32)
    for gi in range(N_EXPERT_GROUPS):
        start = jnp.concatenate([off[gi:gi + 1, :]] * (tm // LANES), axis=1)
        dest = dest + member[gi] * (start + before[gi:gi + 1, :])
    dest_ref[...] = dest.astype(jnp.int32)


def _dispatch_kernel(dest_ref, hx_ref, init_ref, xs_ref, sem):
    del init_ref
    n = hx_ref.shape[0]

    def row_copy(r, d):
        return pltpu.make_async_copy(hx_ref.at[pl.ds(r, 1)], xs_ref.at[pl.ds(d, 1)], sem)

    def start(r, carry):
        row_copy(r, dest_ref[0, 0, r]).start()
        return carry

    def wait(r, carry):
        row_copy(0, 0).wait()
        return carry

    lax.fori_loop(0, n, start, 0, unroll=8)
    lax.fori_loop(0, n, wait, 0, unroll=8)


def _moe_kernel(grp_ref, xs_ref, wg_ref, wu_ref, wd_ref, y_ref, acc_ref):
    del grp_ref
    u = pl.program_id(1)
    d_model = y_ref.shape[1]

    @pl.when(u == 0)
    def _():
        acc_ref[...] = jnp.zeros_like(acc_ref)

    h = xs_ref[:, 0:d_model].astype(BF16)
    gates = xs_ref[:, d_model:d_model + LANES]
    lane = lax.broadcasted_iota(jnp.int32, gates.shape, 1)
    gcol = jnp.sum(jnp.where(lane == u, gates, 0.0), axis=-1, keepdims=True)
    hg = jnp.dot(h, wg_ref[0].astype(BF16), preferred_element_type=F32)
    hu = jnp.dot(h, wu_ref[0].astype(BF16), preferred_element_type=F32)
    a = (hg * jax.nn.sigmoid(hg)) * hu
    acc_ref[...] += _dot(gcol * a, wd_ref[0])

    @pl.when(u == pl.num_programs(1) - 1)
    def _():
        y_ref[...] = acc_ref[...]


def _combine_kernel(cur_ref, nxt_ref, y_ref, xn_ref, mod_ref, o_ref, buf, sems):
    i = pl.program_id(0)
    n = xn_ref.shape[0]

    def request(idx_ref, slot):
        def start(r, carry):
            pltpu.make_async_copy(y_ref.at[pl.ds(idx_ref[0, 0, r], 1)], buf.at[slot, pl.ds(r, 1)],
                                  sems.at[slot]).start()
            return carry
        lax.fori_loop(0, n, start, 0, unroll=8)

    @pl.when(i == 0)
    def _():
        request(cur_ref, 0)

    @pl.when(i < pl.num_programs(0) - 1)
    def _():
        request(nxt_ref, (i + 1) % 2)

    slot = i % 2

    def wait(r, carry):
        pltpu.make_async_copy(y_ref.at[pl.ds(0, 1)], buf.at[slot, pl.ds(0, 1)], sems.at[slot]).wait()
        return carry

    lax.fori_loop(0, n, wait, 0, unroll=8)
    o_ref[...] = xn_ref[...] + mod_ref[0][5:6] * buf[slot]


def _decay_tables():
    H, C, d = RET_HEADS, CHUNK, HEAD_DIM
    log_gamma = jnp.log1p(-jnp.power(2.0, -5.0 - jnp.arange(H, dtype=F32)))
    idx = jnp.arange(C, dtype=F32)
    diff = idx[:, None] - idx[None, :]
    d_in = jnp.where(diff >= 0, jnp.exp(jnp.maximum(diff, 0.0)[None] * log_gamma[:, None, None]), 0.0).astype(F32)
    d_q = jnp.exp((idx + 1.0)[:, None] * log_gamma[None]).astype(F32)
    d_k = jnp.exp((C - 1.0 - idx)[:, None] * log_gamma[None]).astype(F32)
    d_c = jnp.exp(C * log_gamma).astype(F32)
    bcast = lambda a: jnp.broadcast_to(a.T[:, :, None], (H, C, d))
    return d_in, bcast(d_q), bcast(d_k), jnp.broadcast_to(d_c[:, None, None], (H, d, d))


def _gate_expand():
    m = np.zeros((LANES, N_BRANCH * NSA_W), np.float32)
    for h in range(NSA_HEADS):
        for br in range(N_BRANCH):
            m[h * N_BRANCH + br, br * NSA_W + h * HEAD_DIM: br * NSA_W + (h + 1) * HEAD_DIM] = 1.0
    return jnp.asarray(m, BF16)


def _block_expand(seq):
    m = (np.arange(LANES)[:, None] == (np.arange(seq) // SEL_BLOCK)[None, :]).astype(np.float32)
    return jnp.asarray(m, BF16)


def _compress_weights(pe, w1, w2):
    G, d = NSA_KV_HEADS, HEAD_DIM
    hid = w1.shape[-1]
    eye = jnp.eye(G, dtype=F32)
    w1h = w1.reshape(2, CMP_STRIDE, d, hid)
    w1b = jnp.einsum('pldh,ge->plgdeh', w1h, eye).reshape(2, CMP_ROW, G * hid)
    w2b = jnp.einsum('hd,ge->ghed', w2, eye).reshape(G * hid, G * d)
    peb = jnp.broadcast_to(pe.reshape(2, CMP_STRIDE, 1, d), (2, CMP_STRIDE, G, d)).reshape(2, CMP_ROW)
    return peb, w1b.astype(BF16), w2b.astype(BF16)


def kernel(x, c, positions, ada_w, ada_b, norm_mix_g, norm_ffn_g, w_in, w_out, ret_norm_g, gmlp_ln_g, gmlp_ws,
           gmlp_b, nsa_q_norm_g, nsa_k_norm_g, cmp_pe_k, cmp_pe_v, cmp_w1_k, cmp_w2_k, cmp_w1_v, cmp_w2_v,
           router_w, router_b, moe_w_gate, moe_w_up, moe_w_down):
    B, S, D = x.shape
    L = ada_w.shape[0]
    E = router_w.shape[1]
    DE = moe_w_gate.shape[-1]
    T = B * S
    n_chunk = S // CHUNK
    n_cmp = S // CMP_STRIDE
    n_sel = S // SEL_BLOCK
    top_n = min(SEL_TOPK, n_sel)
    n_qt = S // Q_TILE
    tm = min(512, S)
    tm_moe = min(1024, S)
    assert S % KEY_CHUNK == 0 and S >= WINDOW + Q_TILE and D == RET_W + GMLP_W + NSA_W
    assert w_in.shape[-1] == N_IN and T % tm_moe == 0 and n_sel <= LANES and NSA_KV_HEADS == 2

    half = HEAD_DIM // 2
    inv_freq = jnp.power(ROPE_THETA, -jnp.arange(half, dtype=F32) / half)
    invf = jnp.tile(inv_freq, LANES // half)[None, :]
    cos_t, sin_t = pl.pallas_call(
        _rope_table_kernel, name="rope_tables", grid=(B,),
        in_specs=[pl.BlockSpec((1, S, 1), lambda b: (b, 0, 0)), pl.BlockSpec((1, LANES), lambda b: (0, 0))],
        out_specs=[pl.BlockSpec((1, S, LANES), lambda b: (b, 0, 0))] * 2,
        out_shape=[jax.ShapeDtypeStruct((B, S, LANES), F32)] * 2,
        compiler_params=_cparams("parallel"),
    )(positions[:, :, None], invf)
    pad_c = lambda a: jnp.pad(a[:, CMP_LEN - 1::CMP_STRIDE], ((0, 0), (0, 1), (0, 0)))
    cos_c, sin_c = pad_c(cos_t), pad_c(sin_t)
    cos_f, sin_f = cos_t.reshape(T, LANES), sin_t.reshape(T, LANES)

    mod = pl.pallas_call(
        _ada_kernel, name="ada_mod", grid=(L, 6),
        in_specs=[pl.BlockSpec((B, D), lambda l, j: (0, 0)),
                  pl.BlockSpec((1, D, D), lambda l, j: (l, 0, j)),
                  pl.BlockSpec((1, 1, 1, D), lambda l, j: (l, j, 0, 0))],
        out_specs=pl.BlockSpec((1, 1, B, D), lambda l, j: (l, j, 0, 0)),
        out_shape=jax.ShapeDtypeStruct((L, 6, B, D), F32),
        compiler_params=_cparams("parallel", "parallel"),
    )(c, ada_w, ada_b.reshape(L, 6, 1, D))
    mod = mod.transpose(0, 2, 1, 3)

    d_in, d_q, d_k, d_c = _decay_tables()
    gexp = _gate_expand()
    expand = _block_expand(S)
    rw_hi = router_w.T.astype(BF16)
    rw_lo = (router_w.T - rw_hi.astype(F32)).astype(BF16)
    rwt = jnp.stack([rw_hi, rw_lo])
    rb = router_b.reshape(E, 1)
    tri = jnp.asarray(np.triu(np.ones((tm, tm), np.float32), k=1), BF16)
    x2 = x.reshape(T, D)
    tiles_per_batch = S // tm
    row_spec = lambda w: pl.BlockSpec((tm, w), lambda i: (i, 0))
    full = lambda shape: pl.BlockSpec(shape, lambda *_: (0,) * len(shape))

    for l in range(L):
        mod_l = mod[l]
        w_in_p = jnp.pad(w_in[l], ((0, 0), (0, N_IN_PAD - N_IN))).astype(BF16)
        raw_widths = tuple(hi - lo for lo, hi in _IN_RAW)
        mod_spec = pl.BlockSpec((1, 6, D), lambda i: (i // tiles_per_batch, 0, 0))
        kt_out = pl.BlockSpec((1, KV_W, tm), lambda i: (i // tiles_per_batch, 0, i % tiles_per_batch))
        kgain = jnp.tile(nsa_k_norm_g[l], (1, NSA_KV_HEADS))
        zr, zg, zkc, zvc, zgl, q_r, ks_t, vs_b, kw_t, vw_b = pl.pallas_call(
            _in_proj_kernel, name="in_proj", grid=(T // tm,),
            in_specs=[row_spec(D), mod_spec, full((1, D)), full((D, N_IN_PAD)), row_spec(LANES), row_spec(LANES),
                      full((1, NSA_W)), full((N_BRANCH, KV_W))],
            out_specs=[row_spec(w) for w in raw_widths]
            + [row_spec(NSA_W), kt_out, row_spec(KV_W), kt_out, row_spec(KV_W)],
            out_shape=[jax.ShapeDtypeStruct((T, w), F32) for w in raw_widths]
            + [jax.ShapeDtypeStruct((T, NSA_W), BF16), jax.ShapeDtypeStruct((B, KV_W, S), BF16),
               jax.ShapeDtypeStruct((T, KV_W), BF16), jax.ShapeDtypeStruct((B, KV_W, S), BF16),
               jax.ShapeDtypeStruct((T, KV_W), BF16)],
            compiler_params=_cparams("parallel"),
        )(x2, mod_l, norm_mix_g[l][None, :], w_in_p, cos_f, sin_f,
          jnp.tile(nsa_q_norm_g[l], NSA_HEADS)[None, :], kgain)

        crow = lambda w: pl.BlockSpec((MIX_BATCH, CHUNK, w), lambda b, cc: (b, cc, 0))
        y_ret, y_gm = pl.pallas_call(
            _mixer_kernel, name="ret_gmlp", grid=(B // MIX_BATCH, n_chunk),
            in_specs=[crow(4 * RET_W), crow(2 * GMLP_W), crow(LANES), crow(LANES),
                      full((RET_HEADS, CHUNK, CHUNK)), full((RET_HEADS, CHUNK, HEAD_DIM)),
                      full((RET_HEADS, CHUNK, HEAD_DIM)), full((RET_HEADS, HEAD_DIM, HEAD_DIM)),
                      full((RET_HEADS, 1, HEAD_DIM)), full((1, GMLP_W)),
                      full((GMLP_GROUPS, CHUNK, CHUNK)), full((GMLP_GROUPS, CHUNK, HEAD_DIM))],
            out_specs=[crow(RET_W), crow(GMLP_W)],
            out_shape=[jax.ShapeDtypeStruct((B, S, RET_W), BF16), jax.ShapeDtypeStruct((B, S, GMLP_W), BF16)],
            scratch_shapes=[pltpu.VMEM((MIX_BATCH, RET_HEADS, HEAD_DIM, HEAD_DIM), F32)],
            compiler_params=_cparams("parallel", "arbitrary"),
        )(zr.reshape(B, S, 4 * RET_W), zg.reshape(B, S, 2 * GMLP_W), cos_t, sin_t, d_in, d_q, d_k, d_c,
          ret_norm_g[l][:, None, :], gmlp_ln_g[l][None, :], gmlp_ws[l],
          jnp.broadcast_to(gmlp_b[l][:, :, None], (GMLP_GROUPS, CHUNK, HEAD_DIM)))
        y_ret, y_gm = y_ret.reshape(T, RET_W), y_gm.reshape(T, GMLP_W)

        pek, w1k, w2k = _compress_weights(cmp_pe_k[l], cmp_w1_k[l], cmp_w2_k[l])
        pev, w1v, w2v = _compress_weights(cmp_pe_v[l], cmp_w1_v[l], cmp_w2_v[l])
        cblk = pl.BlockSpec((n_cmp, CMP_ROW), lambda b: (b, 0))
        ghid = w1k.shape[-1]
        ctab = pl.BlockSpec((1, n_cmp, LANES), lambda b: (b, 0, 0))
        kc, vc = pl.pallas_call(
            _compress_kernel, name="nsa_compress", grid=(B,),
            in_specs=[cblk, cblk, full((2, CMP_ROW)), full((2, CMP_ROW)),
                      full((2, CMP_ROW, ghid)), full((ghid, KV_W)), full((2, CMP_ROW, ghid)), full((ghid, KV_W)),
                      full((N_BRANCH, KV_W)), ctab, ctab],
            out_specs=[ctab, ctab],
            out_shape=[jax.ShapeDtypeStruct((B, n_cmp, KV_W), F32)] * 2,
            compiler_params=_cparams("parallel"),
        )(zkc.reshape(T // CMP_STRIDE, CMP_ROW), zvc.reshape(T // CMP_STRIDE, CMP_ROW), pek, pev, w1k, w2k, w1v, w2v,
          kgain, cos_c, sin_c)

        qrow = lambda w: pl.BlockSpec((Q_TILE, w), lambda b, i: (b * n_qt + i, 0))
        kt_in = pl.BlockSpec((1, KV_W, S), lambda b, i: (b, 0, 0))
        v_in = pl.BlockSpec((S, KV_W), lambda b, i: (b, 0))
        ctab2 = pl.BlockSpec((1, n_cmp, KV_W), lambda b, i: (b, 0, 0))
        y_nsa = pl.pallas_call(
            functools.partial(_nsa_kernel, n_sel=n_sel, top_n=top_n), name="nsa_attn", grid=(B, n_qt),
            in_specs=[qrow(NSA_W), qrow(LANES), ctab2, ctab2, kt_in, v_in, kt_in, v_in,
                      full((LANES, N_BRANCH * NSA_W)), full((LANES, S))],
            out_specs=qrow(NSA_W),
            out_shape=jax.ShapeDtypeStruct((T, NSA_W), BF16),
            scratch_shapes=[
                pltpu.VMEM((NSA_KV_HEADS, NSA_GROUP * Q_TILE, WINDOW + Q_TILE), F32),
                pltpu.VMEM((NSA_KV_HEADS, NSA_GROUP * Q_TILE, WINDOW + Q_TILE), BF16),
                pltpu.VMEM((NSA_KV_HEADS, NSA_GROUP * Q_TILE, LANES), F32),
                pltpu.VMEM((NSA_KV_HEADS, NSA_GROUP * Q_TILE, KV_W), F32)],
            compiler_params=_cparams("parallel", "parallel"),
        )(q_r, zgl, kc, vc, ks_t, vs_b, kw_t, vw_b, gexp, expand)

        n_tok_tiles = T // tm
        DX = D + LANES
        x_new, hx, bgrp, cnt = pl.pallas_call(
            functools.partial(_out_proj_kernel, n_experts=E), name="out_proj_router", grid=(n_tok_tiles,),
            in_specs=[row_spec(RET_W), row_spec(GMLP_W), row_spec(NSA_W), row_spec(D),
                      pl.BlockSpec((1, 6, D), lambda i: (i // tiles_per_batch, 0, 0)),
                      full((D, D)), full((1, D)), full((2, E, D)), full((E, 1))],
            out_specs=[row_spec(D), row_spec(DX), pl.BlockSpec((1, tm), lambda i: (0, i)),
                       pl.BlockSpec((1, N_EXPERT_GROUPS, LANES), lambda i: (i, 0, 0))],
            out_shape=[jax.ShapeDtypeStruct((T, D), F32), jax.ShapeDtypeStruct((T, DX), F32),
                       jax.ShapeDtypeStruct((1, T), jnp.int32),
                       jax.ShapeDtypeStruct((n_tok_tiles, N_EXPERT_GROUPS, LANES), F32)],
            compiler_params=_cparams("parallel"),
        )(y_ret, y_gm, y_nsa, x2, mod_l, w_out[l].astype(BF16), norm_ffn_g[l][None, :], rwt, rb)

        tile_cnt = cnt[:, :, 0].astype(jnp.int32)
        seg = ((jnp.sum(tile_cnt, axis=0) + tm_moe - 1) // tm_moe) * tm_moe
        earlier = lambda n: jnp.arange(n)[:, None] > jnp.arange(n)[None, :]
        seg_start = jnp.sum(jnp.where(earlier(N_EXPERT_GROUPS), seg[None, :], 0), axis=1)
        tile_off = seg_start[None, :] + jnp.sum(
            jnp.where(earlier(n_tok_tiles)[:, :, None], tile_cnt[None, :, :], 0), axis=1)
        n_row_tiles = T // tm_moe + N_EXPERT_GROUPS
        row0 = jnp.arange(n_row_tiles) * tm_moe
        tile_grp = jnp.minimum(jnp.sum(((seg_start + seg)[None, :] <= row0[:, None]).astype(jnp.int32), axis=1),
                               N_EXPERT_GROUPS - 1)
        dest = pl.pallas_call(
            _route_kernel, name="moe_route", grid=(n_tok_tiles,),
            in_specs=[pl.BlockSpec((1, tm), lambda i: (0, i)),
                      pl.BlockSpec((1, N_EXPERT_GROUPS, LANES), lambda i: (i, 0, 0)), full((tm, tm))],
            out_specs=pl.BlockSpec((1, tm), lambda i: (0, i)),
            out_shape=jax.ShapeDtypeStruct((1, T), jnp.int32),
            compiler_params=_cparams("parallel"),
        )(bgrp, jnp.broadcast_to(tile_off.astype(F32)[:, :, None], (n_tok_tiles, N_EXPERT_GROUPS, LANES)), tri)
        dest3 = dest.reshape(n_tok_tiles, 1, tm)
        dest_spec = pl.BlockSpec((1, 1, tm), lambda i: (i, 0, 0), memory_space=pltpu.SMEM)
        any_spec = pl.BlockSpec(memory_space=pl.ANY)

        n_rows = n_row_tiles * tm_moe
        xs = pl.pallas_call(
            _dispatch_kernel, name="moe_dispatch", grid=(n_tok_tiles,),
            in_specs=[dest_spec, row_spec(DX), any_spec],
            out_specs=any_spec,
            out_shape=jax.ShapeDtypeStruct((n_rows, DX), F32),
            scratch_shapes=[pltpu.SemaphoreType.DMA(())],
            input_output_aliases={2: 0},
            compiler_params=_cparams("arbitrary"),
        )(dest3, hx, jnp.zeros((n_rows, DX), F32))

        per = E // N_EXPERT_GROUPS
        ys = pl.pallas_call(
            _moe_kernel, name="moe",
            grid_spec=pltpu.PrefetchScalarGridSpec(
                num_scalar_prefetch=1, grid=(n_row_tiles, per),
                in_specs=[pl.BlockSpec((tm_moe, DX), lambda k, u, grp: (k, 0)),
                          pl.BlockSpec((None, 1, D, DE), lambda k, u, grp: (l, grp[k] * per + u, 0, 0)),
                          pl.BlockSpec((None, 1, D, DE), lambda k, u, grp: (l, grp[k] * per + u, 0, 0)),
                          pl.BlockSpec((None, 1, DE, D), lambda k, u, grp: (l, grp[k] * per + u, 0, 0))],
                out_specs=pl.BlockSpec((tm_moe, D), lambda k, u, grp: (k, 0)),
                scratch_shapes=[pltpu.VMEM((tm_moe, D), F32)]),
            out_shape=jax.ShapeDtypeStruct((n_rows, D), F32),
            compiler_params=_cparams("parallel", "arbitrary"),
        )(tile_grp, xs, moe_w_gate, moe_w_up, moe_w_down)

        dest_next = pl.BlockSpec((1, 1, tm), lambda i: (jnp.minimum(i + 1, n_tok_tiles - 1), 0, 0),
                                 memory_space=pltpu.SMEM)
        x2 = pl.pallas_call(
            _combine_kernel, name="moe_combine", grid=(n_tok_tiles,),
            in_specs=[dest_spec, dest_next, any_spec, row_spec(D),
                      pl.BlockSpec((1, 6, D), lambda i: (i // tiles_per_batch, 0, 0))],
            out_specs=row_spec(D),
            out_shape=jax.ShapeDtypeStruct((T, D), F32),
            scratch_shapes=[pltpu.VMEM((2, tm, D), F32), pltpu.SemaphoreType.DMA((2,))],
            compiler_params=_cparams("arbitrary"),
        )(dest3, dest3, ys, x_new, mod_l)

    return x2.reshape(B, S, D)
```

```python
import functools

import numpy as np
import jax
import jax.numpy as jnp
from jax import lax
from jax.experimental import pallas as pl
from jax.experimental.pallas import tpu as pltpu

F32 = jnp.float32
BF16 = jnp.bfloat16

HEAD_DIM = 64
LANES = 128
RET_HEADS = 4
GMLP_GROUPS = 4
NSA_HEADS = 8
NSA_KV_HEADS = 2
NSA_GROUP = NSA_HEADS // NSA_KV_HEADS
N_BRANCH = 3
CHUNK = 128
CMP_LEN = 32
CMP_STRIDE = 16
SEL_BLOCK = 64
SEL_TOPK = 8
WINDOW = 512
ROPE_THETA = 10000.0
N_EXPERT_GROUPS = 4
EXPERT_TOPK = 2
NORM_EPS = 1e-6
NEG_INF = -1e30
FORCE_SCORE = 1e4

RET_W = RET_HEADS * HEAD_DIM
GMLP_W = GMLP_GROUPS * HEAD_DIM
NSA_W = NSA_HEADS * HEAD_DIM
KV_W = NSA_KV_HEADS * HEAD_DIM
GATE_W = NSA_HEADS * N_BRANCH
CMP_ROW = CMP_STRIDE * KV_W
OFF_RET = 0
OFF_GMLP = 4 * RET_W
OFF_Q = OFF_GMLP + 2 * GMLP_W
OFF_KV = OFF_Q + NSA_W
OFF_GL = OFF_KV + 6 * KV_W
N_IN = OFF_GL + GATE_W
N_IN_PAD = OFF_GL + LANES

Q_TILE = 256
KEY_CHUNK = 512
MIX_BATCH = 2
ROW_BLOCK = 32
ROW_UNROLL = True
VMEM_LIMIT = 48 * 1024 * 1024


def _cparams(*sem):
    return pltpu.CompilerParams(dimension_semantics=sem, vmem_limit_bytes=VMEM_LIMIT)


def _dot(a, b):
    return jnp.dot(a.astype(BF16), b.astype(BF16), preferred_element_type=F32)


def _dot_nt(a, b):
    return lax.dot_general(a.astype(BF16), b.astype(BF16), (((1,), (1,)), ((), ())),
                           preferred_element_type=F32)


def _split(a):
    hi = a.astype(BF16)
    lo = (a - hi.astype(F32)).astype(BF16)
    return hi, lo


def _dot_split_lhs(a, b):
    hi, lo = _split(a)
    return (jnp.dot(hi, b, preferred_element_type=F32) + jnp.dot(lo, b, preferred_element_type=F32))


def _dot_split_rhs(a, b):
    hi, lo = _split(b)
    return (jnp.dot(a, hi, preferred_element_type=F32) + jnp.dot(a, lo, preferred_element_type=F32))


def _head_ones(width=LANES):
    r = lax.broadcasted_iota(jnp.int32, (width, width), 0) // HEAD_DIM
    c = lax.broadcasted_iota(jnp.int32, (width, width), 1) // HEAD_DIM
    return jnp.where(r == c, 1.0, 0.0).astype(BF16)


def _rotate_half(y):
    lane = lax.broadcasted_iota(jnp.int32, y.shape, 1)
    first = (lane & (HEAD_DIM - 1)) < (HEAD_DIM // 2)
    return jnp.where(first, -pltpu.roll(y, LANES - HEAD_DIM // 2, 1), pltpu.roll(y, HEAD_DIM // 2, 1))


def _rope_slab(y, cos, sin):
    return y * cos + _rotate_half(y) * sin


def _rope(x, cos, sin):
    return jnp.concatenate(
        [_rope_slab(x[:, k * LANES:(k + 1) * LANES], cos, sin) for k in range(x.shape[1] // LANES)], axis=1)


def _head_rms_rope(x, gain, cos, sin, ones):
    outs = []
    for k in range(x.shape[1] // LANES):
        xs = x[:, k * LANES:(k + 1) * LANES]
        ssq = _dot_split_lhs(xs * xs, ones)
        y = xs * lax.rsqrt(ssq * (1.0 / HEAD_DIM) + NORM_EPS) * gain[:, k * LANES:(k + 1) * LANES]
        outs.append(_rope_slab(y, cos, sin))
    return outs[0] if len(outs) == 1 else jnp.concatenate(outs, axis=1)


def _rms_mod(x, gain, scale, shift):
    y = x * lax.rsqrt(jnp.mean(x * x, axis=-1, keepdims=True) + NORM_EPS)
    return (y * gain) * (1.0 + scale) + shift


def _softmax_rows(s):
    e = jnp.exp(s - jnp.max(s, axis=-1, keepdims=True))
    return e / jnp.sum(e, axis=-1, keepdims=True)


def _mask_bias(keep):
    return jnp.where(keep, 0.0, NEG_INF)


def _rope_table_kernel(pos_ref, invf_ref, cos_ref, sin_ref):
    ang = pos_ref[0].astype(F32) * invf_ref[...]
    cos_ref[0] = jnp.cos(ang)
    sin_ref[0] = jnp.sin(ang)


def _ada_kernel(c_ref, w_ref, b_ref, o_ref):
    c = c_ref[...]
    a = c * jax.nn.sigmoid(c)
    o_ref[0, 0] = jnp.dot(a, w_ref[0], preferred_element_type=F32,
                          precision=lax.Precision.HIGHEST) + b_ref[0, 0]


_IN_RAW = ((OFF_RET, OFF_GMLP), (OFF_GMLP, OFF_Q), (OFF_KV, OFF_KV + KV_W), (OFF_KV + KV_W, OFF_KV + 2 * KV_W),
           (OFF_GL, N_IN_PAD))


def _in_proj_kernel(x_ref, mod_ref, g_ref, w_ref, cos_ref, sin_ref, qg_ref, kg_ref, *out_refs):
    *raw_refs, q_out, kst_out, vs_out, kwt_out, vw_out = out_refs
    mod = mod_ref[0]
    h = _rms_mod(x_ref[...], g_ref[...], mod[1:2], mod[0:1]).astype(BF16)
    proj = lambda lo, hi: jnp.dot(h, w_ref[:, lo:hi], preferred_element_type=F32)
    for ref, (lo, hi) in zip(raw_refs, _IN_RAW):
        ref[...] = proj(lo, hi)

    ones = _head_ones()
    cos, sin = cos_ref[...], sin_ref[...]
    q_out[...] = (_head_rms_rope(proj(OFF_Q, OFF_KV), qg_ref[...], cos, sin, ones) * (HEAD_DIM ** -0.5)).astype(BF16)
    sel_kv = OFF_KV + 2 * KV_W
    kst_out[0] = _head_rms_rope(proj(sel_kv, sel_kv + KV_W), kg_ref[1:2], cos, sin, ones).T.astype(BF16)
    vs_out[...] = proj(sel_kv + KV_W, sel_kv + 2 * KV_W).astype(BF16)
    kwt_out[0] = _head_rms_rope(proj(sel_kv + 2 * KV_W, sel_kv + 3 * KV_W), kg_ref[2:3], cos, sin, ones).T.astype(BF16)
    vw_out[...] = proj(sel_kv + 3 * KV_W, OFF_GL).astype(BF16)


def _mixer_kernel(zr_ref, zg_ref, cos_ref, sin_ref, din_ref, dq_ref, dk_ref, dc_ref, rg_ref,
                  lng_ref, ws_ref, bs_ref, yret_ref, ygm_ref, state_ref):
    @pl.when(pl.program_id(1) == 0)
    def _():
        state_ref[...] = jnp.zeros_like(state_ref)

    row = lax.broadcasted_iota(jnp.int32, (CHUNK, CHUNK), 0)
    col = lax.broadcasted_iota(jnp.int32, (CHUNK, CHUNK), 1)
    ws = [jnp.where(row >= col, ws_ref[gi], 0.0).astype(BF16) for gi in range(GMLP_GROUPS)]

    for bb in range(zr_ref.shape[0]):
        cos, sin = cos_ref[bb], sin_ref[bb]
        q = _rope(zr_ref[bb, :, 0:RET_W], cos, sin)
        k = _rope(zr_ref[bb, :, RET_W:2 * RET_W], cos, sin) * (HEAD_DIM ** -0.5)
        v = zr_ref[bb, :, 2 * RET_W:3 * RET_W]
        g = zr_ref[bb, :, 3 * RET_W:4 * RET_W]
        outs = []
        for h in range(RET_HEADS):
            sl = slice(h * HEAD_DIM, (h + 1) * HEAD_DIM)
            qh, kh, vh = q[:, sl], k[:, sl], v[:, sl]
            st = state_ref[bb, h]
            inner = _dot_nt(qh, kh) * din_ref[h]
            o = _dot(inner, vh) + _dot(qh, st) * dq_ref[h]
            state_ref[bb, h] = st * dc_ref[h] + _dot((kh * dk_ref[h]).T, vh)
            o = o * lax.rsqrt(jnp.mean(o * o, axis=-1, keepdims=True) + NORM_EPS) * rg_ref[h]
            outs.append(o)
        yret_ref[bb] = (jnp.concatenate(outs, axis=1) * (g * jax.nn.sigmoid(g))).astype(BF16)

        u = jax.nn.gelu(zg_ref[bb, :, 0:GMLP_W])
        vv = jax.nn.gelu(zg_ref[bb, :, GMLP_W:2 * GMLP_W])
        mu = jnp.mean(vv, axis=-1, keepdims=True)
        var = jnp.mean(jnp.square(vv - mu), axis=-1, keepdims=True)
        vn = (vv - mu) * lax.rsqrt(var + NORM_EPS) * lng_ref[...]
        ss = [_dot(ws[gi], vn[:, gi * HEAD_DIM:(gi + 1) * HEAD_DIM]) + bs_ref[gi] for gi in range(GMLP_GROUPS)]
        ygm_ref[bb] = (u * jnp.concatenate(ss, axis=1)).astype(BF16)


def _compress_kernel(xk_ref, xv_ref, pek_ref, pev_ref, w1k_ref, w2k_ref, w1v_ref, w2v_ref, kg_ref,
                     cos_ref, sin_ref, kc_ref, vc_ref):
    n_rows = xk_ref.shape[0]

    def comp(x, pe_ref, w1_ref, w2_ref):
        a = _dot(x + pe_ref[0:1], w1_ref[0])
        b = _dot(x + pe_ref[1:2], w1_ref[1])
        h = a + pltpu.roll(b, n_rows - 1, 0)
        return _dot(jax.nn.gelu(h), w2_ref[...])

    kc_ref[0] = _head_rms_rope(comp(xk_ref[...], pek_ref, w1k_ref, w2k_ref), kg_ref[0:1],
                               cos_ref[0], sin_ref[0], _head_ones())
    vc_ref[0] = comp(xv_ref[...], pev_ref, w1v_ref, w2v_ref)


def _nsa_kernel(q_ref, gl_ref, kc_ref, vc_ref, kst_ref, vs_ref, kwt_ref, vw_ref, gexp_ref, expand_ref, o_ref,
                s_scr, p_scr, m_scr, acc_scr, *, n_sel, top_n):
    i = pl.program_id(1)
    qt = q_ref.shape[0]
    n_cmp = kc_ref.shape[1]
    t0 = i * qt
    n_chunks = (t0 + qt + KEY_CHUNK - 1) // KEY_CHUNK
    wk = WINDOW + qt
    wstart = pl.multiple_of(jnp.maximum(t0 - WINDOW, 0), LANES)

    t_r = t0 + lax.broadcasted_iota(jnp.int32, (qt, n_cmp), 0)
    n_c = lax.broadcasted_iota(jnp.int32, (qt, n_cmp), 1)
    keep_c = jnp.where(n_c * CMP_STRIDE + (CMP_LEN - 1) <= t_r, 1.0, 0.0)
    bias_c = (keep_c - 1.0) * (-NEG_INF)
    t_l = t0 + lax.broadcasted_iota(jnp.int32, (n_cmp, qt), 1)
    n_s = lax.broadcasted_iota(jnp.int32, (n_cmp, qt), 0)
    keep_ct = jnp.where(n_s * CMP_STRIDE + (CMP_LEN - 1) <= t_l, 1.0, 0.0)
    keep_ct4 = jnp.concatenate([keep_ct] * NSA_GROUP, axis=1)
    bias_ct4 = (keep_ct4 - 1.0) * (-NEG_INF)
    jo = lax.broadcasted_iota(jnp.int32, (n_sel, n_cmp), 0) * SEL_BLOCK
    no = lax.broadcasted_iota(jnp.int32, (n_sel, n_cmp), 1) * CMP_STRIDE
    overlap = jnp.where((no < jo + SEL_BLOCK) & (no + CMP_LEN > jo), 1.0, 0.0).astype(BF16)
    j = lax.broadcasted_iota(jnp.int32, (n_sel, qt), 0)
    tq = t0 + lax.broadcasted_iota(jnp.int32, (n_sel, qt), 1)
    cur = tq // SEL_BLOCK
    forced = (j == 0) | (j == cur) | (j == cur - 1)
    allowed = j * SEL_BLOCK <= tq
    t_w = t0 + lax.broadcasted_iota(jnp.int32, (qt, wk), 0)
    s_w = wstart + lax.broadcasted_iota(jnp.int32, (qt, wk), 1)
    bias_w = _mask_bias((s_w <= t_w) & (t_w - s_w < WINDOW))
    t_k = t0 + lax.broadcasted_iota(jnp.int32, (qt, KEY_CHUNK), 0)
    s_k = lax.broadcasted_iota(jnp.int32, (qt, KEY_CHUNK), 1)

    rep = lambda a: jnp.concatenate([a] * NSA_GROUP, axis=0)
    keep_c4, bias_c4 = rep(keep_c), rep(bias_c)
    unstack = lambda o, gs: [o[r * qt:(r + 1) * qt, gs] for r in range(NSA_GROUP)]
    rows = NSA_GROUP * qt
    lane_v = lax.broadcasted_iota(jnp.int32, (1, KV_W), 1)

    def attend(g, qs, kt, vv, bias, first):
        width = bias.shape[1]
        own = (lane_v >= g * HEAD_DIM) & (lane_v < (g + 1) * HEAD_DIM)
        vv1 = jnp.where(own, vv, jnp.ones_like(vv))
        s = jnp.dot(qs, kt, preferred_element_type=F32) + rep(bias)
        s_scr[g, :, 0:width] = s
        mx = jnp.max(s, axis=-1, keepdims=True)
        if first:
            m_scr[g] = jnp.broadcast_to(mx, (rows, LANES))
        else:
            m_old = m_scr[g]
            m_new = jnp.maximum(m_old, mx)
            alpha = jnp.exp(m_old - m_new)
            m_scr[g] = m_new

        def exp_pass(b, carry):
            r0 = pl.multiple_of(b * ROW_BLOCK, ROW_BLOCK)
            m_new = m_scr[g, pl.ds(r0, ROW_BLOCK), :]
            p = jnp.exp(s_scr[g, pl.ds(r0, ROW_BLOCK), 0:width]
                        - jnp.concatenate([m_new] * (width // LANES), axis=1))
            p_scr[g, pl.ds(r0, ROW_BLOCK), 0:width] = p.astype(BF16)
            return carry

        lax.fori_loop(0, rows // ROW_BLOCK, exp_pass, 0, unroll=ROW_UNROLL)
        pv = jnp.dot(p_scr[g, :, 0:width], vv1, preferred_element_type=F32)
        acc_scr[g] = pv if first else alpha * acc_scr[g] + pv

    def finish(g):
        acc = acc_scr[g]
        num, den = acc[:, g * HEAD_DIM:(g + 1) * HEAD_DIM], acc[:, (1 - g) * HEAD_DIM:(2 - g) * HEAD_DIM]
        return unstack(num / den, slice(None))

    o_cmp, o_win, qs_g, sel_g = [], [], [], []
    for g in range(NSA_KV_HEADS):
        gs = slice(g * HEAD_DIM, (g + 1) * HEAD_DIM)
        qs = jnp.concatenate([q_ref[:, (g * NSA_GROUP + r) * HEAD_DIM:(g * NSA_GROUP + r + 1) * HEAD_DIM]
                              for r in range(NSA_GROUP)], axis=0)
        qs_g.append(qs)
        kc = kc_ref[0][:, gs].astype(BF16)
        vc = vc_ref[0].astype(BF16)

        p = _softmax_rows(_dot_nt(qs, kc) + bias_c4) * keep_c4
        o_cmp += unstack(_dot(p, vc), gs)

        st = _dot_nt(kc, qs) + bias_ct4
        e = jnp.exp(st - jnp.max(st, axis=0, keepdims=True))
        pt = e / jnp.sum(e, axis=0, keepdims=True) * keep_ct4
        psum = pt[:, 0:qt]
        for r in range(1, NSA_GROUP):
            psum = psum + pt[:, r * qt:(r + 1) * qt]
        imp = _dot_split_rhs(overlap, psum)
        score = jnp.where(forced, FORCE_SCORE, jnp.where(allowed, imp, -1.0))
        rank = jnp.zeros((n_sel, qt), F32)
        for jp in range(n_sel):
            other = score[jp:jp + 1, :]
            tie = jnp.where(j > jp, 1.0, 0.0)
            rank = rank + jnp.where(other > score, 1.0, jnp.where(other == score, tie, 0.0))
        sel_t = jnp.where(rank < top_n, 1.0, 0.0)
        if n_sel < LANES:
            sel_t = jnp.concatenate([sel_t, jnp.zeros((LANES - n_sel, qt), F32)], axis=0)
        sel_g.append(sel_t.T.astype(BF16))

        attend(g, qs, kwt_ref[0, gs, pl.ds(wstart, wk)], vw_ref[pl.ds(wstart, wk), :], bias_w, True)
        o_win += finish(g)

    def chunk(c, first):
        k0 = pl.multiple_of(c * KEY_CHUNK, KEY_CHUNK)
        vv = vs_ref[pl.ds(k0, KEY_CHUNK), :]
        ex = expand_ref[:, pl.ds(k0, KEY_CHUNK)]
        causal = k0 + s_k <= t_k
        for g in range(NSA_KV_HEADS):
            chosen = jnp.dot(sel_g[g], ex, preferred_element_type=F32)
            attend(g, qs_g[g], kst_ref[0, g * HEAD_DIM:(g + 1) * HEAD_DIM, pl.ds(k0, KEY_CHUNK)], vv,
                   _mask_bias((chosen > 0.5) & causal), first)

    chunk(0, True)

    def body(c, carry):
        chunk(c, False)
        return carry

    lax.fori_loop(1, n_chunks, body, 0)
    o_sel = finish(0) + finish(1)

    gx = _dot_split_lhs(jax.nn.sigmoid(gl_ref[...]), gexp_ref[...])
    cat = lambda parts: jnp.concatenate(parts, axis=1)
    o_ref[...] = (gx[:, 0:NSA_W] * cat(o_cmp) + gx[:, NSA_W:2 * NSA_W] * cat(o_sel)
                  + gx[:, 2 * NSA_W:3 * NSA_W] * cat(o_win)).astype(BF16)


def _out_proj_kernel(yret_ref, ygm_ref, ynsa_ref, x_ref, mod_ref, wout_ref,
                     g2_ref, rwt_ref, rb_ref, xn_ref, hx_ref, bg_ref, cnt_ref, *, n_experts):
    mod = mod_ref[0]
    d_model = x_ref.shape[1]
    y = (jnp.dot(yret_ref[...], wout_ref[0:RET_W], preferred_element_type=F32)
         + jnp.dot(ygm_ref[...], wout_ref[RET_W:RET_W + GMLP_W], preferred_element_type=F32)
         + jnp.dot(ynsa_ref[...], wout_ref[RET_W + GMLP_W:], preferred_element_type=F32))
    xn = x_ref[...] + mod[2:3] * y
    xn_ref[...] = xn
    h2 = _rms_mod(xn, g2_ref[...], mod[4:5], mod[3:4])
    hx_ref[:, 0:d_model] = h2

    hh, hl = _split(h2)
    wh, wl = rwt_ref[0], rwt_ref[1]
    nt = lambda a, b: lax.dot_general(a, b, (((1,), (1,)), ((), ())), preferred_element_type=F32)
    logits = nt(wh, hh) + nt(wh, hl) + nt(wl, hh)
    ex = jnp.exp(logits - jnp.max(logits, axis=0, keepdims=True))
    probs = ex / jnp.sum(ex, axis=0, keepdims=True)
    sel = probs + rb_ref[...]
    per = n_experts // N_EXPERT_GROUPS
    srow = [sel[e:e + 1, :] for e in range(n_experts)]
    prow = [probs[e:e + 1, :] for e in range(n_experts)]
    gscore = []
    for gi in range(N_EXPERT_GROUPS):
        a = srow[gi * per:(gi + 1) * per]
        best = None
        for u in range(per):
            for w in range(u + 1, per):
                pair = a[u] + a[w]
                best = pair if best is None else jnp.maximum(best, pair)
        gscore.append(best)
    bg = jnp.zeros_like(gscore[0], dtype=jnp.int32)
    bs = gscore[0]
    for gi in range(1, N_EXPERT_GROUPS):
        upd = gscore[gi] > bs
        bg = jnp.where(upd, gi, bg)
        bs = jnp.where(upd, gscore[gi], bs)
    cs, cp = [], []
    for u in range(per):
        su, pu = srow[u], prow[u]
        for gi in range(1, N_EXPERT_GROUPS):
            su = jnp.where(bg == gi, srow[gi * per + u], su)
            pu = jnp.where(bg == gi, prow[gi * per + u], pu)
        cs.append(su)
        cp.append(pu)
    chosen = []
    for u in range(per):
        rank = jnp.zeros_like(cs[u])
        for w in range(per):
            if w == u:
                continue
            before = (cs[w] >= cs[u]) if w < u else (cs[w] > cs[u])
            rank = rank + jnp.where(before, 1.0, 0.0)
        chosen.append(jnp.where(rank < EXPERT_TOPK, 1.0, 0.0))
    denom = jnp.zeros_like(cp[0])
    for u in range(per):
        denom = denom + chosen[u] * cp[u]
    tm = bg.shape[1]
    grows = [chosen[u] * cp[u] / denom for u in range(per)]
    gpad = jnp.concatenate(grows + [jnp.zeros((LANES - per, tm), F32)], axis=0)
    hx_ref[:, d_model:d_model + LANES] = gpad.T
    bg_ref[...] = bg
    cnt_ref[0] = jnp.concatenate(
        [jnp.broadcast_to(jnp.sum(jnp.where(bg == gi, 1.0, 0.0), axis=1, keepdims=True), (1, LANES))
         for gi in range(N_EXPERT_GROUPS)], axis=0)


def _route_kernel(bg_ref, off_ref, tri_ref, dest_ref):
    bg = bg_ref[...]
    tm = bg.shape[1]
    member = [jnp.where(bg == gi, 1.0, 0.0) for gi in range(N_EXPERT_GROUPS)]
    pad = jnp.zeros((8 - N_EXPERT_GROUPS, tm), F32)
    before = jnp.dot(jnp.concatenate(member + [pad], axis=0).astype(BF16), tri_ref[...],
                     preferred_element_type=F32)
    off = off_ref[0]
    dest = jnp.zeros((1, tm), F32)
    for gi in range(N_EXPERT_GROUPS):
        start = jnp.concatenate([off[gi:gi + 1, :]] * (tm // LANES), axis=1)
        dest = dest + member[gi] * (start + before[gi:gi + 1, :])
    dest_ref[...] = dest.astype(jnp.int32)


def _dispatch_kernel(dest_ref, hx_ref, init_ref, xs_ref, sem):
    del init_ref
    n = hx_ref.shape[0]

    def row_copy(r, d):
        return pltpu.make_async_copy(hx_ref.at[pl.ds(r, 1)], xs_ref.at[pl.ds(d, 1)], sem)

    def start(r, carry):
        row_copy(r, dest_ref[0, 0, r]).start()
        return carry

    def wait(r, carry):
        row_copy(0, 0).wait()
        return carry

    lax.fori_loop(0, n, start, 0, unroll=8)
    lax.fori_loop(0, n, wait, 0, unroll=8)


def _moe_kernel(grp_ref, xs_ref, wg_ref, wu_ref, wd_ref, y_ref, acc_ref):
    del grp_ref
    u = pl.program_id(1)
    d_model = y_ref.shape[1]

    @pl.when(u == 0)
    def _():
        acc_ref[...] = jnp.zeros_like(acc_ref)

    h = xs_ref[:, 0:d_model].astype(BF16)
    gates = xs_ref[:, d_model:d_model + LANES]
    lane = lax.broadcasted_iota(jnp.int32, gates.shape, 1)
    gcol = jnp.sum(jnp.where(lane == u, gates, 0.0), axis=-1, keepdims=True)
    hg = jnp.dot(h, wg_ref[0].astype(BF16), preferred_element_type=F32)
    hu = jnp.dot(h, wu_ref[0].astype(BF16), preferred_element_type=F32)
    a = (hg * jax.nn.sigmoid(hg)) * hu
    acc_ref[...] += _dot(gcol * a, wd_ref[0])

    @pl.when(u == pl.num_programs(1) - 1)
    def _():
        y_ref[...] = acc_ref[...]


def _combine_kernel(cur_ref, nxt_ref, y_ref, xn_ref, mod_ref, o_ref, buf, sems):
    i = pl.program_id(0)
    n = xn_ref.shape[0]

    def request(idx_ref, slot):
        def start(r, carry):
            pltpu.make_async_copy(y_ref.at[pl.ds(idx_ref[0, 0, r], 1)], buf.at[slot, pl.ds(r, 1)],
                                  sems.at[slot]).start()
            return carry
        lax.fori_loop(0, n, start, 0, unroll=8)

    @pl.when(i == 0)
    def _():
        request(cur_ref, 0)

    @pl.when(i < pl.num_programs(0) - 1)
    def _():
        request(nxt_ref, (i + 1) % 2)

    slot = i % 2

    def wait(r, carry):
        pltpu.make_async_copy(y_ref.at[pl.ds(0, 1)], buf.at[slot, pl.ds(0, 1)], sems.at[slot]).wait()
        return carry

    lax.fori_loop(0, n, wait, 0, unroll=8)
    o_ref[...] = xn_ref[...] + mod_ref[0][5:6] * buf[slot]


def _decay_tables():
    H, C, d = RET_HEADS, CHUNK, HEAD_DIM
    log_gamma = jnp.log1p(-jnp.power(2.0, -5.0 - jnp.arange(H, dtype=F32)))
    idx = jnp.arange(C, dtype=F32)
    diff = idx[:, None] - idx[None, :]
    d_in = jnp.where(diff >= 0, jnp.exp(jnp.maximum(diff, 0.0)[None] * log_gamma[:, None, None]), 0.0).astype(F32)
    d_q = jnp.exp((idx + 1.0)[:, None] * log_gamma[None]).astype(F32)
    d_k = jnp.exp((C - 1.0 - idx)[:, None] * log_gamma[None]).astype(F32)
    d_c = jnp.exp(C * log_gamma).astype(F32)
    bcast = lambda a: jnp.broadcast_to(a.T[:, :, None], (H, C, d))
    return d_in, bcast(d_q), bcast(d_k), jnp.broadcast_to(d_c[:, None, None], (H, d, d))


def _gate_expand():
    m = np.zeros((LANES, N_BRANCH * NSA_W), np.float32)
    for h in range(NSA_HEADS):
        for br in range(N_BRANCH):
            m[h * N_BRANCH + br, br * NSA_W + h * HEAD_DIM: br * NSA_W + (h + 1) * HEAD_DIM] = 1.0
    return jnp.asarray(m, BF16)


def _block_expand(seq):
    m = (np.arange(LANES)[:, None] == (np.arange(seq) // SEL_BLOCK)[None, :]).astype(np.float32)
    return jnp.asarray(m, BF16)


def _compress_weights(pe, w1, w2):
    G, d = NSA_KV_HEADS, HEAD_DIM
    hid = w1.shape[-1]
    eye = jnp.eye(G, dtype=F32)
    w1h = w1.reshape(2, CMP_STRIDE, d, hid)
    w1b = jnp.einsum('pldh,ge->plgdeh', w1h, eye).reshape(2, CMP_ROW, G * hid)
    w2b = jnp.einsum('hd,ge->ghed', w2, eye).reshape(G * hid, G * d)
    peb = jnp.broadcast_to(pe.reshape(2, CMP_STRIDE, 1, d), (2, CMP_STRIDE, G, d)).reshape(2, CMP_ROW)
    return peb, w1b.astype(BF16), w2b.astype(BF16)


def kernel(x, c, positions, ada_w, ada_b, norm_mix_g, norm_ffn_g, w_in, w_out, ret_norm_g, gmlp_ln_g, gmlp_ws,
           gmlp_b, nsa_q_norm_g, nsa_k_norm_g, cmp_pe_k, cmp_pe_v, cmp_w1_k, cmp_w2_k, cmp_w1_v, cmp_w2_v,
           router_w, router_b, moe_w_gate, moe_w_up, moe_w_down):
    B, S, D = x.shape
    L = ada_w.shape[0]
    E = router_w.shape[1]
    DE = moe_w_gate.shape[-1]
    T = B * S
    n_chunk = S // CHUNK
    n_cmp = S // CMP_STRIDE
    n_sel = S // SEL_BLOCK
    top_n = min(SEL_TOPK, n_sel)
    n_qt = S // Q_TILE
    tm = min(512, S)
    tm_moe = min(1024, S)
    assert S % KEY_CHUNK == 0 and S >= WINDOW + Q_TILE and D == RET_W + GMLP_W + NSA_W
    assert w_in.shape[-1] == N_IN and T % tm_moe == 0 and n_sel <= LANES and NSA_KV_HEADS == 2

    half = HEAD_DIM // 2
    inv_freq = jnp.power(ROPE_THETA, -jnp.arange(half, dtype=F32) / half)
    invf = jnp.tile(inv_freq, LANES // half)[None, :]
    cos_t, sin_t = pl.pallas_call(
        _rope_table_kernel, name="rope_tables", grid=(B,),
        in_specs=[pl.BlockSpec((1, S, 1), lambda b: (b, 0, 0)), pl.BlockSpec((1, LANES), lambda b: (0, 0))],
        out_specs=[pl.BlockSpec((1, S, LANES), lambda b: (b, 0, 0))] * 2,
        out_shape=[jax.ShapeDtypeStruct((B, S, LANES), F32)] * 2,
        compiler_params=_cparams("parallel"),
    )(positions[:, :, None], invf)
    pad_c = lambda a: jnp.pad(a[:, CMP_LEN - 1::CMP_STRIDE], ((0, 0), (0, 1), (0, 0)))
    cos_c, sin_c = pad_c(cos_t), pad_c(sin_t)
    cos_f, sin_f = cos_t.reshape(T, LANES), sin_t.reshape(T, LANES)

    mod = pl.pallas_call(
        _ada_kernel, name="ada_mod", grid=(L, 6),
        in_specs=[pl.BlockSpec((B, D), lambda l, j: (0, 0)),
                  pl.BlockSpec((1, D, D), lambda l, j: (l, 0, j)),
                  pl.BlockSpec((1, 1, 1, D), lambda l, j: (l, j, 0, 0))],
        out_specs=pl.BlockSpec((1, 1, B, D), lambda l, j: (l, j, 0, 0)),
        out_shape=jax.ShapeDtypeStruct((L, 6, B, D), F32),
        compiler_params=_cparams("parallel", "parallel"),
    )(c, ada_w, ada_b.reshape(L, 6, 1, D))
    mod = mod.transpose(0, 2, 1, 3)

    d_in, d_q, d_k, d_c = _decay_tables()
    gexp = _gate_expand()
    expand = _block_expand(S)
    rw_hi = router_w.T.astype(BF16)
    rw_lo = (router_w.T - rw_hi.astype(F32)).astype(BF16)
    rwt = jnp.stack([rw_hi, rw_lo])
    rb = router_b.reshape(E, 1)
    tri = jnp.asarray(np.triu(np.ones((tm, tm), np.float32), k=1), BF16)
    x2 = x.reshape(T, D)
    tiles_per_batch = S // tm
    row_spec = lambda w: pl.BlockSpec((tm, w), lambda i: (i, 0))
    full = lambda shape: pl.BlockSpec(shape, lambda *_: (0,) * len(shape))

    for l in range(L):
        mod_l = mod[l]
        w_in_p = jnp.pad(w_in[l], ((0, 0), (0, N_IN_PAD - N_IN))).astype(BF16)
        raw_widths = tuple(hi - lo for lo, hi in _IN_RAW)
        mod_spec = pl.BlockSpec((1, 6, D), lambda i: (i // tiles_per_batch, 0, 0))
        kt_out = pl.BlockSpec((1, KV_W, tm), lambda i: (i // tiles_per_batch, 0, i % tiles_per_batch))
        kgain = jnp.tile(nsa_k_norm_g[l], (1, NSA_KV_HEADS))
        zr, zg, zkc, zvc, zgl, q_r, ks_t, vs_b, kw_t, vw_b = pl.pallas_call(
            _in_proj_kernel, name="in_proj", grid=(T // tm,),
            in_specs=[row_spec(D), mod_spec, full((1, D)), full((D, N_IN_PAD)), row_spec(LANES), row_spec(LANES),
                      full((1, NSA_W)), full((N_BRANCH, KV_W))],
            out_specs=[row_spec(w) for w in raw_widths]
            + [row_spec(NSA_W), kt_out, row_spec(KV_W), kt_out, row_spec(KV_W)],
            out_shape=[jax.ShapeDtypeStruct((T, w), F32) for w in raw_widths]
            + [jax.ShapeDtypeStruct((T, NSA_W), BF16), jax.ShapeDtypeStruct((B, KV_W, S), BF16),
               jax.ShapeDtypeStruct((T, KV_W), BF16), jax.ShapeDtypeStruct((B, KV_W, S), BF16),
               jax.ShapeDtypeStruct((T, KV_W), BF16)],
            compiler_params=_cparams("parallel"),
        )(x2, mod_l, norm_mix_g[l][None, :], w_in_p, cos_f, sin_f,
          jnp.tile(nsa_q_norm_g[l], NSA_HEADS)[None, :], kgain)

        crow = lambda w: pl.BlockSpec((MIX_BATCH, CHUNK, w), lambda b, cc: (b, cc, 0))
        y_ret, y_gm = pl.pallas_call(
            _mixer_kernel, name="ret_gmlp", grid=(B // MIX_BATCH, n_chunk),
            in_specs=[crow(4 * RET_W), crow(2 * GMLP_W), crow(LANES), crow(LANES),
                      full((RET_HEADS, CHUNK, CHUNK)), full((RET_HEADS, CHUNK, HEAD_DIM)),
                      full((RET_HEADS, CHUNK, HEAD_DIM)), full((RET_HEADS, HEAD_DIM, HEAD_DIM)),
                      full((RET_HEADS, 1, HEAD_DIM)), full((1, GMLP_W)),
                      full((GMLP_GROUPS, CHUNK, CHUNK)), full((GMLP_GROUPS, CHUNK, HEAD_DIM))],
            out_specs=[crow(RET_W), crow(GMLP_W)],
            out_shape=[jax.ShapeDtypeStruct((B, S, RET_W), BF16), jax.ShapeDtypeStruct((B, S, GMLP_W), BF16)],
            scratch_shapes=[pltpu.VMEM((MIX_BATCH, RET_HEADS, HEAD_DIM, HEAD_DIM), F32)],
            compiler_params=_cparams("parallel", "arbitrary"),
        )(zr.reshape(B, S, 4 * RET_W), zg.reshape(B, S, 2 * GMLP_W), cos_t, sin_t, d_in, d_q, d_k, d_c,
          ret_norm_g[l][:, None, :], gmlp_ln_g[l][None, :], gmlp_ws[l],
          jnp.broadcast_to(gmlp_b[l][:, :, None], (GMLP_GROUPS, CHUNK, HEAD_DIM)))
        y_ret, y_gm = y_ret.reshape(T, RET_W), y_gm.reshape(T, GMLP_W)

        pek, w1k, w2k = _compress_weights(cmp_pe_k[l], cmp_w1_k[l], cmp_w2_k[l])
        pev, w1v, w2v = _compress_weights(cmp_pe_v[l], cmp_w1_v[l], cmp_w2_v[l])
        cblk = pl.BlockSpec((n_cmp, CMP_ROW), lambda b: (b, 0))
        ghid = w1k.shape[-1]
        ctab = pl.BlockSpec((1, n_cmp, LANES), lambda b: (b, 0, 0))
        kc, vc = pl.pallas_call(
            _compress_kernel, name="nsa_compress", grid=(B,),
            in_specs=[cblk, cblk, full((2, CMP_ROW)), full((2, CMP_ROW)),
                      full((2, CMP_ROW, ghid)), full((ghid, KV_W)), full((2, CMP_ROW, ghid)), full((ghid, KV_W)),
                      full((N_BRANCH, KV_W)), ctab, ctab],
            out_specs=[ctab, ctab],
            out_shape=[jax.ShapeDtypeStruct((B, n_cmp, KV_W), F32)] * 2,
            compiler_params=_cparams("parallel"),
        )(zkc.reshape(T // CMP_STRIDE, CMP_ROW), zvc.reshape(T // CMP_STRIDE, CMP_ROW), pek, pev, w1k, w2k, w1v, w2v,
          kgain, cos_c, sin_c)

        qrow = lambda w: pl.BlockSpec((Q_TILE, w), lambda b, i: (b * n_qt + i, 0))
        kt_in = pl.BlockSpec((1, KV_W, S), lambda b, i: (b, 0, 0))
        v_in = pl.BlockSpec((S, KV_W), lambda b, i: (b, 0))
        ctab2 = pl.BlockSpec((1, n_cmp, KV_W), lambda b, i: (b, 0, 0))
        y_nsa = pl.pallas_call(
            functools.partial(_nsa_kernel, n_sel=n_sel, top_n=top_n), name="nsa_attn", grid=(B, n_qt),
            in_specs=[qrow(NSA_W), qrow(LANES), ctab2, ctab2, kt_in, v_in, kt_in, v_in,
                      full((LANES, N_BRANCH * NSA_W)), full((LANES, S))],
            out_specs=qrow(NSA_W),
            out_shape=jax.ShapeDtypeStruct((T, NSA_W), BF16),
            scratch_shapes=[
                pltpu.VMEM((NSA_KV_HEADS, NSA_GROUP * Q_TILE, WINDOW + Q_TILE), F32),
                pltpu.VMEM((NSA_KV_HEADS, NSA_GROUP * Q_TILE, WINDOW + Q_TILE), BF16),
                pltpu.VMEM((NSA_KV_HEADS, NSA_GROUP * Q_TILE, LANES), F32),
                pltpu.VMEM((NSA_KV_HEADS, NSA_GROUP * Q_TILE, KV_W), F32)],
            compiler_params=_cparams("parallel", "parallel"),
        )(q_r, zgl, kc, vc, ks_t, vs_b, kw_t, vw_b, gexp, expand)

        n_tok_tiles = T // tm
        DX = D + LANES
        x_new, hx, bgrp, cnt = pl.pallas_call(
            functools.partial(_out_proj_kernel, n_experts=E), name="out_proj_router", grid=(n_tok_tiles,),
            in_specs=[row_spec(RET_W), row_spec(GMLP_W), row_spec(NSA_W), row_spec(D),
                      pl.BlockSpec((1, 6, D), lambda i: (i // tiles_per_batch, 0, 0)),
                      full((D, D)), full((1, D)), full((2, E, D)), full((E, 1))],
            out_specs=[row_spec(D), row_spec(DX), pl.BlockSpec((1, tm), lambda i: (0, i)),
                       pl.BlockSpec((1, N_EXPERT_GROUPS, LANES), lambda i: (i, 0, 0))],
            out_shape=[jax.ShapeDtypeStruct((T, D), F32), jax.ShapeDtypeStruct((T, DX), F32),
                       jax.ShapeDtypeStruct((1, T), jnp.int32),
                       jax.ShapeDtypeStruct((n_tok_tiles, N_EXPERT_GROUPS, LANES), F32)],
            compiler_params=_cparams("parallel"),
        )(y_ret, y_gm, y_nsa, x2, mod_l, w_out[l].astype(BF16), norm_ffn_g[l][None, :], rwt, rb)

        tile_cnt = cnt[:, :, 0].astype(jnp.int32)
        seg = ((jnp.sum(tile_cnt, axis=0) + tm_moe - 1) // tm_moe) * tm_moe
        earlier = lambda n: jnp.arange(n)[:, None] > jnp.arange(n)[None, :]
        seg_start = jnp.sum(jnp.where(earlier(N_EXPERT_GROUPS), seg[None, :], 0), axis=1)
        tile_off = seg_start[None, :] + jnp.sum(
            jnp.where(earlier(n_tok_tiles)[:, :, None], tile_cnt[None, :, :], 0), axis=1)
        n_row_tiles = T // tm_moe + N_EXPERT_GROUPS
        row0 = jnp.arange(n_row_tiles) * tm_moe
        tile_grp = jnp.minimum(jnp.sum(((seg_start + seg)[None, :] <= row0[:, None]).astype(jnp.int32), axis=1),
                               N_EXPERT_GROUPS - 1)
        dest = pl.pallas_call(
            _route_kernel, name="moe_route", grid=(n_tok_tiles,),
            in_specs=[pl.BlockSpec((1, tm), lambda i: (0, i)),
                      pl.BlockSpec((1, N_EXPERT_GROUPS, LANES), lambda i: (i, 0, 0)), full((tm, tm))],
            out_specs=pl.BlockSpec((1, tm), lambda i: (0, i)),
            out_shape=jax.ShapeDtypeStruct((1, T), jnp.int32),
            compiler_params=_cparams("parallel"),
        )(bgrp, jnp.broadcast_to(tile_off.astype(F32)[:, :, None], (n_tok_tiles, N_EXPERT_GROUPS, LANES)), tri)
        dest3 = dest.reshape(n_tok_tiles, 1, tm)
        dest_spec = pl.BlockSpec((1, 1, tm), lambda i: (i, 0, 0), memory_space=pltpu.SMEM)
        any_spec = pl.BlockSpec(memory_space=pl.ANY)

        n_rows = n_row_tiles * tm_moe
        xs = pl.pallas_call(
            _dispatch_kernel, name="moe_dispatch", grid=(n_tok_tiles,),
            in_specs=[dest_spec, row_spec(DX), any_spec],
            out_specs=any_spec,
            out_shape=jax.ShapeDtypeStruct((n_rows, DX), F32),
            scratch_shapes=[pltpu.SemaphoreType.DMA(())],
            input_output_aliases={2: 0},
            compiler_params=_cparams("arbitrary"),
        )(dest3, hx, jnp.zeros((n_rows, DX), F32))

        per = E // N_EXPERT_GROUPS
        ys = pl.pallas_call(
            _moe_kernel, name="moe",
            grid_spec=pltpu.PrefetchScalarGridSpec(
                num_scalar_prefetch=1, grid=(n_row_tiles, per),
                in_specs=[pl.BlockSpec((tm_moe, DX), lambda k, u, grp: (k, 0)),
                          pl.BlockSpec((None, 1, D, DE), lambda k, u, grp: (l, grp[k] * per + u, 0, 0)),
                          pl.BlockSpec((None, 1, D, DE), lambda k, u, grp: (l, grp[k] * per + u, 0, 0)),
                          pl.BlockSpec((None, 1, DE, D), lambda k, u, grp: (l, grp[k] * per + u, 0, 0))],
                out_specs=pl.BlockSpec((tm_moe, D), lambda k, u, grp: (k, 0)),
                scratch_shapes=[pltpu.VMEM((tm_moe, D), F32)]),
            out_shape=jax.ShapeDtypeStruct((n_rows, D), F32),
            compiler_params=_cparams("parallel", "arbitrary"),
        )(tile_grp, xs, moe_w_gate, moe_w_up, moe_w_down)

        dest_next = pl.BlockSpec((1, 1, tm), lambda i: (jnp.minimum(i + 1, n_tok_tiles - 1), 0, 0),
                                 memory_space=pltpu.SMEM)
        x2 = pl.pallas_call(
            _combine_kernel, name="moe_combine", grid=(n_tok_tiles,),
            in_specs=[dest_spec, dest_next, any_spec, row_spec(D),
                      pl.BlockSpec((1, 6, D), lambda i: (i // tiles_per_batch, 0, 0))],
            out_specs=row_spec(D),
            out_shape=jax.ShapeDtypeStruct((T, D), F32),
            scratch_shapes=[pltpu.VMEM((2, tm, D), F32), pltpu.SemaphoreType.DMA((2,))],
            compiler_params=_cparams("arbitrary"),
        )(dest3, dest3, ys, x_new, mod_l)

    return x2.reshape(B, S, D)
```

```python
import functools

import numpy as np
import jax
import jax.numpy as jnp
from jax import lax
from jax.experimental import pallas as pl
from jax.experimental.pallas import tpu as pltpu

F32 = jnp.float32
BF16 = jnp.bfloat16

HEAD_DIM = 64
LANES = 128
RET_HEADS = 4
GMLP_GROUPS = 4
NSA_HEADS = 8
NSA_KV_HEADS = 2
NSA_GROUP = NSA_HEADS // NSA_KV_HEADS
N_BRANCH = 3
CHUNK = 128
CMP_LEN = 32
CMP_STRIDE = 16
SEL_BLOCK = 64
SEL_TOPK = 8
WINDOW = 512
ROPE_THETA = 10000.0
N_EXPERT_GROUPS = 4
EXPERT_TOPK = 2
NORM_EPS = 1e-6
NEG_INF = -1e30
FORCE_SCORE = 1e4

RET_W = RET_HEADS * HEAD_DIM
GMLP_W = GMLP_GROUPS * HEAD_DIM
NSA_W = NSA_HEADS * HEAD_DIM
KV_W = NSA_KV_HEADS * HEAD_DIM
GATE_W = NSA_HEADS * N_BRANCH
CMP_ROW = CMP_STRIDE * KV_W
OFF_RET = 0
OFF_GMLP = 4 * RET_W
OFF_Q = OFF_GMLP + 2 * GMLP_W
OFF_KV = OFF_Q + NSA_W
OFF_GL = OFF_KV + 6 * KV_W
N_IN = OFF_GL + GATE_W
N_IN_PAD = OFF_GL + LANES

Q_TILE = 256
KEY_CHUNK = 512
MIX_BATCH = 2
ROW_BLOCK = 32
ROW_UNROLL = True
VMEM_LIMIT = 48 * 1024 * 1024


def _cparams(*sem):
    return pltpu.CompilerParams(dimension_semantics=sem, vmem_limit_bytes=VMEM_LIMIT)


def _dot(a, b):
    return jnp.dot(a.astype(BF16), b.astype(BF16), preferred_element_type=F32)


def _dot_nt(a, b):
    return lax.dot_general(a.astype(BF16), b.astype(BF16), (((1,), (1,)), ((), ())),
                           preferred_element_type=F32)


def _split(a):
    hi = a.astype(BF16)
    lo = (a - hi.astype(F32)).astype(BF16)
    return hi, lo


def _dot_split_lhs(a, b):
    hi, lo = _split(a)
    return (jnp.dot(hi, b, preferred_element_type=F32) + jnp.dot(lo, b, preferred_element_type=F32))


def _dot_split_rhs(a, b):
    hi, lo = _split(b)
    return (jnp.dot(a, hi, preferred_element_type=F32) + jnp.dot(a, lo, preferred_element_type=F32))


def _head_ones(width=LANES):
    r = lax.broadcasted_iota(jnp.int32, (width, width), 0) // HEAD_DIM
    c = lax.broadcasted_iota(jnp.int32, (width, width), 1) // HEAD_DIM
    return jnp.where(r == c, 1.0, 0.0).astype(BF16)


def _rotate_half(y):
    lane = lax.broadcasted_iota(jnp.int32, y.shape, 1)
    first = (lane & (HEAD_DIM - 1)) < (HEAD_DIM // 2)
    return jnp.where(first, -pltpu.roll(y, LANES - HEAD_DIM // 2, 1), pltpu.roll(y, HEAD_DIM // 2, 1))


def _rope_slab(y, cos, sin):
    return y * cos + _rotate_half(y) * sin


def _rope(x, cos, sin):
    return jnp.concatenate(
        [_rope_slab(x[:, k * LANES:(k + 1) * LANES], cos, sin) for k in range(x.shape[1] // LANES)], axis=1)


def _head_rms_rope(x, gain, cos, sin, ones):
    outs = []
    for k in range(x.shape[1] // LANES):
        xs = x[:, k * LANES:(k + 1) * LANES]
        ssq = _dot_split_lhs(xs * xs, ones)
        y = xs * lax.rsqrt(ssq * (1.0 / HEAD_DIM) + NORM_EPS) * gain[:, k * LANES:(k + 1) * LANES]
        outs.append(_rope_slab(y, cos, sin))
    return outs[0] if len(outs) == 1 else jnp.concatenate(outs, axis=1)


def _rms_mod(x, gain, scale, shift):
    y = x * lax.rsqrt(jnp.mean(x * x, axis=-1, keepdims=True) + NORM_EPS)
    return (y * gain) * (1.0 + scale) + shift


def _softmax_rows(s):
    e = jnp.exp(s - jnp.max(s, axis=-1, keepdims=True))
    return e / jnp.sum(e, axis=-1, keepdims=True)


def _mask_bias(keep):
    return jnp.where(keep, 0.0, NEG_INF)


def _rope_table_kernel(pos_ref, invf_ref, cos_ref, sin_ref):
    ang = pos_ref[0].astype(F32) * invf_ref[...]
    cos_ref[0] = jnp.cos(ang)
    sin_ref[0] = jnp.sin(ang)


def _ada_kernel(c_ref, w_ref, b_ref, o_ref):
    c = c_ref[...]
    a = c * jax.nn.sigmoid(c)
    o_ref[0, 0] = jnp.dot(a, w_ref[0], preferred_element_type=F32,
                          precision=lax.Precision.HIGHEST) + b_ref[0, 0]


_IN_RAW = ((OFF_RET, OFF_GMLP), (OFF_GMLP, OFF_Q), (OFF_KV, OFF_KV + KV_W), (OFF_KV + KV_W, OFF_KV + 2 * KV_W),
           (OFF_GL, N_IN_PAD))


def _in_proj_kernel(x_ref, mod_ref, g_ref, w_ref, cos_ref, sin_ref, qg_ref, kg_ref, *out_refs):
    *raw_refs, q_out, kst_out, vs_out, kwt_out, vw_out = out_refs
    mod = mod_ref[0]
    h = _rms_mod(x_ref[...], g_ref[...], mod[1:2], mod[0:1]).astype(BF16)
    proj = lambda lo, hi: jnp.dot(h, w_ref[:, lo:hi], preferred_element_type=F32)
    for ref, (lo, hi) in zip(raw_refs, _IN_RAW):
        ref[...] = proj(lo, hi)

    ones = _head_ones()
    cos, sin = cos_ref[...], sin_ref[...]
    q_out[...] = (_head_rms_rope(proj(OFF_Q, OFF_KV), qg_ref[...], cos, sin, ones) * (HEAD_DIM ** -0.5)).astype(BF16)
    sel_kv = OFF_KV + 2 * KV_W
    kst_out[0] = _head_rms_rope(proj(sel_kv, sel_kv + KV_W), kg_ref[1:2], cos, sin, ones).T.astype(BF16)
    vs_out[...] = proj(sel_kv + KV_W, sel_kv + 2 * KV_W).astype(BF16)
    kwt_out[0] = _head_rms_rope(proj(sel_kv + 2 * KV_W, sel_kv + 3 * KV_W), kg_ref[2:3], cos, sin, ones).T.astype(BF16)
    vw_out[...] = proj(sel_kv + 3 * KV_W, OFF_GL).astype(BF16)


def _mixer_kernel(zr_ref, zg_ref, cos_ref, sin_ref, din_ref, dq_ref, dk_ref, dc_ref, rg_ref,
                  lng_ref, ws_ref, bs_ref, yret_ref, ygm_ref, state_ref):
    @pl.when(pl.program_id(1) == 0)
    def _():
        state_ref[...] = jnp.zeros_like(state_ref)

    row = lax.broadcasted_iota(jnp.int32, (CHUNK, CHUNK), 0)
    col = lax.broadcasted_iota(jnp.int32, (CHUNK, CHUNK), 1)
    ws = [jnp.where(row >= col, ws_ref[gi], 0.0).astype(BF16) for gi in range(GMLP_GROUPS)]

    for bb in range(zr_ref.shape[0]):
        cos, sin = cos_ref[bb], sin_ref[bb]
        q = _rope(zr_ref[bb, :, 0:RET_W], cos, sin)
        k = _rope(zr_ref[bb, :, RET_W:2 * RET_W], cos, sin) * (HEAD_DIM ** -0.5)
        v = zr_ref[bb, :, 2 * RET_W:3 * RET_W]
        g = zr_ref[bb, :, 3 * RET_W:4 * RET_W]
        outs = []
        for h in range(RET_HEADS):
            sl = slice(h * HEAD_DIM, (h + 1) * HEAD_DIM)
            qh, kh, vh = q[:, sl], k[:, sl], v[:, sl]
            st = state_ref[bb, h]
            inner = _dot_nt(qh, kh) * din_ref[h]
            o = _dot(inner, vh) + _dot(qh, st) * dq_ref[h]
            state_ref[bb, h] = st * dc_ref[h] + _dot((kh * dk_ref[h]).T, vh)
            o = o * lax.rsqrt(jnp.mean(o * o, axis=-1, keepdims=True) + NORM_EPS) * rg_ref[h]
            outs.append(o)
        yret_ref[bb] = (jnp.concatenate(outs, axis=1) * (g * jax.nn.sigmoid(g))).astype(BF16)

        u = jax.nn.gelu(zg_ref[bb, :, 0:GMLP_W])
        vv = jax.nn.gelu(zg_ref[bb, :, GMLP_W:2 * GMLP_W])
        mu = jnp.mean(vv, axis=-1, keepdims=True)
        var = jnp.mean(jnp.square(vv - mu), axis=-1, keepdims=True)
        vn = (vv - mu) * lax.rsqrt(var + NORM_EPS) * lng_ref[...]
        ss = [_dot(ws[gi], vn[:, gi * HEAD_DIM:(gi + 1) * HEAD_DIM]) + bs_ref[gi] for gi in range(GMLP_GROUPS)]
        ygm_ref[bb] = (u * jnp.concatenate(ss, axis=1)).astype(BF16)


def _compress_kernel(xk_ref, xv_ref, pek_ref, pev_ref, w1k_ref, w2k_ref, w1v_ref, w2v_ref, kg_ref,
                     cos_ref, sin_ref, kc_ref, vc_ref):
    n_rows = xk_ref.shape[0]

    def comp(x, pe_ref, w1_ref, w2_ref):
        a = _dot(x + pe_ref[0:1], w1_ref[0])
        b = _dot(x + pe_ref[1:2], w1_ref[1])
        h = a + pltpu.roll(b, n_rows - 1, 0)
        return _dot(jax.nn.gelu(h), w2_ref[...])

    kc_ref[0] = _head_rms_rope(comp(xk_ref[...], pek_ref, w1k_ref, w2k_ref), kg_ref[0:1],
                               cos_ref[0], sin_ref[0], _head_ones())
    vc_ref[0] = comp(xv_ref[...], pev_ref, w1v_ref, w2v_ref)


def _nsa_kernel(q_ref, gl_ref, kc_ref, vc_ref, kst_ref, vs_ref, kwt_ref, vw_ref, gexp_ref, expand_ref, o_ref,
                s_scr, p_scr, m_scr, acc_scr, *, n_sel, top_n):
    i = pl.program_id(1)
    qt = q_ref.shape[0]
    n_cmp = kc_ref.shape[1]
    t0 = i * qt
    n_chunks = (t0 + qt + KEY_CHUNK - 1) // KEY_CHUNK
    wk = WINDOW + qt
    wstart = pl.multiple_of(jnp.maximum(t0 - WINDOW, 0), LANES)

    t_r = t0 + lax.broadcasted_iota(jnp.int32, (qt, n_cmp), 0)
    n_c = lax.broadcasted_iota(jnp.int32, (qt, n_cmp), 1)
    keep_c = jnp.where(n_c * CMP_STRIDE + (CMP_LEN - 1) <= t_r, 1.0, 0.0)
    bias_c = (keep_c - 1.0) * (-NEG_INF)
    t_l = t0 + lax.broadcasted_iota(jnp.int32, (n_cmp, qt), 1)
    n_s = lax.broadcasted_iota(jnp.int32, (n_cmp, qt), 0)
    keep_ct = jnp.where(n_s * CMP_STRIDE + (CMP_LEN - 1) <= t_l, 1.0, 0.0)
    keep_ct4 = jnp.concatenate([keep_ct] * NSA_GROUP, axis=1)
    bias_ct4 = (keep_ct4 - 1.0) * (-NEG_INF)
    jo = lax.broadcasted_iota(jnp.int32, (n_sel, n_cmp), 0) * SEL_BLOCK
    no = lax.broadcasted_iota(jnp.int32, (n_sel, n_cmp), 1) * CMP_STRIDE
    overlap = jnp.where((no < jo + SEL_BLOCK) & (no + CMP_LEN > jo), 1.0, 0.0).astype(BF16)
    j = lax.broadcasted_iota(jnp.int32, (n_sel, qt), 0)
    tq = t0 + lax.broadcasted_iota(jnp.int32, (n_sel, qt), 1)
    cur = tq // SEL_BLOCK
    forced = (j == 0) | (j == cur) | (j == cur - 1)
    allowed = j * SEL_BLOCK <= tq
    t_w = t0 + lax.broadcasted_iota(jnp.int32, (qt, wk), 0)
    s_w = wstart + lax.broadcasted_iota(jnp.int32, (qt, wk), 1)
    bias_w = _mask_bias((s_w <= t_w) & (t_w - s_w < WINDOW))
    t_k = t0 + lax.broadcasted_iota(jnp.int32, (qt, KEY_CHUNK), 0)
    s_k = lax.broadcasted_iota(jnp.int32, (qt, KEY_CHUNK), 1)

    rep = lambda a: jnp.concatenate([a] * NSA_GROUP, axis=0)
    keep_c4, bias_c4 = rep(keep_c), rep(bias_c)
    unstack = lambda o, gs: [o[r * qt:(r + 1) * qt, gs] for r in range(NSA_GROUP)]
    rows = NSA_GROUP * qt
    lane_v = lax.broadcasted_iota(jnp.int32, (1, KV_W), 1)

    def attend(g, qs, kt, vv, bias, first):
        width = bias.shape[1]
        own = (lane_v >= g * HEAD_DIM) & (lane_v < (g + 1) * HEAD_DIM)
        vv1 = jnp.where(own, vv, jnp.ones_like(vv))
        s = jnp.dot(qs, kt, preferred_element_type=F32) + rep(bias)
        s_scr[g, :, 0:width] = s
        mx = jnp.max(s, axis=-1, keepdims=True)
        if first:
            m_scr[g] = jnp.broadcast_to(mx, (rows, LANES))
        else:
            m_old = m_scr[g]
            m_new = jnp.maximum(m_old, mx)
            alpha = jnp.exp(m_old - m_new)
            m_scr[g] = m_new

        def exp_pass(b, carry):
            r0 = pl.multiple_of(b * ROW_BLOCK, ROW_BLOCK)
            m_new = m_scr[g, pl.ds(r0, ROW_BLOCK), :]
            p = jnp.exp(s_scr[g, pl.ds(r0, ROW_BLOCK), 0:width]
                        - jnp.concatenate([m_new] * (width // LANES), axis=1))
            p_scr[g, pl.ds(r0, ROW_BLOCK), 0:width] = p.astype(BF16)
            return carry

        lax.fori_loop(0, rows // ROW_BLOCK, exp_pass, 0, unroll=ROW_UNROLL)
        pv = jnp.dot(p_scr[g, :, 0:width], vv1, preferred_element_type=F32)
        acc_scr[g] = pv if first else alpha * acc_scr[g] + pv

    def finish(g):
        acc = acc_scr[g]
        num, den = acc[:, g * HEAD_DIM:(g + 1) * HEAD_DIM], acc[:, (1 - g) * HEAD_DIM:(2 - g) * HEAD_DIM]
        return unstack(num / den, slice(None))

    o_cmp, o_win, qs_g, sel_g = [], [], [], []
    for g in range(NSA_KV_HEADS):
        gs = slice(g * HEAD_DIM, (g + 1) * HEAD_DIM)
        qs = jnp.concatenate([q_ref[:, (g * NSA_GROUP + r) * HEAD_DIM:(g * NSA_GROUP + r + 1) * HEAD_DIM]
                              for r in range(NSA_GROUP)], axis=0)
        qs_g.append(qs)
        kc = kc_ref[0][:, gs].astype(BF16)
        vc = vc_ref[0].astype(BF16)

        p = _softmax_rows(_dot_nt(qs, kc) + bias_c4) * keep_c4
        o_cmp += unstack(_dot(p, vc), gs)

        st = _dot_nt(kc, qs) + bias_ct4
        e = jnp.exp(st - jnp.max(st, axis=0, keepdims=True))
        pt = e / jnp.sum(e, axis=0, keepdims=True) * keep_ct4
        psum = pt[:, 0:qt]
        for r in range(1, NSA_GROUP):
            psum = psum + pt[:, r * qt:(r + 1) * qt]
        imp = _dot_split_rhs(overlap, psum)
        score = jnp.where(forced, FORCE_SCORE, jnp.where(allowed, imp, -1.0))
        rank = jnp.zeros((n_sel, qt), F32)
        for jp in range(n_sel):
            other = score[jp:jp + 1, :]
            tie = jnp.where(j > jp, 1.0, 0.0)
            rank = rank + jnp.where(other > score, 1.0, jnp.where(other == score, tie, 0.0))
        sel_t = jnp.where(rank < top_n, 1.0, 0.0)
        if n_sel < LANES:
            sel_t = jnp.concatenate([sel_t, jnp.zeros((LANES - n_sel, qt), F32)], axis=0)
        sel_g.append(sel_t.T.astype(BF16))

        attend(g, qs, kwt_ref[0, gs, pl.ds(wstart, wk)], vw_ref[pl.ds(wstart, wk), :], bias_w, True)
        o_win += finish(g)

    def chunk(c, first):
        k0 = pl.multiple_of(c * KEY_CHUNK, KEY_CHUNK)
        vv = vs_ref[pl.ds(k0, KEY_CHUNK), :]
        ex = expand_ref[:, pl.ds(k0, KEY_CHUNK)]
        causal = k0 + s_k <= t_k
        for g in range(NSA_KV_HEADS):
            chosen = jnp.dot(sel_g[g], ex, preferred_element_type=F32)
            attend(g, qs_g[g], kst_ref[0, g * HEAD_DIM:(g + 1) * HEAD_DIM, pl.ds(k0, KEY_CHUNK)], vv,
                   _mask_bias((chosen > 0.5) & causal), first)

    chunk(0, True)

    def body(c, carry):
        chunk(c, False)
        return carry

    lax.fori_loop(1, n_chunks, body, 0)
    o_sel = finish(0) + finish(1)

    gx = _dot_split_lhs(jax.nn.sigmoid(gl_ref[...]), gexp_ref[...])
    cat = lambda parts: jnp.concatenate(parts, axis=1)
    o_ref[...] = (gx[:, 0:NSA_W] * cat(o_cmp) + gx[:, NSA_W:2 * NSA_W] * cat(o_sel)
                  + gx[:, 2 * NSA_W:3 * NSA_W] * cat(o_win)).astype(BF16)


def _out_proj_kernel(yret_ref, ygm_ref, ynsa_ref, x_ref, mod_ref, wout_ref,
                     g2_ref, rwt_ref, rb_ref, xn_ref, hx_ref, bg_ref, cnt_ref, *, n_experts):
    mod = mod_ref[0]
    d_model = x_ref.shape[1]
    y = (jnp.dot(yret_ref[...], wout_ref[0:RET_W], preferred_element_type=F32)
         + jnp.dot(ygm_ref[...], wout_ref[RET_W:RET_W + GMLP_W], preferred_element_type=F32)
         + jnp.dot(ynsa_ref[...], wout_ref[RET_W + GMLP_W:], preferred_element_type=F32))
    xn = x_ref[...] + mod[2:3] * y
    xn_ref[...] = xn
    h2 = _rms_mod(xn, g2_ref[...], mod[4:5], mod[3:4])
    hx_ref[:, 0:d_model] = h2

    hh, hl = _split(h2)
    wh, wl = rwt_ref[0], rwt_ref[1]
    nt = lambda a, b: lax.dot_general(a, b, (((1,), (1,)), ((), ())), preferred_element_type=F32)
    logits = nt(wh, hh) + nt(wh, hl) + nt(wl, hh)
    ex = jnp.exp(logits - jnp.max(logits, axis=0, keepdims=True))
    probs = ex / jnp.sum(ex, axis=0, keepdims=True)
    sel = probs + rb_ref[...]
    per = n_experts // N_EXPERT_GROUPS
    srow = [sel[e:e + 1, :] for e in range(n_experts)]
    prow = [probs[e:e + 1, :] for e in range(n_experts)]
    gscore = []
    for gi in range(N_EXPERT_GROUPS):
        a = srow[gi * per:(gi + 1) * per]
        best = None
        for u in range(per):
            for w in range(u + 1, per):
                pair = a[u] + a[w]
                best = pair if best is None else jnp.maximum(best, pair)
        gscore.append(best)
    bg = jnp.zeros_like(gscore[0], dtype=jnp.int32)
    bs = gscore[0]
    for gi in range(1, N_EXPERT_GROUPS):
        upd = gscore[gi] > bs
        bg = jnp.where(upd, gi, bg)
        bs = jnp.where(upd, gscore[gi], bs)
    cs, cp = [], []
    for u in range(per):
        su, pu = srow[u], prow[u]
        for gi in range(1, N_EXPERT_GROUPS):
            su = jnp.where(bg == gi, srow[gi * per + u], su)
            pu = jnp.where(bg == gi, prow[gi * per + u], pu)
        cs.append(su)
        cp.append(pu)
    chosen = []
    for u in range(per):
        rank = jnp.zeros_like(cs[u])
        for w in range(per):
            if w == u:
                continue
            before = (cs[w] >= cs[u]) if w < u else (cs[w] > cs[u])
            rank = rank + jnp.where(before, 1.0, 0.0)
        chosen.append(jnp.where(rank < EXPERT_TOPK, 1.0, 0.0))
    denom = jnp.zeros_like(cp[0])
    for u in range(per):
        denom = denom + chosen[u] * cp[u]
    tm = bg.shape[1]
    grows = [chosen[u] * cp[u] / denom for u in range(per)]
    gpad = jnp.concatenate(grows + [jnp.zeros((LANES - per, tm), F32)], axis=0)
    hx_ref[:, d_model:d_model + LANES] = gpad.T
    bg_ref[...] = bg
    cnt_ref[0] = jnp.concatenate(
        [jnp.broadcast_to(jnp.sum(jnp.where(bg == gi, 1.0, 0.0), axis=1, keepdims=True), (1, LANES))
         for gi in range(N_EXPERT_GROUPS)], axis=0)


def _route_kernel(bg_ref, off_ref, tri_ref, dest_ref):
    bg = bg_ref[...]
    tm = bg.shape[1]
    member = [jnp.where(bg == gi, 1.0, 0.0) for gi in range(N_EXPERT_GROUPS)]
    pad = jnp.zeros((8 - N_EXPERT_GROUPS, tm), F32)
    before = jnp.dot(jnp.concatenate(member + [pad], axis=0).astype(BF16), tri_ref[...],
                     preferred_element_type=F32)
    off = off_ref[0]
    dest = jnp.zeros((1, tm), F32)
    for gi in range(N_EXPERT_GROUPS):
        start = jnp.concatenate([off[gi:gi + 1, :]] * (tm // LANES), axis=1)
        dest = dest + member[gi] * (start + before[gi:gi + 1, :])
    dest_ref[...] = dest.astype(jnp.int32)


def _dispatch_kernel(dest_ref, hx_ref, init_ref, xs_ref, stage, sems):
    del init_ref
    i = pl.program_id(0)
    n = hx_ref.shape[0]
    slot = i % 2

    def wait_slot(s):
        def wait(r, carry):
            pltpu.make_async_copy(stage.at[s, pl.ds(0, 1)], xs_ref.at[pl.ds(0, 1)], sems.at[s]).wait()
            return carry
        lax.fori_loop(0, n, wait, 0, unroll=8)

    @pl.when(i >= 2)
    def _():
        wait_slot(slot)

    stage[slot] = hx_ref[...]

    def start(r, carry):
        pltpu.make_async_copy(stage.at[slot, pl.ds(r, 1)], xs_ref.at[pl.ds(dest_ref[0, 0, r], 1)],
                              sems.at[slot]).start()
        return carry

    lax.fori_loop(0, n, start, 0, unroll=8)

    @pl.when(i == pl.num_programs(0) - 1)
    def _():
        @pl.when(i >= 1)
        def _():
            wait_slot(1 - slot)
        wait_slot(slot)


def _moe_kernel(grp_ref, used_ref, xs_ref, wg_ref, wu_ref, wd_ref, y_ref, acc_ref):
    del grp_ref
    u = pl.program_id(1)
    d_model = y_ref.shape[1]

    @pl.when(u == 0)
    def _():
        acc_ref[...] = jnp.zeros_like(acc_ref)

    @pl.when(pl.program_id(0) < used_ref[0])
    def _():
        h = xs_ref[:, 0:d_model].astype(BF16)
        gates = xs_ref[:, d_model:d_model + LANES]
        lane = lax.broadcasted_iota(jnp.int32, gates.shape, 1)
        gcol = jnp.sum(jnp.where(lane == u, gates, 0.0), axis=-1, keepdims=True)
        hg = jnp.dot(h, wg_ref[0].astype(BF16), preferred_element_type=F32)
        hu = jnp.dot(h, wu_ref[0].astype(BF16), preferred_element_type=F32)
        a = (hg * jax.nn.sigmoid(hg)) * hu
        acc_ref[...] += _dot(gcol * a, wd_ref[0])

    @pl.when(u == pl.num_programs(1) - 1)
    def _():
        y_ref[...] = acc_ref[...]


def _combine_kernel(cur_ref, nxt_ref, y_ref, xn_ref, mod_ref, o_ref, buf, sems):
    i = pl.program_id(0)
    n = xn_ref.shape[0]

    def request(idx_ref, slot):
        def start(r, carry):
            pltpu.make_async_copy(y_ref.at[pl.ds(idx_ref[0, 0, r], 1)], buf.at[slot, pl.ds(r, 1)],
                                  sems.at[slot]).start()
            return carry
        lax.fori_loop(0, n, start, 0, unroll=8)

    @pl.when(i == 0)
    def _():
        request(cur_ref, 0)

    @pl.when(i < pl.num_programs(0) - 1)
    def _():
        request(nxt_ref, (i + 1) % 2)

    slot = i % 2

    def wait(r, carry):
        pltpu.make_async_copy(y_ref.at[pl.ds(0, 1)], buf.at[slot, pl.ds(0, 1)], sems.at[slot]).wait()
        return carry

    lax.fori_loop(0, n, wait, 0, unroll=8)
    o_ref[...] = xn_ref[...] + mod_ref[0][5:6] * buf[slot]


def _decay_tables():
    H, C, d = RET_HEADS, CHUNK, HEAD_DIM
    log_gamma = jnp.log1p(-jnp.power(2.0, -5.0 - jnp.arange(H, dtype=F32)))
    idx = jnp.arange(C, dtype=F32)
    diff = idx[:, None] - idx[None, :]
    d_in = jnp.where(diff >= 0, jnp.exp(jnp.maximum(diff, 0.0)[None] * log_gamma[:, None, None]), 0.0).astype(F32)
    d_q = jnp.exp((idx + 1.0)[:, None] * log_gamma[None]).astype(F32)
    d_k = jnp.exp((C - 1.0 - idx)[:, None] * log_gamma[None]).astype(F32)
    d_c = jnp.exp(C * log_gamma).astype(F32)
    bcast = lambda a: jnp.broadcast_to(a.T[:, :, None], (H, C, d))
    return d_in, bcast(d_q), bcast(d_k), jnp.broadcast_to(d_c[:, None, None], (H, d, d))


def _gate_expand():
    m = np.zeros((LANES, N_BRANCH * NSA_W), np.float32)
    for h in range(NSA_HEADS):
        for br in range(N_BRANCH):
            m[h * N_BRANCH + br, br * NSA_W + h * HEAD_DIM: br * NSA_W + (h + 1) * HEAD_DIM] = 1.0
    return jnp.asarray(m, BF16)


def _block_expand(seq):
    m = (np.arange(LANES)[:, None] == (np.arange(seq) // SEL_BLOCK)[None, :]).astype(np.float32)
    return jnp.asarray(m, BF16)


def _compress_weights(pe, w1, w2):
    G, d = NSA_KV_HEADS, HEAD_DIM
    hid = w1.shape[-1]
    eye = jnp.eye(G, dtype=F32)
    w1h = w1.reshape(2, CMP_STRIDE, d, hid)
    w1b = jnp.einsum('pldh,ge->plgdeh', w1h, eye).reshape(2, CMP_ROW, G * hid)
    w2b = jnp.einsum('hd,ge->ghed', w2, eye).reshape(G * hid, G * d)
    peb = jnp.broadcast_to(pe.reshape(2, CMP_STRIDE, 1, d), (2, CMP_STRIDE, G, d)).reshape(2, CMP_ROW)
    return peb, w1b.astype(BF16), w2b.astype(BF16)


def kernel(x, c, positions, ada_w, ada_b, norm_mix_g, norm_ffn_g, w_in, w_out, ret_norm_g, gmlp_ln_g, gmlp_ws,
           gmlp_b, nsa_q_norm_g, nsa_k_norm_g, cmp_pe_k, cmp_pe_v, cmp_w1_k, cmp_w2_k, cmp_w1_v, cmp_w2_v,
           router_w, router_b, moe_w_gate, moe_w_up, moe_w_down):
    B, S, D = x.shape
    L = ada_w.shape[0]
    E = router_w.shape[1]
    DE = moe_w_gate.shape[-1]
    T = B * S
    n_chunk = S // CHUNK
    n_cmp = S // CMP_STRIDE
    n_sel = S // SEL_BLOCK
    top_n = min(SEL_TOPK, n_sel)
    n_qt = S // Q_TILE
    tm = min(512, S)
    tm_moe = min(1024, S)
    assert S % KEY_CHUNK == 0 and S >= WINDOW + Q_TILE and D == RET_W + GMLP_W + NSA_W
    assert w_in.shape[-1] == N_IN and T % tm_moe == 0 and n_sel <= LANES and NSA_KV_HEADS == 2

    half = HEAD_DIM // 2
    inv_freq = jnp.power(ROPE_THETA, -jnp.arange(half, dtype=F32) / half)
    invf = jnp.tile(inv_freq, LANES // half)[None, :]
    cos_t, sin_t = pl.pallas_call(
        _rope_table_kernel, name="rope_tables", grid=(B,),
        in_specs=[pl.BlockSpec((1, S, 1), lambda b: (b, 0, 0)), pl.BlockSpec((1, LANES), lambda b: (0, 0))],
        out_specs=[pl.BlockSpec((1, S, LANES), lambda b: (b, 0, 0))] * 2,
        out_shape=[jax.ShapeDtypeStruct((B, S, LANES), F32)] * 2,
        compiler_params=_cparams("parallel"),
    )(positions[:, :, None], invf)
    pad_c = lambda a: jnp.pad(a[:, CMP_LEN - 1::CMP_STRIDE], ((0, 0), (0, 1), (0, 0)))
    cos_c, sin_c = pad_c(cos_t), pad_c(sin_t)
    cos_f, sin_f = cos_t.reshape(T, LANES), sin_t.reshape(T, LANES)

    mod = pl.pallas_call(
        _ada_kernel, name="ada_mod", grid=(L, 6),
        in_specs=[pl.BlockSpec((B, D), lambda l, j: (0, 0)),
                  pl.BlockSpec((1, D, D), lambda l, j: (l, 0, j)),
                  pl.BlockSpec((1, 1, 1, D), lambda l, j: (l, j, 0, 0))],
        out_specs=pl.BlockSpec((1, 1, B, D), lambda l, j: (l, j, 0, 0)),
        out_shape=jax.ShapeDtypeStruct((L, 6, B, D), F32),
        compiler_params=_cparams("parallel", "parallel"),
    )(c, ada_w, ada_b.reshape(L, 6, 1, D))
    mod = mod.transpose(0, 2, 1, 3)

    d_in, d_q, d_k, d_c = _decay_tables()
    gexp = _gate_expand()
    expand = _block_expand(S)
    rw_hi = router_w.T.astype(BF16)
    rw_lo = (router_w.T - rw_hi.astype(F32)).astype(BF16)
    rwt = jnp.stack([rw_hi, rw_lo])
    rb = router_b.reshape(E, 1)
    tri = jnp.asarray(np.triu(np.ones((tm, tm), np.float32), k=1), BF16)
    x2 = x.reshape(T, D)
    tiles_per_batch = S // tm
    row_spec = lambda w: pl.BlockSpec((tm, w), lambda i: (i, 0))
    full = lambda shape: pl.BlockSpec(shape, lambda *_: (0,) * len(shape))

    for l in range(L):
        mod_l = mod[l]
        w_in_p = jnp.pad(w_in[l], ((0, 0), (0, N_IN_PAD - N_IN))).astype(BF16)
        raw_widths = tuple(hi - lo for lo, hi in _IN_RAW)
        mod_spec = pl.BlockSpec((1, 6, D), lambda i: (i // tiles_per_batch, 0, 0))
        kt_out = pl.BlockSpec((1, KV_W, tm), lambda i: (i // tiles_per_batch, 0, i % tiles_per_batch))
        kgain = jnp.tile(nsa_k_norm_g[l], (1, NSA_KV_HEADS))
        zr, zg, zkc, zvc, zgl, q_r, ks_t, vs_b, kw_t, vw_b = pl.pallas_call(
            _in_proj_kernel, name="in_proj", grid=(T // tm,),
            in_specs=[row_spec(D), mod_spec, full((1, D)), full((D, N_IN_PAD)), row_spec(LANES), row_spec(LANES),
                      full((1, NSA_W)), full((N_BRANCH, KV_W))],
            out_specs=[row_spec(w) for w in raw_widths]
            + [row_spec(NSA_W), kt_out, row_spec(KV_W), kt_out, row_spec(KV_W)],
            out_shape=[jax.ShapeDtypeStruct((T, w), F32) for w in raw_widths]
            + [jax.ShapeDtypeStruct((T, NSA_W), BF16), jax.ShapeDtypeStruct((B, KV_W, S), BF16),
               jax.ShapeDtypeStruct((T, KV_W), BF16), jax.ShapeDtypeStruct((B, KV_W, S), BF16),
               jax.ShapeDtypeStruct((T, KV_W), BF16)],
            compiler_params=_cparams("parallel"),
        )(x2, mod_l, norm_mix_g[l][None, :], w_in_p, cos_f, sin_f,
          jnp.tile(nsa_q_norm_g[l], NSA_HEADS)[None, :], kgain)

        crow = lambda w: pl.BlockSpec((MIX_BATCH, CHUNK, w), lambda b, cc: (b, cc, 0))
        y_ret, y_gm = pl.pallas_call(
            _mixer_kernel, name="ret_gmlp", grid=(B // MIX_BATCH, n_chunk),
            in_specs=[crow(4 * RET_W), crow(2 * GMLP_W), crow(LANES), crow(LANES),
                      full((RET_HEADS, CHUNK, CHUNK)), full((RET_HEADS, CHUNK, HEAD_DIM)),
                      full((RET_HEADS, CHUNK, HEAD_DIM)), full((RET_HEADS, HEAD_DIM, HEAD_DIM)),
                      full((RET_HEADS, 1, HEAD_DIM)), full((1, GMLP_W)),
                      full((GMLP_GROUPS, CHUNK, CHUNK)), full((GMLP_GROUPS, CHUNK, HEAD_DIM))],
            out_specs=[crow(RET_W), crow(GMLP_W)],
            out_shape=[jax.ShapeDtypeStruct((B, S, RET_W), BF16), jax.ShapeDtypeStruct((B, S, GMLP_W), BF16)],
            scratch_shapes=[pltpu.VMEM((MIX_BATCH, RET_HEADS, HEAD_DIM, HEAD_DIM), F32)],
            compiler_params=_cparams("parallel", "arbitrary"),
        )(zr.reshape(B, S, 4 * RET_W), zg.reshape(B, S, 2 * GMLP_W), cos_t, sin_t, d_in, d_q, d_k, d_c,
          ret_norm_g[l][:, None, :], gmlp_ln_g[l][None, :], gmlp_ws[l],
          jnp.broadcast_to(gmlp_b[l][:, :, None], (GMLP_GROUPS, CHUNK, HEAD_DIM)))
        y_ret, y_gm = y_ret.reshape(T, RET_W), y_gm.reshape(T, GMLP_W)

        pek, w1k, w2k = _compress_weights(cmp_pe_k[l], cmp_w1_k[l], cmp_w2_k[l])
        pev, w1v, w2v = _compress_weights(cmp_pe_v[l], cmp_w1_v[l], cmp_w2_v[l])
        cblk = pl.BlockSpec((n_cmp, CMP_ROW), lambda b: (b, 0))
        ghid = w1k.shape[-1]
        ctab = pl.BlockSpec((1, n_cmp, LANES), lambda b: (b, 0, 0))
        kc, vc = pl.pallas_call(
            _compress_kernel, name="nsa_compress", grid=(B,),
            in_specs=[cblk, cblk, full((2, CMP_ROW)), full((2, CMP_ROW)),
                      full((2, CMP_ROW, ghid)), full((ghid, KV_W)), full((2, CMP_ROW, ghid)), full((ghid, KV_W)),
                      full((N_BRANCH, KV_W)), ctab, ctab],
            out_specs=[ctab, ctab],
            out_shape=[jax.ShapeDtypeStruct((B, n_cmp, KV_W), F32)] * 2,
            compiler_params=_cparams("parallel"),
        )(zkc.reshape(T // CMP_STRIDE, CMP_ROW), zvc.reshape(T // CMP_STRIDE, CMP_ROW), pek, pev, w1k, w2k, w1v, w2v,
          kgain, cos_c, sin_c)

        qrow = lambda w: pl.BlockSpec((Q_TILE, w), lambda b, i: (b * n_qt + i, 0))
        kt_in = pl.BlockSpec((1, KV_W, S), lambda b, i: (b, 0, 0))
        v_in = pl.BlockSpec((S, KV_W), lambda b, i: (b, 0))
        ctab2 = pl.BlockSpec((1, n_cmp, KV_W), lambda b, i: (b, 0, 0))
        y_nsa = pl.pallas_call(
            functools.partial(_nsa_kernel, n_sel=n_sel, top_n=top_n), name="nsa_attn", grid=(B, n_qt),
            in_specs=[qrow(NSA_W), qrow(LANES), ctab2, ctab2, kt_in, v_in, kt_in, v_in,
                      full((LANES, N_BRANCH * NSA_W)), full((LANES, S))],
            out_specs=qrow(NSA_W),
            out_shape=jax.ShapeDtypeStruct((T, NSA_W), BF16),
            scratch_shapes=[
                pltpu.VMEM((NSA_KV_HEADS, NSA_GROUP * Q_TILE, WINDOW + Q_TILE), F32),
                pltpu.VMEM((NSA_KV_HEADS, NSA_GROUP * Q_TILE, WINDOW + Q_TILE), BF16),
                pltpu.VMEM((NSA_KV_HEADS, NSA_GROUP * Q_TILE, LANES), F32),
                pltpu.VMEM((NSA_KV_HEADS, NSA_GROUP * Q_TILE, KV_W), F32)],
            compiler_params=_cparams("parallel", "parallel"),
        )(q_r, zgl, kc, vc, ks_t, vs_b, kw_t, vw_b, gexp, expand)

        n_tok_tiles = T // tm
        DX = D + LANES
        x_new, hx, bgrp, cnt = pl.pallas_call(
            functools.partial(_out_proj_kernel, n_experts=E), name="out_proj_router", grid=(n_tok_tiles,),
            in_specs=[row_spec(RET_W), row_spec(GMLP_W), row_spec(NSA_W), row_spec(D),
                      pl.BlockSpec((1, 6, D), lambda i: (i // tiles_per_batch, 0, 0)),
                      full((D, D)), full((1, D)), full((2, E, D)), full((E, 1))],
            out_specs=[row_spec(D), row_spec(DX), pl.BlockSpec((1, tm), lambda i: (0, i)),
                       pl.BlockSpec((1, N_EXPERT_GROUPS, LANES), lambda i: (i, 0, 0))],
            out_shape=[jax.ShapeDtypeStruct((T, D), F32), jax.ShapeDtypeStruct((T, DX), F32),
                       jax.ShapeDtypeStruct((1, T), jnp.int32),
                       jax.ShapeDtypeStruct((n_tok_tiles, N_EXPERT_GROUPS, LANES), F32)],
            compiler_params=_cparams("parallel"),
        )(y_ret, y_gm, y_nsa, x2, mod_l, w_out[l].astype(BF16), norm_ffn_g[l][None, :], rwt, rb)

        tile_cnt = cnt[:, :, 0].astype(jnp.int32)
        seg = ((jnp.sum(tile_cnt, axis=0) + tm_moe - 1) // tm_moe) * tm_moe
        earlier = lambda n: jnp.arange(n)[:, None] > jnp.arange(n)[None, :]
        seg_start = jnp.sum(jnp.where(earlier(N_EXPERT_GROUPS), seg[None, :], 0), axis=1)
        tile_off = seg_start[None, :] + jnp.sum(
            jnp.where(earlier(n_tok_tiles)[:, :, None], tile_cnt[None, :, :], 0), axis=1)
        n_row_tiles = T // tm_moe + N_EXPERT_GROUPS
        row0 = jnp.arange(n_row_tiles) * tm_moe
        tile_grp = jnp.minimum(jnp.sum(((seg_start + seg)[None, :] <= row0[:, None]).astype(jnp.int32), axis=1),
                               N_EXPERT_GROUPS - 1)
        dest = pl.pallas_call(
            _route_kernel, name="moe_route", grid=(n_tok_tiles,),
            in_specs=[pl.BlockSpec((1, tm), lambda i: (0, i)),
                      pl.BlockSpec((1, N_EXPERT_GROUPS, LANES), lambda i: (i, 0, 0)), full((tm, tm))],
            out_specs=pl.BlockSpec((1, tm), lambda i: (0, i)),
            out_shape=jax.ShapeDtypeStruct((1, T), jnp.int32),
            compiler_params=_cparams("parallel"),
        )(bgrp, jnp.broadcast_to(tile_off.astype(F32)[:, :, None], (n_tok_tiles, N_EXPERT_GROUPS, LANES)), tri)
        dest3 = dest.reshape(n_tok_tiles, 1, tm)
        dest_spec = pl.BlockSpec((1, 1, tm), lambda i: (i, 0, 0), memory_space=pltpu.SMEM)
        any_spec = pl.BlockSpec(memory_space=pl.ANY)

        n_rows = n_row_tiles * tm_moe
        xs = pl.pallas_call(
            _dispatch_kernel, name="moe_dispatch", grid=(n_tok_tiles,),
            in_specs=[dest_spec, row_spec(DX), any_spec],
            out_specs=any_spec,
            out_shape=jax.ShapeDtypeStruct((n_rows, DX), F32),
            scratch_shapes=[pltpu.VMEM((2, tm, DX), F32), pltpu.SemaphoreType.DMA((2,))],
            input_output_aliases={2: 0},
            compiler_params=_cparams("arbitrary"),
        )(dest3, hx, jnp.zeros((n_rows, DX), F32) if l == 0 else xs)

        per = E // N_EXPERT_GROUPS
        tiles_used = (jnp.sum(seg) // tm_moe).astype(jnp.int32).reshape(1)

        def w_idx(k, u, grp, used):
            return l, grp[k] * per + jnp.where(k < used[0], u, per - 1), 0, 0

        ys = pl.pallas_call(
            _moe_kernel, name="moe",
            grid_spec=pltpu.PrefetchScalarGridSpec(
                num_scalar_prefetch=2, grid=(n_row_tiles, per),
                in_specs=[pl.BlockSpec((tm_moe, DX), lambda k, u, grp, used: (k, 0)),
                          pl.BlockSpec((None, 1, D, DE), w_idx), pl.BlockSpec((None, 1, D, DE), w_idx),
                          pl.BlockSpec((None, 1, DE, D), w_idx)],
                out_specs=pl.BlockSpec((tm_moe, D), lambda k, u, grp, used: (k, 0)),
                scratch_shapes=[pltpu.VMEM((tm_moe, D), F32)]),
            out_shape=jax.ShapeDtypeStruct((n_rows, D), F32),
            compiler_params=_cparams("parallel", "arbitrary"),
        )(tile_grp, tiles_used, xs, moe_w_gate, moe_w_up, moe_w_down)

        dest_next = pl.BlockSpec((1, 1, tm), lambda i: (jnp.minimum(i + 1, n_tok_tiles - 1), 0, 0),
                                 memory_space=pltpu.SMEM)
        x2 = pl.pallas_call(
            _combine_kernel, name="moe_combine", grid=(n_tok_tiles,),
            in_specs=[dest_spec, dest_next, any_spec, row_spec(D),
                      pl.BlockSpec((1, 6, D), lambda i: (i // tiles_per_batch, 0, 0))],
            out_specs=row_spec(D),
            out_shape=jax.ShapeDtypeStruct((T, D), F32),
            scratch_shapes=[pltpu.VMEM((2, tm, D), F32), pltpu.SemaphoreType.DMA((2,))],
            compiler_params=_cparams("arbitrary"),
        )(dest3, dest3, ys, x_new, mod_l)

    return x2.reshape(B, S, D)
```

```python
import functools

import numpy as np
import jax
import jax.numpy as jnp
from jax import lax
from jax.experimental import pallas as pl
from jax.experimental.pallas import tpu as pltpu

F32 = jnp.float32
BF16 = jnp.bfloat16

HEAD_DIM = 64
LANES = 128
RET_HEADS = 4
GMLP_GROUPS = 4
NSA_HEADS = 8
NSA_KV_HEADS = 2
NSA_GROUP = NSA_HEADS // NSA_KV_HEADS
N_BRANCH = 3
CHUNK = 128
CMP_LEN = 32
CMP_STRIDE = 16
SEL_BLOCK = 64
SEL_TOPK = 8
WINDOW = 512
ROPE_THETA = 10000.0
N_EXPERT_GROUPS = 4
EXPERT_TOPK = 2
NORM_EPS = 1e-6
NEG_INF = -1e30
FORCE_SCORE = 1e4

RET_W = RET_HEADS * HEAD_DIM
GMLP_W = GMLP_GROUPS * HEAD_DIM
NSA_W = NSA_HEADS * HEAD_DIM
KV_W = NSA_KV_HEADS * HEAD_DIM
GATE_W = NSA_HEADS * N_BRANCH
OFF_RET = 0
OFF_GMLP = 4 * RET_W
OFF_Q = OFF_GMLP + 2 * GMLP_W
OFF_KV = OFF_Q + NSA_W
OFF_GL = OFF_KV + 6 * KV_W
N_IN = OFF_GL + GATE_W
N_IN_PAD = OFF_GL + LANES

Q_TILE = 256
KEY_CHUNK = 512
MIX_BATCH = 2
ROW_BLOCK = 32
ROW_UNROLL = True
VMEM_LIMIT = 48 * 1024 * 1024


def _cparams(*sem):
    return pltpu.CompilerParams(dimension_semantics=sem, vmem_limit_bytes=VMEM_LIMIT)


def _dot(a, b):
    return jnp.dot(a.astype(BF16), b.astype(BF16), preferred_element_type=F32)


def _dot_nt(a, b):
    return lax.dot_general(a.astype(BF16), b.astype(BF16), (((1,), (1,)), ((), ())),
                           preferred_element_type=F32)


def _split(a):
    hi = a.astype(BF16)
    lo = (a - hi.astype(F32)).astype(BF16)
    return hi, lo


def _dot_split_lhs(a, b):
    hi, lo = _split(a)
    return (jnp.dot(hi, b, preferred_element_type=F32) + jnp.dot(lo, b, preferred_element_type=F32))


def _dot_split_rhs(a, b):
    hi, lo = _split(b)
    return (jnp.dot(a, hi, preferred_element_type=F32) + jnp.dot(a, lo, preferred_element_type=F32))


def _head_ones(width=LANES):
    r = lax.broadcasted_iota(jnp.int32, (width, width), 0) // HEAD_DIM
    c = lax.broadcasted_iota(jnp.int32, (width, width), 1) // HEAD_DIM
    return jnp.where(r == c, 1.0, 0.0).astype(BF16)


def _rotate_half(y):
    lane = lax.broadcasted_iota(jnp.int32, y.shape, 1)
    first = (lane & (HEAD_DIM - 1)) < (HEAD_DIM // 2)
    return jnp.where(first, -pltpu.roll(y, LANES - HEAD_DIM // 2, 1), pltpu.roll(y, HEAD_DIM // 2, 1))


def _rope_slab(y, cos, sin):
    return y * cos + _rotate_half(y) * sin


def _rope(x, cos, sin):
    return jnp.concatenate(
        [_rope_slab(x[:, k * LANES:(k + 1) * LANES], cos, sin) for k in range(x.shape[1] // LANES)], axis=1)


def _head_rms_rope(x, gain, cos, sin, ones):
    outs = []
    for k in range(x.shape[1] // LANES):
        xs = x[:, k * LANES:(k + 1) * LANES]
        ssq = _dot_split_lhs(xs * xs, ones)
        y = xs * lax.rsqrt(ssq * (1.0 / HEAD_DIM) + NORM_EPS) * gain[:, k * LANES:(k + 1) * LANES]
        outs.append(_rope_slab(y, cos, sin))
    return outs[0] if len(outs) == 1 else jnp.concatenate(outs, axis=1)


def _rms_mod(x, gain, scale, shift):
    y = x * lax.rsqrt(jnp.mean(x * x, axis=-1, keepdims=True) + NORM_EPS)
    return (y * gain) * (1.0 + scale) + shift


def _softmax_rows(s):
    e = jnp.exp(s - jnp.max(s, axis=-1, keepdims=True))
    return e / jnp.sum(e, axis=-1, keepdims=True)


def _mask_bias(keep):
    return jnp.where(keep, 0.0, NEG_INF)


def _rope_table_kernel(pos_ref, invf_ref, cos_ref, sin_ref):
    ang = pos_ref[0].astype(F32) * invf_ref[...]
    cos_ref[0] = jnp.cos(ang)
    sin_ref[0] = jnp.sin(ang)


def _ada_kernel(c_ref, w_ref, b_ref, o_ref):
    c = c_ref[...]
    a = c * jax.nn.sigmoid(c)
    o_ref[0, 0] = jnp.dot(a, w_ref[0], preferred_element_type=F32,
                          precision=lax.Precision.HIGHEST) + b_ref[0, 0]


_IN_RAW = ((OFF_RET, OFF_GMLP), (OFF_GMLP, OFF_Q), (OFF_KV, OFF_KV + KV_W), (OFF_KV + KV_W, OFF_KV + 2 * KV_W),
           (OFF_GL, N_IN_PAD))


def _in_proj_kernel(x_ref, mod_ref, g_ref, w_ref, cos_ref, sin_ref, qg_ref, kg_ref, *out_refs):
    *raw_refs, q_out, kst_out, vs_out, kwt_out, vw_out = out_refs
    mod = mod_ref[0]
    h = _rms_mod(x_ref[...], g_ref[...], mod[1:2], mod[0:1]).astype(BF16)
    proj = lambda lo, hi: jnp.dot(h, w_ref[:, lo:hi], preferred_element_type=F32)
    for ref, (lo, hi) in zip(raw_refs, _IN_RAW):
        ref[...] = proj(lo, hi)

    ones = _head_ones()
    cos, sin = cos_ref[...], sin_ref[...]
    q_out[...] = (_head_rms_rope(proj(OFF_Q, OFF_KV), qg_ref[...], cos, sin, ones) * (HEAD_DIM ** -0.5)).astype(BF16)
    sel_kv = OFF_KV + 2 * KV_W
    kst_out[0] = _head_rms_rope(proj(sel_kv, sel_kv + KV_W), kg_ref[1:2], cos, sin, ones).T.astype(BF16)
    vs_out[...] = proj(sel_kv + KV_W, sel_kv + 2 * KV_W).astype(BF16)
    kwt_out[0] = _head_rms_rope(proj(sel_kv + 2 * KV_W, sel_kv + 3 * KV_W), kg_ref[2:3], cos, sin, ones).T.astype(BF16)
    vw_out[...] = proj(sel_kv + 3 * KV_W, OFF_GL).astype(BF16)


def _mixer_kernel(zr_ref, zg_ref, cos_ref, sin_ref, din_ref, dq_ref, dk_ref, dc_ref, rg_ref,
                  lng_ref, ws_ref, bs_ref, yret_ref, ygm_ref, state_ref):
    @pl.when(pl.program_id(1) == 0)
    def _():
        state_ref[...] = jnp.zeros_like(state_ref)

    row = lax.broadcasted_iota(jnp.int32, (CHUNK, CHUNK), 0)
    col = lax.broadcasted_iota(jnp.int32, (CHUNK, CHUNK), 1)
    ws = [jnp.where(row >= col, ws_ref[gi], 0.0).astype(BF16) for gi in range(GMLP_GROUPS)]

    for bb in range(zr_ref.shape[0]):
        cos, sin = cos_ref[bb], sin_ref[bb]
        q = _rope(zr_ref[bb, :, 0:RET_W], cos, sin)
        k = _rope(zr_ref[bb, :, RET_W:2 * RET_W], cos, sin) * (HEAD_DIM ** -0.5)
        v = zr_ref[bb, :, 2 * RET_W:3 * RET_W]
        g = zr_ref[bb, :, 3 * RET_W:4 * RET_W]
        outs = []
        for h in range(RET_HEADS):
            sl = slice(h * HEAD_DIM, (h + 1) * HEAD_DIM)
            qh, kh, vh = q[:, sl], k[:, sl], v[:, sl]
            st = state_ref[bb, h]
            inner = _dot_nt(qh, kh) * din_ref[h]
            o = _dot(inner, vh) + _dot(qh, st) * dq_ref[h]
            state_ref[bb, h] = st * dc_ref[h] + _dot((kh * dk_ref[h]).T, vh)
            o = o * lax.rsqrt(jnp.mean(o * o, axis=-1, keepdims=True) + NORM_EPS) * rg_ref[h]
            outs.append(o)
        yret_ref[bb] = (jnp.concatenate(outs, axis=1) * (g * jax.nn.sigmoid(g))).astype(BF16)

        u = jax.nn.gelu(zg_ref[bb, :, 0:GMLP_W])
        vv = jax.nn.gelu(zg_ref[bb, :, GMLP_W:2 * GMLP_W])
        mu = jnp.mean(vv, axis=-1, keepdims=True)
        var = jnp.mean(jnp.square(vv - mu), axis=-1, keepdims=True)
        vn = (vv - mu) * lax.rsqrt(var + NORM_EPS) * lng_ref[...]
        ss = [_dot(ws[gi], vn[:, gi * HEAD_DIM:(gi + 1) * HEAD_DIM]) + bs_ref[gi] for gi in range(GMLP_GROUPS)]
        ygm_ref[bb] = (u * jnp.concatenate(ss, axis=1)).astype(BF16)


def _compress_kernel(xk_ref, xv_ref, pek_ref, pev_ref, w1k_ref, w2k_ref, w1v_ref, w2v_ref, kg_ref,
                     cos_ref, sin_ref, kc_ref, vc_ref):
    n_rows = kc_ref.shape[1]

    def comp(x_ref, pe_ref, w1_ref, w2_ref):
        a = b = None
        for tok in range(CMP_STRIDE):
            x = x_ref[pl.ds(tok, n_rows, stride=CMP_STRIDE), :]
            ta = _dot(x + pe_ref[0, tok:tok + 1, :], w1_ref[0, tok])
            tb = _dot(x + pe_ref[1, tok:tok + 1, :], w1_ref[1, tok])
            a, b = (ta, tb) if a is None else (a + ta, b + tb)
        h = a + pltpu.roll(b, n_rows - 1, 0)
        return _dot(jax.nn.gelu(h), w2_ref[...])

    kc_ref[0] = _head_rms_rope(comp(xk_ref, pek_ref, w1k_ref, w2k_ref), kg_ref[0:1],
                               cos_ref[0], sin_ref[0], _head_ones())
    vc_ref[0] = comp(xv_ref, pev_ref, w1v_ref, w2v_ref)


def _nsa_kernel(q_ref, gl_ref, kc_ref, vc_ref, kst_ref, vs_ref, kwt_ref, vw_ref, gexp_ref, expand_ref, o_ref,
                s_scr, p_scr, m_scr, acc_scr, *, n_sel, top_n):
    i = pl.program_id(1)
    qt = q_ref.shape[0]
    n_cmp = kc_ref.shape[1]
    t0 = i * qt
    n_chunks = (t0 + qt + KEY_CHUNK - 1) // KEY_CHUNK
    wk = WINDOW + qt
    wstart = pl.multiple_of(jnp.maximum(t0 - WINDOW, 0), LANES)

    t_r = t0 + lax.broadcasted_iota(jnp.int32, (qt, n_cmp), 0)
    n_c = lax.broadcasted_iota(jnp.int32, (qt, n_cmp), 1)
    keep_c = jnp.where(n_c * CMP_STRIDE + (CMP_LEN - 1) <= t_r, 1.0, 0.0)
    bias_c = (keep_c - 1.0) * (-NEG_INF)
    t_l = t0 + lax.broadcasted_iota(jnp.int32, (n_cmp, qt), 1)
    n_s = lax.broadcasted_iota(jnp.int32, (n_cmp, qt), 0)
    keep_ct = jnp.where(n_s * CMP_STRIDE + (CMP_LEN - 1) <= t_l, 1.0, 0.0)
    keep_ct4 = jnp.concatenate([keep_ct] * NSA_GROUP, axis=1)
    bias_ct4 = (keep_ct4 - 1.0) * (-NEG_INF)
    jo = lax.broadcasted_iota(jnp.int32, (n_sel, n_cmp), 0) * SEL_BLOCK
    no = lax.broadcasted_iota(jnp.int32, (n_sel, n_cmp), 1) * CMP_STRIDE
    overlap = jnp.where((no < jo + SEL_BLOCK) & (no + CMP_LEN > jo), 1.0, 0.0).astype(BF16)
    j = lax.broadcasted_iota(jnp.int32, (n_sel, qt), 0)
    tq = t0 + lax.broadcasted_iota(jnp.int32, (n_sel, qt), 1)
    cur = tq // SEL_BLOCK
    forced = (j == 0) | (j == cur) | (j == cur - 1)
    allowed = j * SEL_BLOCK <= tq
    t_w = t0 + lax.broadcasted_iota(jnp.int32, (qt, wk), 0)
    s_w = wstart + lax.broadcasted_iota(jnp.int32, (qt, wk), 1)
    bias_w = _mask_bias((s_w <= t_w) & (t_w - s_w < WINDOW))
    t_k = t0 + lax.broadcasted_iota(jnp.int32, (qt, KEY_CHUNK), 0)
    s_k = lax.broadcasted_iota(jnp.int32, (qt, KEY_CHUNK), 1)

    rep = lambda a: jnp.concatenate([a] * NSA_GROUP, axis=0)
    keep_c4, bias_c4 = rep(keep_c), rep(bias_c)
    unstack = lambda o, gs: [o[r * qt:(r + 1) * qt, gs] for r in range(NSA_GROUP)]
    rows = NSA_GROUP * qt
    lane_v = lax.broadcasted_iota(jnp.int32, (1, KV_W), 1)

    def attend(g, qs, kt, vv, bias, first):
        width = bias.shape[1]
        own = (lane_v >= g * HEAD_DIM) & (lane_v < (g + 1) * HEAD_DIM)
        vv1 = jnp.where(own, vv, jnp.ones_like(vv))
        s = jnp.dot(qs, kt, preferred_element_type=F32) + rep(bias)
        s_scr[g, :, 0:width] = s
        mx = jnp.max(s, axis=-1, keepdims=True)
        if first:
            m_scr[g] = jnp.broadcast_to(mx, (rows, LANES))
        else:
            m_old = m_scr[g]
            m_new = jnp.maximum(m_old, mx)
            alpha = jnp.exp(m_old - m_new)
            m_scr[g] = m_new

        def exp_pass(b, carry):
            r0 = pl.multiple_of(b * ROW_BLOCK, ROW_BLOCK)
            m_new = m_scr[g, pl.ds(r0, ROW_BLOCK), :]
            p = jnp.exp(s_scr[g, pl.ds(r0, ROW_BLOCK), 0:width]
                        - jnp.concatenate([m_new] * (width // LANES), axis=1))
            p_scr[g, pl.ds(r0, ROW_BLOCK), 0:width] = p.astype(BF16)
            return carry

        lax.fori_loop(0, rows // ROW_BLOCK, exp_pass, 0, unroll=ROW_UNROLL)
        pv = jnp.dot(p_scr[g, :, 0:width], vv1, preferred_element_type=F32)
        acc_scr[g] = pv if first else alpha * acc_scr[g] + pv

    def finish(g):
        acc = acc_scr[g]
        num, den = acc[:, g * HEAD_DIM:(g + 1) * HEAD_DIM], acc[:, (1 - g) * HEAD_DIM:(2 - g) * HEAD_DIM]
        return unstack(num / den, slice(None))

    o_cmp, o_win, qs_g, sel_g = [], [], [], []
    for g in range(NSA_KV_HEADS):
        gs = slice(g * HEAD_DIM, (g + 1) * HEAD_DIM)
        qs = jnp.concatenate([q_ref[:, (g * NSA_GROUP + r) * HEAD_DIM:(g * NSA_GROUP + r + 1) * HEAD_DIM]
                              for r in range(NSA_GROUP)], axis=0)
        qs_g.append(qs)
        kc = kc_ref[0][:, gs].astype(BF16)
        vc = vc_ref[0].astype(BF16)

        p = _softmax_rows(_dot_nt(qs, kc) + bias_c4) * keep_c4
        o_cmp += unstack(_dot(p, vc), gs)

        st = _dot_nt(kc, qs) + bias_ct4
        e = jnp.exp(st - jnp.max(st, axis=0, keepdims=True))
        pt = e / jnp.sum(e, axis=0, keepdims=True) * keep_ct4
        psum = pt[:, 0:qt]
        for r in range(1, NSA_GROUP):
            psum = psum + pt[:, r * qt:(r + 1) * qt]
        imp = _dot_split_rhs(overlap, psum)
        score = jnp.where(forced, FORCE_SCORE, jnp.where(allowed, imp, -1.0))
        rank = jnp.zeros((n_sel, qt), F32)
        for jp in range(n_sel):
            other = score[jp:jp + 1, :]
            tie = jnp.where(j > jp, 1.0, 0.0)
            rank = rank + jnp.where(other > score, 1.0, jnp.where(other == score, tie, 0.0))
        sel_t = jnp.where(rank < top_n, 1.0, 0.0)
        if n_sel < LANES:
            sel_t = jnp.concatenate([sel_t, jnp.zeros((LANES - n_sel, qt), F32)], axis=0)
        sel_g.append(sel_t.T.astype(BF16))

        attend(g, qs, kwt_ref[0, gs, pl.ds(wstart, wk)], vw_ref[pl.ds(wstart, wk), :], bias_w, True)
        o_win += finish(g)

    def chunk(c, first):
        k0 = pl.multiple_of(c * KEY_CHUNK, KEY_CHUNK)
        vv = vs_ref[pl.ds(k0, KEY_CHUNK), :]
        ex = expand_ref[:, pl.ds(k0, KEY_CHUNK)]
        causal = k0 + s_k <= t_k
        for g in range(NSA_KV_HEADS):
            chosen = jnp.dot(sel_g[g], ex, preferred_element_type=F32)
            attend(g, qs_g[g], kst_ref[0, g * HEAD_DIM:(g + 1) * HEAD_DIM, pl.ds(k0, KEY_CHUNK)], vv,
                   _mask_bias((chosen > 0.5) & causal), first)

    chunk(0, True)

    def body(c, carry):
        chunk(c, False)
        return carry

    lax.fori_loop(1, n_chunks, body, 0)
    o_sel = finish(0) + finish(1)

    gx = _dot_split_lhs(jax.nn.sigmoid(gl_ref[...]), gexp_ref[...])
    cat = lambda parts: jnp.concatenate(parts, axis=1)
    o_ref[...] = (gx[:, 0:NSA_W] * cat(o_cmp) + gx[:, NSA_W:2 * NSA_W] * cat(o_sel)
                  + gx[:, 2 * NSA_W:3 * NSA_W] * cat(o_win)).astype(BF16)


def _out_proj_kernel(yret_ref, ygm_ref, ynsa_ref, x_ref, mod_ref, wout_ref,
                     g2_ref, rwt_ref, rb_ref, xn_ref, hx_ref, bg_ref, cnt_ref, *, n_experts):
    mod = mod_ref[0]
    d_model = x_ref.shape[1]
    y = (jnp.dot(yret_ref[...], wout_ref[0:RET_W], preferred_element_type=F32)
         + jnp.dot(ygm_ref[...], wout_ref[RET_W:RET_W + GMLP_W], preferred_element_type=F32)
         + jnp.dot(ynsa_ref[...], wout_ref[RET_W + GMLP_W:], preferred_element_type=F32))
    xn = x_ref[...] + mod[2:3] * y
    xn_ref[...] = xn
    h2 = _rms_mod(xn, g2_ref[...], mod[4:5], mod[3:4])
    hx_ref[:, 0:d_model] = h2

    hh, hl = _split(h2)
    wh, wl = rwt_ref[0], rwt_ref[1]
    nt = lambda a, b: lax.dot_general(a, b, (((1,), (1,)), ((), ())), preferred_element_type=F32)
    logits = nt(wh, hh) + nt(wh, hl) + nt(wl, hh)
    ex = jnp.exp(logits - jnp.max(logits, axis=0, keepdims=True))
    probs = ex / jnp.sum(ex, axis=0, keepdims=True)
    sel = probs + rb_ref[...]
    per = n_experts // N_EXPERT_GROUPS
    srow = [sel[e:e + 1, :] for e in range(n_experts)]
    prow = [probs[e:e + 1, :] for e in range(n_experts)]
    gscore = []
    for gi in range(N_EXPERT_GROUPS):
        a = srow[gi * per:(gi + 1) * per]
        best = None
        for u in range(per):
            for w in range(u + 1, per):
                pair = a[u] + a[w]
                best = pair if best is None else jnp.maximum(best, pair)
        gscore.append(best)
    bg = jnp.zeros_like(gscore[0], dtype=jnp.int32)
    bs = gscore[0]
    for gi in range(1, N_EXPERT_GROUPS):
        upd = gscore[gi] > bs
        bg = jnp.where(upd, gi, bg)
        bs = jnp.where(upd, gscore[gi], bs)
    cs, cp = [], []
    for u in range(per):
        su, pu = srow[u], prow[u]
        for gi in range(1, N_EXPERT_GROUPS):
            su = jnp.where(bg == gi, srow[gi * per + u], su)
            pu = jnp.where(bg == gi, prow[gi * per + u], pu)
        cs.append(su)
        cp.append(pu)
    chosen = []
    for u in range(per):
        rank = jnp.zeros_like(cs[u])
        for w in range(per):
            if w == u:
                continue
            before = (cs[w] >= cs[u]) if w < u else (cs[w] > cs[u])
            rank = rank + jnp.where(before, 1.0, 0.0)
        chosen.append(jnp.where(rank < EXPERT_TOPK, 1.0, 0.0))
    denom = jnp.zeros_like(cp[0])
    for u in range(per):
        denom = denom + chosen[u] * cp[u]
    tm = bg.shape[1]
    grows = [chosen[u] * cp[u] / denom for u in range(per)]
    gpad = jnp.concatenate(grows + [jnp.zeros((LANES - per, tm), F32)], axis=0)
    hx_ref[:, d_model:d_model + LANES] = gpad.T
    bg_ref[...] = bg
    cnt_ref[0] = jnp.concatenate(
        [jnp.broadcast_to(jnp.sum(jnp.where(bg == gi, 1.0, 0.0), axis=1, keepdims=True), (1, LANES))
         for gi in range(N_EXPERT_GROUPS)], axis=0)


def _route_kernel(bg_ref, off_ref, tri_ref, dest_ref):
    bg = bg_ref[...]
    tm = bg.shape[1]
    member = [jnp.where(bg == gi, 1.0, 0.0) for gi in range(N_EXPERT_GROUPS)]
    pad = jnp.zeros((8 - N_EXPERT_GROUPS, tm), F32)
    before = jnp.dot(jnp.concatenate(member + [pad], axis=0).astype(BF16), tri_ref[...],
                     preferred_element_type=F32)
    off = off_ref[0]
    dest = jnp.zeros((1, tm), F32)
    for gi in range(N_EXPERT_GROUPS):
        start = jnp.concatenate([off[gi:gi + 1, :]] * (tm // LANES), axis=1)
        dest = dest + member[gi] * (start + before[gi:gi + 1, :])
    dest_ref[...] = dest.astype(jnp.int32)


def _dispatch_kernel(dest_ref, hx_ref, init_ref, xs_ref, stage, sems):
    del init_ref
    i = pl.program_id(0)
    n = hx_ref.shape[0]
    slot = i % 2

    def wait_slot(s):
        def wait(r, carry):
            pltpu.make_async_copy(stage.at[s, pl.ds(0, 1)], xs_ref.at[pl.ds(0, 1)], sems.at[s]).wait()
            return carry
        lax.fori_loop(0, n, wait, 0, unroll=8)

    @pl.when(i >= 2)
    def _():
        wait_slot(slot)

    stage[slot] = hx_ref[...]

    def start(r, carry):
        pltpu.make_async_copy(stage.at[slot, pl.ds(r, 1)], xs_ref.at[pl.ds(dest_ref[0, 0, r], 1)],
                              sems.at[slot]).start()
        return carry

    lax.fori_loop(0, n, start, 0, unroll=8)

    @pl.when(i == pl.num_programs(0) - 1)
    def _():
        @pl.when(i >= 1)
        def _():
            wait_slot(1 - slot)
        wait_slot(slot)


def _moe_kernel(grp_ref, used_ref, xs_ref, wg_ref, wu_ref, wd_ref, y_ref, acc_ref):
    del grp_ref
    u = pl.program_id(1)
    d_model = y_ref.shape[1]

    @pl.when(u == 0)
    def _():
        acc_ref[...] = jnp.zeros_like(acc_ref)

    @pl.when(pl.program_id(0) < used_ref[0])
    def _():
        h = xs_ref[:, 0:d_model].astype(BF16)
        gates = xs_ref[:, d_model:d_model + LANES]
        lane = lax.broadcasted_iota(jnp.int32, gates.shape, 1)
        gcol = jnp.sum(jnp.where(lane == u, gates, 0.0), axis=-1, keepdims=True)
        hg = jnp.dot(h, wg_ref[0].astype(BF16), preferred_element_type=F32)
        hu = jnp.dot(h, wu_ref[0].astype(BF16), preferred_element_type=F32)
        a = (hg * jax.nn.sigmoid(hg)) * hu
        acc_ref[...] += _dot(gcol * a, wd_ref[0])

    @pl.when(u == pl.num_programs(1) - 1)
    def _():
        y_ref[...] = acc_ref[...]


def _combine_kernel(cur_ref, nxt_ref, y_ref, xn_ref, mod_ref, o_ref, buf, sems):
    i = pl.program_id(0)
    n = xn_ref.shape[0]

    def request(idx_ref, slot):
        def start(r, carry):
            pltpu.make_async_copy(y_ref.at[pl.ds(idx_ref[0, 0, r], 1)], buf.at[slot, pl.ds(r, 1)],
                                  sems.at[slot]).start()
            return carry
        lax.fori_loop(0, n, start, 0, unroll=8)

    @pl.when(i == 0)
    def _():
        request(cur_ref, 0)

    @pl.when(i < pl.num_programs(0) - 1)
    def _():
        request(nxt_ref, (i + 1) % 2)

    slot = i % 2

    def wait(r, carry):
        pltpu.make_async_copy(y_ref.at[pl.ds(0, 1)], buf.at[slot, pl.ds(0, 1)], sems.at[slot]).wait()
        return carry

    lax.fori_loop(0, n, wait, 0, unroll=8)
    o_ref[...] = xn_ref[...] + mod_ref[0][5:6] * buf[slot]


def _decay_tables():
    H, C, d = RET_HEADS, CHUNK, HEAD_DIM
    log_gamma = jnp.log1p(-jnp.power(2.0, -5.0 - jnp.arange(H, dtype=F32)))
    idx = jnp.arange(C, dtype=F32)
    diff = idx[:, None] - idx[None, :]
    d_in = jnp.where(diff >= 0, jnp.exp(jnp.maximum(diff, 0.0)[None] * log_gamma[:, None, None]), 0.0).astype(F32)
    d_q = jnp.exp((idx + 1.0)[:, None] * log_gamma[None]).astype(F32)
    d_k = jnp.exp((C - 1.0 - idx)[:, None] * log_gamma[None]).astype(F32)
    d_c = jnp.exp(C * log_gamma).astype(F32)
    bcast = lambda a: jnp.broadcast_to(a.T[:, :, None], (H, C, d))
    return d_in, bcast(d_q), bcast(d_k), jnp.broadcast_to(d_c[:, None, None], (H, d, d))


def _gate_expand():
    m = np.zeros((LANES, N_BRANCH * NSA_W), np.float32)
    for h in range(NSA_HEADS):
        for br in range(N_BRANCH):
            m[h * N_BRANCH + br, br * NSA_W + h * HEAD_DIM: br * NSA_W + (h + 1) * HEAD_DIM] = 1.0
    return jnp.asarray(m, BF16)


def _block_expand(seq):
    m = (np.arange(LANES)[:, None] == (np.arange(seq) // SEL_BLOCK)[None, :]).astype(np.float32)
    return jnp.asarray(m, BF16)


def _compress_weights(pe, w1, w2):
    G, d = NSA_KV_HEADS, HEAD_DIM
    hid = w1.shape[-1]
    eye = jnp.eye(G, dtype=F32)
    w1h = w1.reshape(2, CMP_STRIDE, d, hid)
    w1b = jnp.einsum('pldh,ge->plgdeh', w1h, eye).reshape(2, CMP_STRIDE, G * d, G * hid)
    w2b = jnp.einsum('hd,ge->ghed', w2, eye).reshape(G * hid, G * d)
    peb = jnp.broadcast_to(pe.reshape(2, CMP_STRIDE, 1, d), (2, CMP_STRIDE, G, d)).reshape(2, CMP_STRIDE, G * d)
    return peb, w1b.astype(BF16), w2b.astype(BF16)


def kernel(x, c, positions, ada_w, ada_b, norm_mix_g, norm_ffn_g, w_in, w_out, ret_norm_g, gmlp_ln_g, gmlp_ws,
           gmlp_b, nsa_q_norm_g, nsa_k_norm_g, cmp_pe_k, cmp_pe_v, cmp_w1_k, cmp_w2_k, cmp_w1_v, cmp_w2_v,
           router_w, router_b, moe_w_gate, moe_w_up, moe_w_down):
    B, S, D = x.shape
    L = ada_w.shape[0]
    E = router_w.shape[1]
    DE = moe_w_gate.shape[-1]
    T = B * S
    n_chunk = S // CHUNK
    n_cmp = S // CMP_STRIDE
    n_sel = S // SEL_BLOCK
    top_n = min(SEL_TOPK, n_sel)
    n_qt = S // Q_TILE
    tm = min(512, S)
    tm_moe = min(1024, S)
    assert S % KEY_CHUNK == 0 and S >= WINDOW + Q_TILE and D == RET_W + GMLP_W + NSA_W
    assert w_in.shape[-1] == N_IN and T % tm_moe == 0 and n_sel <= LANES and NSA_KV_HEADS == 2

    half = HEAD_DIM // 2
    inv_freq = jnp.power(ROPE_THETA, -jnp.arange(half, dtype=F32) / half)
    invf = jnp.tile(inv_freq, LANES // half)[None, :]
    cos_t, sin_t = pl.pallas_call(
        _rope_table_kernel, name="rope_tables", grid=(B,),
        in_specs=[pl.BlockSpec((1, S, 1), lambda b: (b, 0, 0)), pl.BlockSpec((1, LANES), lambda b: (0, 0))],
        out_specs=[pl.BlockSpec((1, S, LANES), lambda b: (b, 0, 0))] * 2,
        out_shape=[jax.ShapeDtypeStruct((B, S, LANES), F32)] * 2,
        compiler_params=_cparams("parallel"),
    )(positions[:, :, None], invf)
    pad_c = lambda a: jnp.pad(a[:, CMP_LEN - 1::CMP_STRIDE], ((0, 0), (0, 1), (0, 0)))
    cos_c, sin_c = pad_c(cos_t), pad_c(sin_t)
    cos_f, sin_f = cos_t.reshape(T, LANES), sin_t.reshape(T, LANES)

    mod = pl.pallas_call(
        _ada_kernel, name="ada_mod", grid=(L, 6),
        in_specs=[pl.BlockSpec((B, D), lambda l, j: (0, 0)),
                  pl.BlockSpec((1, D, D), lambda l, j: (l, 0, j)),
                  pl.BlockSpec((1, 1, 1, D), lambda l, j: (l, j, 0, 0))],
        out_specs=pl.BlockSpec((1, 1, B, D), lambda l, j: (l, j, 0, 0)),
        out_shape=jax.ShapeDtypeStruct((L, 6, B, D), F32),
        compiler_params=_cparams("parallel", "parallel"),
    )(c, ada_w, ada_b.reshape(L, 6, 1, D))
    mod = mod.transpose(0, 2, 1, 3)

    d_in, d_q, d_k, d_c = _decay_tables()
    gexp = _gate_expand()
    expand = _block_expand(S)
    rw_hi = router_w.T.astype(BF16)
    rw_lo = (router_w.T - rw_hi.astype(F32)).astype(BF16)
    rwt = jnp.stack([rw_hi, rw_lo])
    rb = router_b.reshape(E, 1)
    tri = jnp.asarray(np.triu(np.ones((tm, tm), np.float32), k=1), BF16)
    x2 = x.reshape(T, D)
    tiles_per_batch = S // tm
    row_spec = lambda w: pl.BlockSpec((tm, w), lambda i: (i, 0))
    full = lambda shape: pl.BlockSpec(shape, lambda *_: (0,) * len(shape))

    for l in range(L):
        mod_l = mod[l]
        w_in_p = jnp.pad(w_in[l], ((0, 0), (0, N_IN_PAD - N_IN))).astype(BF16)
        raw_widths = tuple(hi - lo for lo, hi in _IN_RAW)
        mod_spec = pl.BlockSpec((1, 6, D), lambda i: (i // tiles_per_batch, 0, 0))
        kt_out = pl.BlockSpec((1, KV_W, tm), lambda i: (i // tiles_per_batch, 0, i % tiles_per_batch))
        kgain = jnp.tile(nsa_k_norm_g[l], (1, NSA_KV_HEADS))
        zr, zg, zkc, zvc, zgl, q_r, ks_t, vs_b, kw_t, vw_b = pl.pallas_call(
            _in_proj_kernel, name="in_proj", grid=(T // tm,),
            in_specs=[row_spec(D), mod_spec, full((1, D)), full((D, N_IN_PAD)), row_spec(LANES), row_spec(LANES),
                      full((1, NSA_W)), full((N_BRANCH, KV_W))],
            out_specs=[row_spec(w) for w in raw_widths]
            + [row_spec(NSA_W), kt_out, row_spec(KV_W), kt_out, row_spec(KV_W)],
            out_shape=[jax.ShapeDtypeStruct((T, w), F32) for w in raw_widths]
            + [jax.ShapeDtypeStruct((T, NSA_W), BF16), jax.ShapeDtypeStruct((B, KV_W, S), BF16),
               jax.ShapeDtypeStruct((T, KV_W), BF16), jax.ShapeDtypeStruct((B, KV_W, S), BF16),
               jax.ShapeDtypeStruct((T, KV_W), BF16)],
            compiler_params=_cparams("parallel"),
        )(x2, mod_l, norm_mix_g[l][None, :], w_in_p, cos_f, sin_f,
          jnp.tile(nsa_q_norm_g[l], NSA_HEADS)[None, :], kgain)

        crow = lambda w: pl.BlockSpec((MIX_BATCH, CHUNK, w), lambda b, cc: (b, cc, 0))
        y_ret, y_gm = pl.pallas_call(
            _mixer_kernel, name="ret_gmlp", grid=(B // MIX_BATCH, n_chunk),
            in_specs=[crow(4 * RET_W), crow(2 * GMLP_W), crow(LANES), crow(LANES),
                      full((RET_HEADS, CHUNK, CHUNK)), full((RET_HEADS, CHUNK, HEAD_DIM)),
                      full((RET_HEADS, CHUNK, HEAD_DIM)), full((RET_HEADS, HEAD_DIM, HEAD_DIM)),
                      full((RET_HEADS, 1, HEAD_DIM)), full((1, GMLP_W)),
                      full((GMLP_GROUPS, CHUNK, CHUNK)), full((GMLP_GROUPS, CHUNK, HEAD_DIM))],
            out_specs=[crow(RET_W), crow(GMLP_W)],
            out_shape=[jax.ShapeDtypeStruct((B, S, RET_W), BF16), jax.ShapeDtypeStruct((B, S, GMLP_W), BF16)],
            scratch_shapes=[pltpu.VMEM((MIX_BATCH, RET_HEADS, HEAD_DIM, HEAD_DIM), F32)],
            compiler_params=_cparams("parallel", "arbitrary"),
        )(zr.reshape(B, S, 4 * RET_W), zg.reshape(B, S, 2 * GMLP_W), cos_t, sin_t, d_in, d_q, d_k, d_c,
          ret_norm_g[l][:, None, :], gmlp_ln_g[l][None, :], gmlp_ws[l],
          jnp.broadcast_to(gmlp_b[l][:, :, None], (GMLP_GROUPS, CHUNK, HEAD_DIM)))
        y_ret, y_gm = y_ret.reshape(T, RET_W), y_gm.reshape(T, GMLP_W)

        pek, w1k, w2k = _compress_weights(cmp_pe_k[l], cmp_w1_k[l], cmp_w2_k[l])
        pev, w1v, w2v = _compress_weights(cmp_pe_v[l], cmp_w1_v[l], cmp_w2_v[l])
        cblk = pl.BlockSpec((S, KV_W), lambda b: (b, 0))
        ghid = w1k.shape[-1]
        ctab = pl.BlockSpec((1, n_cmp, LANES), lambda b: (b, 0, 0))
        kc, vc = pl.pallas_call(
            _compress_kernel, name="nsa_compress", grid=(B,),
            in_specs=[cblk, cblk, full((2, CMP_STRIDE, KV_W)), full((2, CMP_STRIDE, KV_W)),
                      full((2, CMP_STRIDE, KV_W, ghid)), full((ghid, KV_W)),
                      full((2, CMP_STRIDE, KV_W, ghid)), full((ghid, KV_W)),
                      full((N_BRANCH, KV_W)), ctab, ctab],
            out_specs=[ctab, ctab],
            out_shape=[jax.ShapeDtypeStruct((B, n_cmp, KV_W), F32)] * 2,
            compiler_params=_cparams("parallel"),
        )(zkc, zvc, pek, pev, w1k, w2k, w1v, w2v,
          kgain, cos_c, sin_c)

        qrow = lambda w: pl.BlockSpec((Q_TILE, w), lambda b, i: (b * n_qt + i, 0))
        kt_in = pl.BlockSpec((1, KV_W, S), lambda b, i: (b, 0, 0))
        v_in = pl.BlockSpec((S, KV_W), lambda b, i: (b, 0))
        ctab2 = pl.BlockSpec((1, n_cmp, KV_W), lambda b, i: (b, 0, 0))
        y_nsa = pl.pallas_call(
            functools.partial(_nsa_kernel, n_sel=n_sel, top_n=top_n), name="nsa_attn", grid=(B, n_qt),
            in_specs=[qrow(NSA_W), qrow(LANES), ctab2, ctab2, kt_in, v_in, kt_in, v_in,
                      full((LANES, N_BRANCH * NSA_W)), full((LANES, S))],
            out_specs=qrow(NSA_W),
            out_shape=jax.ShapeDtypeStruct((T, NSA_W), BF16),
            scratch_shapes=[
                pltpu.VMEM((NSA_KV_HEADS, NSA_GROUP * Q_TILE, WINDOW + Q_TILE), F32),
                pltpu.VMEM((NSA_KV_HEADS, NSA_GROUP * Q_TILE, WINDOW + Q_TILE), BF16),
                pltpu.VMEM((NSA_KV_HEADS, NSA_GROUP * Q_TILE, LANES), F32),
                pltpu.VMEM((NSA_KV_HEADS, NSA_GROUP * Q_TILE, KV_W), F32)],
            compiler_params=_cparams("parallel", "parallel"),
        )(q_r, zgl, kc, vc, ks_t, vs_b, kw_t, vw_b, gexp, expand)

        n_tok_tiles = T // tm
        DX = D + LANES
        x_new, hx, bgrp, cnt = pl.pallas_call(
            functools.partial(_out_proj_kernel, n_experts=E), name="out_proj_router", grid=(n_tok_tiles,),
            in_specs=[row_spec(RET_W), row_spec(GMLP_W), row_spec(NSA_W), row_spec(D),
                      pl.BlockSpec((1, 6, D), lambda i: (i // tiles_per_batch, 0, 0)),
                      full((D, D)), full((1, D)), full((2, E, D)), full((E, 1))],
            out_specs=[row_spec(D), row_spec(DX), pl.BlockSpec((1, tm), lambda i: (0, i)),
                       pl.BlockSpec((1, N_EXPERT_GROUPS, LANES), lambda i: (i, 0, 0))],
            out_shape=[jax.ShapeDtypeStruct((T, D), F32), jax.ShapeDtypeStruct((T, DX), F32),
                       jax.ShapeDtypeStruct((1, T), jnp.int32),
                       jax.ShapeDtypeStruct((n_tok_tiles, N_EXPERT_GROUPS, LANES), F32)],
            compiler_params=_cparams("parallel"),
        )(y_ret, y_gm, y_nsa, x2, mod_l, w_out[l].astype(BF16), norm_ffn_g[l][None, :], rwt, rb)

        tile_cnt = cnt[:, :, 0].astype(jnp.int32)
        seg = ((jnp.sum(tile_cnt, axis=0) + tm_moe - 1) // tm_moe) * tm_moe
        earlier = lambda n: jnp.arange(n)[:, None] > jnp.arange(n)[None, :]
        seg_start = jnp.sum(jnp.where(earlier(N_EXPERT_GROUPS), seg[None, :], 0), axis=1)
        tile_off = seg_start[None, :] + jnp.sum(
            jnp.where(earlier(n_tok_tiles)[:, :, None], tile_cnt[None, :, :], 0), axis=1)
        n_row_tiles = T // tm_moe + N_EXPERT_GROUPS
        row0 = jnp.arange(n_row_tiles) * tm_moe
        tile_grp = jnp.minimum(jnp.sum(((seg_start + seg)[None, :] <= row0[:, None]).astype(jnp.int32), axis=1),
                               N_EXPERT_GROUPS - 1)
        dest = pl.pallas_call(
            _route_kernel, name="moe_route", grid=(n_tok_tiles,),
            in_specs=[pl.BlockSpec((1, tm), lambda i: (0, i)),
                      pl.BlockSpec((1, N_EXPERT_GROUPS, LANES), lambda i: (i, 0, 0)), full((tm, tm))],
            out_specs=pl.BlockSpec((1, tm), lambda i: (0, i)),
            out_shape=jax.ShapeDtypeStruct((1, T), jnp.int32),
            compiler_params=_cparams("parallel"),
        )(bgrp, jnp.broadcast_to(tile_off.astype(F32)[:, :, None], (n_tok_tiles, N_EXPERT_GROUPS, LANES)), tri)
        dest3 = dest.reshape(n_tok_tiles, 1, tm)
        dest_spec = pl.BlockSpec((1, 1, tm), lambda i: (i, 0, 0), memory_space=pltpu.SMEM)
        any_spec = pl.BlockSpec(memory_space=pl.ANY)

        n_rows = n_row_tiles * tm_moe
        xs = pl.pallas_call(
            _dispatch_kernel, name="moe_dispatch", grid=(n_tok_tiles,),
            in_specs=[dest_spec, row_spec(DX), any_spec],
            out_specs=any_spec,
            out_shape=jax.ShapeDtypeStruct((n_rows, DX), F32),
            scratch_shapes=[pltpu.VMEM((2, tm, DX), F32), pltpu.SemaphoreType.DMA((2,))],
            input_output_aliases={2: 0},
            compiler_params=_cparams("arbitrary"),
        )(dest3, hx, jnp.zeros((n_rows, DX), F32) if l == 0 else xs)

        per = E // N_EXPERT_GROUPS
        tiles_used = (jnp.sum(seg) // tm_moe).astype(jnp.int32).reshape(1)

        def w_idx(k, u, grp, used):
            return l, grp[k] * per + jnp.where(k < used[0], u, per - 1), 0, 0

        ys = pl.pallas_call(
            _moe_kernel, name="moe",
            grid_spec=pltpu.PrefetchScalarGridSpec(
                num_scalar_prefetch=2, grid=(n_row_tiles, per),
                in_specs=[pl.BlockSpec((tm_moe, DX), lambda k, u, grp, used: (k, 0)),
                          pl.BlockSpec((None, 1, D, DE), w_idx), pl.BlockSpec((None, 1, D, DE), w_idx),
                          pl.BlockSpec((None, 1, DE, D), w_idx)],
                out_specs=pl.BlockSpec((tm_moe, D), lambda k, u, grp, used: (k, 0)),
                scratch_shapes=[pltpu.VMEM((tm_moe, D), F32)]),
            out_shape=jax.ShapeDtypeStruct((n_rows, D), F32),
            compiler_params=_cparams("parallel", "arbitrary"),
        )(tile_grp, tiles_used, xs, moe_w_gate, moe_w_up, moe_w_down)

        dest_next = pl.BlockSpec((1, 1, tm), lambda i: (jnp.minimum(i + 1, n_tok_tiles - 1), 0, 0),
                                 memory_space=pltpu.SMEM)
        x2 = pl.pallas_call(
            _combine_kernel, name="moe_combine", grid=(n_tok_tiles,),
            in_specs=[dest_spec, dest_next, any_spec, row_spec(D),
                      pl.BlockSpec((1, 6, D), lambda i: (i // tiles_per_batch, 0, 0))],
            out_specs=row_spec(D),
            out_shape=jax.ShapeDtypeStruct((T, D), F32),
            scratch_shapes=[pltpu.VMEM((2, tm, D), F32), pltpu.SemaphoreType.DMA((2,))],
            compiler_params=_cparams("arbitrary"),
        )(dest3, dest3, ys, x_new, mod_l)

    return x2.reshape(B, S, D)
```

```python
import functools

import numpy as np
import jax
import jax.numpy as jnp
from jax import lax
from jax.experimental import pallas as pl
from jax.experimental.pallas import tpu as pltpu

F32 = jnp.float32
BF16 = jnp.bfloat16

HEAD_DIM = 64
LANES = 128
RET_HEADS = 4
GMLP_GROUPS = 4
NSA_HEADS = 8
NSA_KV_HEADS = 2
NSA_GROUP = NSA_HEADS // NSA_KV_HEADS
N_BRANCH = 3
CHUNK = 128
CMP_LEN = 32
CMP_STRIDE = 16
SEL_BLOCK = 64
SEL_TOPK = 8
WINDOW = 512
ROPE_THETA = 10000.0
N_EXPERT_GROUPS = 4
EXPERT_TOPK = 2
NORM_EPS = 1e-6
NEG_INF = -1e30
FORCE_SCORE = 1e4

RET_W = RET_HEADS * HEAD_DIM
GMLP_W = GMLP_GROUPS * HEAD_DIM
NSA_W = NSA_HEADS * HEAD_DIM
KV_W = NSA_KV_HEADS * HEAD_DIM
GATE_W = NSA_HEADS * N_BRANCH
OFF_RET = 0
OFF_GMLP = 4 * RET_W
OFF_Q = OFF_GMLP + 2 * GMLP_W
OFF_KV = OFF_Q + NSA_W
OFF_GL = OFF_KV + 6 * KV_W
N_IN = OFF_GL + GATE_W
N_IN_PAD = OFF_GL + LANES

Q_TILE = 256
KEY_CHUNK = 512
MIX_BATCH = 2
ROW_BLOCK = 32
ROW_UNROLL = True
VMEM_LIMIT = 48 * 1024 * 1024


def _cparams(*sem):
    return pltpu.CompilerParams(dimension_semantics=sem, vmem_limit_bytes=VMEM_LIMIT)


def _dot(a, b):
    return jnp.dot(a.astype(BF16), b.astype(BF16), preferred_element_type=F32)


def _dot_nt(a, b):
    return lax.dot_general(a.astype(BF16), b.astype(BF16), (((1,), (1,)), ((), ())),
                           preferred_element_type=F32)


def _split(a):
    hi = a.astype(BF16)
    lo = (a - hi.astype(F32)).astype(BF16)
    return hi, lo


def _dot_split_lhs(a, b):
    hi, lo = _split(a)
    return (jnp.dot(hi, b, preferred_element_type=F32) + jnp.dot(lo, b, preferred_element_type=F32))


def _dot_split_rhs(a, b):
    hi, lo = _split(b)
    return (jnp.dot(a, hi, preferred_element_type=F32) + jnp.dot(a, lo, preferred_element_type=F32))


def _head_ones(width=LANES):
    r = lax.broadcasted_iota(jnp.int32, (width, width), 0) // HEAD_DIM
    c = lax.broadcasted_iota(jnp.int32, (width, width), 1) // HEAD_DIM
    return jnp.where(r == c, 1.0, 0.0).astype(BF16)


def _rotate_half(y):
    lane = lax.broadcasted_iota(jnp.int32, y.shape, 1)
    first = (lane & (HEAD_DIM - 1)) < (HEAD_DIM // 2)
    return jnp.where(first, -pltpu.roll(y, LANES - HEAD_DIM // 2, 1), pltpu.roll(y, HEAD_DIM // 2, 1))


def _rope_slab(y, cos, sin):
    return y * cos + _rotate_half(y) * sin


def _rope(x, cos, sin):
    return jnp.concatenate(
        [_rope_slab(x[:, k * LANES:(k + 1) * LANES], cos, sin) for k in range(x.shape[1] // LANES)], axis=1)


def _head_rms_rope(x, gain, cos, sin, ones):
    outs = []
    for k in range(x.shape[1] // LANES):
        xs = x[:, k * LANES:(k + 1) * LANES]
        ssq = _dot_split_lhs(xs * xs, ones)
        y = xs * lax.rsqrt(ssq * (1.0 / HEAD_DIM) + NORM_EPS) * gain[:, k * LANES:(k + 1) * LANES]
        outs.append(_rope_slab(y, cos, sin))
    return outs[0] if len(outs) == 1 else jnp.concatenate(outs, axis=1)


def _rms_mod(x, gain, scale, shift):
    y = x * lax.rsqrt(jnp.mean(x * x, axis=-1, keepdims=True) + NORM_EPS)
    return (y * gain) * (1.0 + scale) + shift


def _softmax_rows(s):
    e = jnp.exp(s - jnp.max(s, axis=-1, keepdims=True))
    return e / jnp.sum(e, axis=-1, keepdims=True)


def _mask_bias(keep):
    return jnp.where(keep, 0.0, NEG_INF)


def _rope_table_kernel(pos_ref, invf_ref, cos_ref, sin_ref):
    ang = pos_ref[0].astype(F32) * invf_ref[...]
    cos_ref[0] = jnp.cos(ang)
    sin_ref[0] = jnp.sin(ang)


def _ada_kernel(c_ref, w_ref, b_ref, o_ref):
    c = c_ref[...]
    a = c * jax.nn.sigmoid(c)
    o_ref[0, 0] = jnp.dot(a, w_ref[0], preferred_element_type=F32,
                          precision=lax.Precision.HIGHEST) + b_ref[0, 0]


_IN_RAW = ((OFF_RET, OFF_GMLP), (OFF_GMLP, OFF_Q), (OFF_KV, OFF_KV + KV_W), (OFF_KV + KV_W, OFF_KV + 2 * KV_W),
           (OFF_GL, N_IN_PAD))


def _in_proj_kernel(x_ref, mod_ref, g_ref, w_ref, cos_ref, sin_ref, qg_ref, kg_ref, *out_refs):
    *raw_refs, q_out, kst_out, vs_out, kwt_out, vw_out = out_refs
    mod = mod_ref[0]
    h = _rms_mod(x_ref[...], g_ref[...], mod[1:2], mod[0:1]).astype(BF16)
    proj = lambda lo, hi: jnp.dot(h, w_ref[:, lo:hi], preferred_element_type=F32)
    tail = proj(OFF_KV, N_IN_PAD)
    cols = lambda lo, hi: tail[:, lo - OFF_KV:hi - OFF_KV] if lo >= OFF_KV else proj(lo, hi)
    for ref, (lo, hi) in zip(raw_refs, _IN_RAW):
        ref[...] = cols(lo, hi)

    ones = _head_ones()
    cos, sin = cos_ref[...], sin_ref[...]
    q_out[...] = (_head_rms_rope(proj(OFF_Q, OFF_KV), qg_ref[...], cos, sin, ones) * (HEAD_DIM ** -0.5)).astype(BF16)
    sel_kv = OFF_KV + 2 * KV_W
    kst_out[0] = _head_rms_rope(cols(sel_kv, sel_kv + KV_W), kg_ref[1:2], cos, sin, ones).T.astype(BF16)
    vs_out[...] = cols(sel_kv + KV_W, sel_kv + 2 * KV_W).astype(BF16)
    kwt_out[0] = _head_rms_rope(cols(sel_kv + 2 * KV_W, sel_kv + 3 * KV_W), kg_ref[2:3], cos, sin, ones).T.astype(BF16)
    vw_out[...] = cols(sel_kv + 3 * KV_W, OFF_GL).astype(BF16)


def _mixer_kernel(zr_ref, zg_ref, cos_ref, sin_ref, din_ref, dq_ref, dk_ref, dc_ref, rg_ref,
                  lng_ref, ws_ref, bs_ref, yret_ref, ygm_ref, state_ref):
    @pl.when(pl.program_id(1) == 0)
    def _():
        state_ref[...] = jnp.zeros_like(state_ref)

    row = lax.broadcasted_iota(jnp.int32, (CHUNK, CHUNK), 0)
    col = lax.broadcasted_iota(jnp.int32, (CHUNK, CHUNK), 1)
    ws = [jnp.where(row >= col, ws_ref[gi], 0.0).astype(BF16) for gi in range(GMLP_GROUPS)]

    for bb in range(zr_ref.shape[0]):
        cos, sin = cos_ref[bb], sin_ref[bb]
        q = _rope(zr_ref[bb, :, 0:RET_W], cos, sin)
        k = _rope(zr_ref[bb, :, RET_W:2 * RET_W], cos, sin) * (HEAD_DIM ** -0.5)
        v = zr_ref[bb, :, 2 * RET_W:3 * RET_W]
        g = zr_ref[bb, :, 3 * RET_W:4 * RET_W]
        outs = []
        for h in range(RET_HEADS):
            sl = slice(h * HEAD_DIM, (h + 1) * HEAD_DIM)
            qh, kh, vh = q[:, sl], k[:, sl], v[:, sl]
            st = state_ref[bb, h]
            inner = _dot_nt(qh, kh) * din_ref[h]
            o = _dot(inner, vh) + _dot(qh, st) * dq_ref[h]
            state_ref[bb, h] = st * dc_ref[h] + _dot((kh * dk_ref[h]).T, vh)
            o = o * lax.rsqrt(jnp.mean(o * o, axis=-1, keepdims=True) + NORM_EPS) * rg_ref[h]
            outs.append(o)
        yret_ref[bb] = (jnp.concatenate(outs, axis=1) * (g * jax.nn.sigmoid(g))).astype(BF16)

        u = jax.nn.gelu(zg_ref[bb, :, 0:GMLP_W])
        vv = jax.nn.gelu(zg_ref[bb, :, GMLP_W:2 * GMLP_W])
        mu = jnp.mean(vv, axis=-1, keepdims=True)
        var = jnp.mean(jnp.square(vv - mu), axis=-1, keepdims=True)
        vn = (vv - mu) * lax.rsqrt(var + NORM_EPS) * lng_ref[...]
        ss = [_dot(ws[gi], vn[:, gi * HEAD_DIM:(gi + 1) * HEAD_DIM]) + bs_ref[gi] for gi in range(GMLP_GROUPS)]
        ygm_ref[bb] = (u * jnp.concatenate(ss, axis=1)).astype(BF16)


def _compress_kernel(xk_ref, xv_ref, pek_ref, pev_ref, w1k_ref, w2k_ref, w1v_ref, w2v_ref, kg_ref,
                     cos_ref, sin_ref, kc_ref, vc_ref):
    n_rows = kc_ref.shape[1]

    def comp(x_ref, pe_ref, w1_ref, w2_ref):
        a = b = None
        for tok in range(CMP_STRIDE):
            x = x_ref[pl.ds(tok, n_rows, stride=CMP_STRIDE), :]
            ta = _dot(x + pe_ref[0, tok:tok + 1, :], w1_ref[0, tok])
            tb = _dot(x + pe_ref[1, tok:tok + 1, :], w1_ref[1, tok])
            a, b = (ta, tb) if a is None else (a + ta, b + tb)
        h = a + pltpu.roll(b, n_rows - 1, 0)
        return _dot(jax.nn.gelu(h), w2_ref[...])

    kc_ref[0] = _head_rms_rope(comp(xk_ref, pek_ref, w1k_ref, w2k_ref), kg_ref[0:1],
                               cos_ref[0], sin_ref[0], _head_ones())
    vc_ref[0] = comp(xv_ref, pev_ref, w1v_ref, w2v_ref)


def _nsa_kernel(q_ref, gl_ref, kc_ref, vc_ref, kst_ref, vs_ref, kwt_ref, vw_ref, gexp_ref, expand_ref, o_ref,
                s_scr, p_scr, m_scr, acc_scr, *, n_sel, top_n):
    i = pl.program_id(1)
    qt = q_ref.shape[0]
    n_cmp = kc_ref.shape[1]
    t0 = i * qt
    n_chunks = (t0 + qt + KEY_CHUNK - 1) // KEY_CHUNK
    wk = WINDOW + qt
    wstart = pl.multiple_of(jnp.maximum(t0 - WINDOW, 0), LANES)

    t_r = t0 + lax.broadcasted_iota(jnp.int32, (qt, n_cmp), 0)
    n_c = lax.broadcasted_iota(jnp.int32, (qt, n_cmp), 1)
    keep_c = jnp.where(n_c * CMP_STRIDE + (CMP_LEN - 1) <= t_r, 1.0, 0.0)
    bias_c = (keep_c - 1.0) * (-NEG_INF)
    t_l = t0 + lax.broadcasted_iota(jnp.int32, (n_cmp, qt), 1)
    n_s = lax.broadcasted_iota(jnp.int32, (n_cmp, qt), 0)
    keep_ct = jnp.where(n_s * CMP_STRIDE + (CMP_LEN - 1) <= t_l, 1.0, 0.0)
    keep_ct4 = jnp.concatenate([keep_ct] * NSA_GROUP, axis=1)
    bias_ct4 = (keep_ct4 - 1.0) * (-NEG_INF)
    jo = lax.broadcasted_iota(jnp.int32, (n_sel, n_cmp), 0) * SEL_BLOCK
    no = lax.broadcasted_iota(jnp.int32, (n_sel, n_cmp), 1) * CMP_STRIDE
    overlap = jnp.where((no < jo + SEL_BLOCK) & (no + CMP_LEN > jo), 1.0, 0.0).astype(BF16)
    j = lax.broadcasted_iota(jnp.int32, (n_sel, qt), 0)
    tq = t0 + lax.broadcasted_iota(jnp.int32, (n_sel, qt), 1)
    cur = tq // SEL_BLOCK
    forced = (j == 0) | (j == cur) | (j == cur - 1)
    allowed = j * SEL_BLOCK <= tq
    t_w = t0 + lax.broadcasted_iota(jnp.int32, (qt, wk), 0)
    s_w = wstart + lax.broadcasted_iota(jnp.int32, (qt, wk), 1)
    bias_w = _mask_bias((s_w <= t_w) & (t_w - s_w < WINDOW))
    t_k = t0 + lax.broadcasted_iota(jnp.int32, (qt, KEY_CHUNK), 0)
    s_k = lax.broadcasted_iota(jnp.int32, (qt, KEY_CHUNK), 1)

    rep = lambda a: jnp.concatenate([a] * NSA_GROUP, axis=0)
    keep_c4, bias_c4 = rep(keep_c), rep(bias_c)
    unstack = lambda o, gs: [o[r * qt:(r + 1) * qt, gs] for r in range(NSA_GROUP)]
    rows = NSA_GROUP * qt
    lane_v = lax.broadcasted_iota(jnp.int32, (1, KV_W), 1)

    def attend(g, qs, kt, vv, bias, first):
        width = bias.shape[1]
        own = (lane_v >= g * HEAD_DIM) & (lane_v < (g + 1) * HEAD_DIM)
        vv1 = jnp.where(own, vv, jnp.ones_like(vv))
        s = jnp.dot(qs, kt, preferred_element_type=F32) + rep(bias)
        s_scr[g, :, 0:width] = s
        mx = jnp.max(s, axis=-1, keepdims=True)
        if first:
            m_scr[g] = jnp.broadcast_to(mx, (rows, LANES))
        else:
            m_old = m_scr[g]
            m_new = jnp.maximum(m_old, mx)
            alpha = jnp.exp(m_old - m_new)
            m_scr[g] = m_new

        def exp_pass(b, carry):
            r0 = pl.multiple_of(b * ROW_BLOCK, ROW_BLOCK)
            m_new = m_scr[g, pl.ds(r0, ROW_BLOCK), :]
            p = jnp.exp(s_scr[g, pl.ds(r0, ROW_BLOCK), 0:width]
                        - jnp.concatenate([m_new] * (width // LANES), axis=1))
            p_scr[g, pl.ds(r0, ROW_BLOCK), 0:width] = p.astype(BF16)
            return carry

        lax.fori_loop(0, rows // ROW_BLOCK, exp_pass, 0, unroll=ROW_UNROLL)
        pv = jnp.dot(p_scr[g, :, 0:width], vv1, preferred_element_type=F32)
        acc_scr[g] = pv if first else alpha * acc_scr[g] + pv

    def finish(g):
        acc = acc_scr[g]
        num, den = acc[:, g * HEAD_DIM:(g + 1) * HEAD_DIM], acc[:, (1 - g) * HEAD_DIM:(2 - g) * HEAD_DIM]
        return unstack(num / den, slice(None))

    o_cmp, o_win, qs_g, sel_g = [], [], [], []
    for g in range(NSA_KV_HEADS):
        gs = slice(g * HEAD_DIM, (g + 1) * HEAD_DIM)
        qs = jnp.concatenate([q_ref[:, (g * NSA_GROUP + r) * HEAD_DIM:(g * NSA_GROUP + r + 1) * HEAD_DIM]
                              for r in range(NSA_GROUP)], axis=0)
        qs_g.append(qs)
        kc = kc_ref[0][:, gs].astype(BF16)
        vc = vc_ref[0].astype(BF16)

        p = _softmax_rows(_dot_nt(qs, kc) + bias_c4) * keep_c4
        o_cmp += unstack(_dot(p, vc), gs)

        st = _dot_nt(kc, qs) + bias_ct4
        e = jnp.exp(st - jnp.max(st, axis=0, keepdims=True))
        pt = e / jnp.sum(e, axis=0, keepdims=True) * keep_ct4
        psum = pt[:, 0:qt]
        for r in range(1, NSA_GROUP):
            psum = psum + pt[:, r * qt:(r + 1) * qt]
        imp = _dot_split_rhs(overlap, psum)
        score = jnp.where(forced, FORCE_SCORE, jnp.where(allowed, imp, -1.0))
        rank = jnp.zeros((n_sel, qt), F32)
        for jp in range(n_sel):
            other = score[jp:jp + 1, :]
            tie = jnp.where(j > jp, 1.0, 0.0)
            rank = rank + jnp.where(other > score, 1.0, jnp.where(other == score, tie, 0.0))
        sel_t = jnp.where(rank < top_n, 1.0, 0.0)
        if n_sel < LANES:
            sel_t = jnp.concatenate([sel_t, jnp.zeros((LANES - n_sel, qt), F32)], axis=0)
        sel_g.append(sel_t.T.astype(BF16))

        attend(g, qs, kwt_ref[0, gs, pl.ds(wstart, wk)], vw_ref[pl.ds(wstart, wk), :], bias_w, True)
        o_win += finish(g)

    def chunk(c, first):
        k0 = pl.multiple_of(c * KEY_CHUNK, KEY_CHUNK)
        vv = vs_ref[pl.ds(k0, KEY_CHUNK), :]
        ex = expand_ref[:, pl.ds(k0, KEY_CHUNK)]
        causal = k0 + s_k <= t_k
        for g in range(NSA_KV_HEADS):
            chosen = jnp.dot(sel_g[g], ex, preferred_element_type=F32)
            attend(g, qs_g[g], kst_ref[0, g * HEAD_DIM:(g + 1) * HEAD_DIM, pl.ds(k0, KEY_CHUNK)], vv,
                   _mask_bias((chosen > 0.5) & causal), first)

    chunk(0, True)

    def body(c, carry):
        chunk(c, False)
        return carry

    lax.fori_loop(1, n_chunks, body, 0)
    o_sel = finish(0) + finish(1)

    gx = _dot_split_lhs(jax.nn.sigmoid(gl_ref[...]), gexp_ref[...])
    cat = lambda parts: jnp.concatenate(parts, axis=1)
    o_ref[...] = (gx[:, 0:NSA_W] * cat(o_cmp) + gx[:, NSA_W:2 * NSA_W] * cat(o_sel)
                  + gx[:, 2 * NSA_W:3 * NSA_W] * cat(o_win)).astype(BF16)


def _out_proj_kernel(yret_ref, ygm_ref, ynsa_ref, x_ref, mod_ref, wout_ref,
                     g2_ref, rwt_ref, rb_ref, xn_ref, hx_ref, bg_ref, cnt_ref, *, n_experts):
    mod = mod_ref[0]
    d_model = x_ref.shape[1]
    y = (jnp.dot(yret_ref[...], wout_ref[0:RET_W], preferred_element_type=F32)
         + jnp.dot(ygm_ref[...], wout_ref[RET_W:RET_W + GMLP_W], preferred_element_type=F32)
         + jnp.dot(ynsa_ref[...], wout_ref[RET_W + GMLP_W:], preferred_element_type=F32))
    xn = x_ref[...] + mod[2:3] * y
    xn_ref[...] = xn
    h2 = _rms_mod(xn, g2_ref[...], mod[4:5], mod[3:4])
    hx_ref[:, 0:d_model] = h2

    hh, hl = _split(h2)
    wh, wl = rwt_ref[0], rwt_ref[1]
    nt = lambda a, b: lax.dot_general(a, b, (((1,), (1,)), ((), ())), preferred_element_type=F32)
    logits = nt(wh, hh) + nt(wh, hl) + nt(wl, hh)
    ex = jnp.exp(logits - jnp.max(logits, axis=0, keepdims=True))
    probs = ex / jnp.sum(ex, axis=0, keepdims=True)
    sel = probs + rb_ref[...]
    per = n_experts // N_EXPERT_GROUPS
    srow = [sel[e:e + 1, :] for e in range(n_experts)]
    prow = [probs[e:e + 1, :] for e in range(n_experts)]
    gscore = []
    for gi in range(N_EXPERT_GROUPS):
        a = srow[gi * per:(gi + 1) * per]
        best = None
        for u in range(per):
            for w in range(u + 1, per):
                pair = a[u] + a[w]
                best = pair if best is None else jnp.maximum(best, pair)
        gscore.append(best)
    bg = jnp.zeros_like(gscore[0], dtype=jnp.int32)
    bs = gscore[0]
    for gi in range(1, N_EXPERT_GROUPS):
        upd = gscore[gi] > bs
        bg = jnp.where(upd, gi, bg)
        bs = jnp.where(upd, gscore[gi], bs)
    cs, cp = [], []
    for u in range(per):
        su, pu = srow[u], prow[u]
        for gi in range(1, N_EXPERT_GROUPS):
            su = jnp.where(bg == gi, srow[gi * per + u], su)
            pu = jnp.where(bg == gi, prow[gi * per + u], pu)
        cs.append(su)
        cp.append(pu)
    chosen = []
    for u in range(per):
        rank = jnp.zeros_like(cs[u])
        for w in range(per):
            if w == u:
                continue
            before = (cs[w] >= cs[u]) if w < u else (cs[w] > cs[u])
            rank = rank + jnp.where(before, 1.0, 0.0)
        chosen.append(jnp.where(rank < EXPERT_TOPK, 1.0, 0.0))
    denom = jnp.zeros_like(cp[0])
    for u in range(per):
        denom = denom + chosen[u] * cp[u]
    tm = bg.shape[1]
    grows = [chosen[u] * cp[u] / denom for u in range(per)]
    gpad = jnp.concatenate(grows + [jnp.zeros((LANES - per, tm), F32)], axis=0)
    hx_ref[:, d_model:d_model + LANES] = gpad.T
    bg_ref[...] = bg
    cnt_ref[0] = jnp.concatenate(
        [jnp.broadcast_to(jnp.sum(jnp.where(bg == gi, 1.0, 0.0), axis=1, keepdims=True), (1, LANES))
         for gi in range(N_EXPERT_GROUPS)], axis=0)


def _route_kernel(bg_ref, off_ref, tri_ref, dest_ref):
    bg = bg_ref[...]
    tm = bg.shape[1]
    member = [jnp.where(bg == gi, 1.0, 0.0) for gi in range(N_EXPERT_GROUPS)]
    pad = jnp.zeros((8 - N_EXPERT_GROUPS, tm), F32)
    before = jnp.dot(jnp.concatenate(member + [pad], axis=0).astype(BF16), tri_ref[...],
                     preferred_element_type=F32)
    off = off_ref[0]
    dest = jnp.zeros((1, tm), F32)
    for gi in range(N_EXPERT_GROUPS):
        start = jnp.concatenate([off[gi:gi + 1, :]] * (tm // LANES), axis=1)
        dest = dest + member[gi] * (start + before[gi:gi + 1, :])
    dest_ref[...] = dest.astype(jnp.int32)


def _dispatch_kernel(dest_ref, hx_ref, init_ref, xs_ref, stage, sems):
    del init_ref
    i = pl.program_id(0)
    n = hx_ref.shape[0]
    slot = i % 2

    def wait_slot(s):
        def wait(r, carry):
            pltpu.make_async_copy(stage.at[s, pl.ds(0, 1)], xs_ref.at[pl.ds(0, 1)], sems.at[s]).wait()
            return carry
        lax.fori_loop(0, n, wait, 0, unroll=8)

    @pl.when(i >= 2)
    def _():
        wait_slot(slot)

    stage[slot] = hx_ref[...]

    def start(r, carry):
        pltpu.make_async_copy(stage.at[slot, pl.ds(r, 1)], xs_ref.at[pl.ds(dest_ref[0, 0, r], 1)],
                              sems.at[slot]).start()
        return carry

    lax.fori_loop(0, n, start, 0, unroll=8)

    @pl.when(i == pl.num_programs(0) - 1)
    def _():
        @pl.when(i >= 1)
        def _():
            wait_slot(1 - slot)
        wait_slot(slot)


def _moe_kernel(grp_ref, used_ref, xs_ref, wg_ref, wu_ref, wd_ref, y_ref, acc_ref):
    del grp_ref
    u = pl.program_id(1)
    d_model = y_ref.shape[1]

    @pl.when(u == 0)
    def _():
        acc_ref[...] = jnp.zeros_like(acc_ref)

    @pl.when(pl.program_id(0) < used_ref[0])
    def _():
        h = xs_ref[:, 0:d_model].astype(BF16)
        gates = xs_ref[:, d_model:d_model + LANES]
        lane = lax.broadcasted_iota(jnp.int32, gates.shape, 1)
        gcol = jnp.sum(jnp.where(lane == u, gates, 0.0), axis=-1, keepdims=True)
        hg = jnp.dot(h, wg_ref[0].astype(BF16), preferred_element_type=F32)
        hu = jnp.dot(h, wu_ref[0].astype(BF16), preferred_element_type=F32)
        a = (hg * jax.nn.sigmoid(hg)) * hu
        acc_ref[...] += _dot(gcol * a, wd_ref[0])

    @pl.when(u == pl.num_programs(1) - 1)
    def _():
        y_ref[...] = acc_ref[...]


def _combine_kernel(cur_ref, nxt_ref, y_ref, xn_ref, mod_ref, o_ref, buf, sems):
    i = pl.program_id(0)
    n = xn_ref.shape[0]

    def request(idx_ref, slot):
        def start(r, carry):
            pltpu.make_async_copy(y_ref.at[pl.ds(idx_ref[0, 0, r], 1)], buf.at[slot, pl.ds(r, 1)],
                                  sems.at[slot]).start()
            return carry
        lax.fori_loop(0, n, start, 0, unroll=8)

    @pl.when(i == 0)
    def _():
        request(cur_ref, 0)

    @pl.when(i < pl.num_programs(0) - 1)
    def _():
        request(nxt_ref, (i + 1) % 2)

    slot = i % 2

    def wait(r, carry):
        pltpu.make_async_copy(y_ref.at[pl.ds(0, 1)], buf.at[slot, pl.ds(0, 1)], sems.at[slot]).wait()
        return carry

    lax.fori_loop(0, n, wait, 0, unroll=8)
    o_ref[...] = xn_ref[...] + mod_ref[0][5:6] * buf[slot]


def _decay_tables():
    H, C, d = RET_HEADS, CHUNK, HEAD_DIM
    log_gamma = jnp.log1p(-jnp.power(2.0, -5.0 - jnp.arange(H, dtype=F32)))
    idx = jnp.arange(C, dtype=F32)
    diff = idx[:, None] - idx[None, :]
    d_in = jnp.where(diff >= 0, jnp.exp(jnp.maximum(diff, 0.0)[None] * log_gamma[:, None, None]), 0.0).astype(F32)
    d_q = jnp.exp((idx + 1.0)[:, None] * log_gamma[None]).astype(F32)
    d_k = jnp.exp((C - 1.0 - idx)[:, None] * log_gamma[None]).astype(F32)
    d_c = jnp.exp(C * log_gamma).astype(F32)
    bcast = lambda a: jnp.broadcast_to(a.T[:, :, None], (H, C, d))
    return d_in, bcast(d_q), bcast(d_k), jnp.broadcast_to(d_c[:, None, None], (H, d, d))


def _gate_expand():
    m = np.zeros((LANES, N_BRANCH * NSA_W), np.float32)
    for h in range(NSA_HEADS):
        for br in range(N_BRANCH):
            m[h * N_BRANCH + br, br * NSA_W + h * HEAD_DIM: br * NSA_W + (h + 1) * HEAD_DIM] = 1.0
    return jnp.asarray(m, BF16)


def _block_expand(seq):
    m = (np.arange(LANES)[:, None] == (np.arange(seq) // SEL_BLOCK)[None, :]).astype(np.float32)
    return jnp.asarray(m, BF16)


def _compress_weights(pe, w1, w2):
    G, d = NSA_KV_HEADS, HEAD_DIM
    hid = w1.shape[-1]
    eye = jnp.eye(G, dtype=F32)
    w1h = w1.reshape(2, CMP_STRIDE, d, hid)
    w1b = jnp.einsum('pldh,ge->plgdeh', w1h, eye).reshape(2, CMP_STRIDE, G * d, G * hid)
    w2b = jnp.einsum('hd,ge->ghed', w2, eye).reshape(G * hid, G * d)
    peb = jnp.broadcast_to(pe.reshape(2, CMP_STRIDE, 1, d), (2, CMP_STRIDE, G, d)).reshape(2, CMP_STRIDE, G * d)
    return peb, w1b.astype(BF16), w2b.astype(BF16)


def kernel(x, c, positions, ada_w, ada_b, norm_mix_g, norm_ffn_g, w_in, w_out, ret_norm_g, gmlp_ln_g, gmlp_ws,
           gmlp_b, nsa_q_norm_g, nsa_k_norm_g, cmp_pe_k, cmp_pe_v, cmp_w1_k, cmp_w2_k, cmp_w1_v, cmp_w2_v,
           router_w, router_b, moe_w_gate, moe_w_up, moe_w_down):
    B, S, D = x.shape
    L = ada_w.shape[0]
    E = router_w.shape[1]
    DE = moe_w_gate.shape[-1]
    T = B * S
    n_chunk = S // CHUNK
    n_cmp = S // CMP_STRIDE
    n_sel = S // SEL_BLOCK
    top_n = min(SEL_TOPK, n_sel)
    n_qt = S // Q_TILE
    tm = min(512, S)
    tm_moe = min(1024, S)
    assert S % KEY_CHUNK == 0 and S >= WINDOW + Q_TILE and D == RET_W + GMLP_W + NSA_W
    assert w_in.shape[-1] == N_IN and T % tm_moe == 0 and n_sel <= LANES and NSA_KV_HEADS == 2

    half = HEAD_DIM // 2
    inv_freq = jnp.power(ROPE_THETA, -jnp.arange(half, dtype=F32) / half)
    invf = jnp.tile(inv_freq, LANES // half)[None, :]
    cos_t, sin_t = pl.pallas_call(
        _rope_table_kernel, name="rope_tables", grid=(B,),
        in_specs=[pl.BlockSpec((1, S, 1), lambda b: (b, 0, 0)), pl.BlockSpec((1, LANES), lambda b: (0, 0))],
        out_specs=[pl.BlockSpec((1, S, LANES), lambda b: (b, 0, 0))] * 2,
        out_shape=[jax.ShapeDtypeStruct((B, S, LANES), F32)] * 2,
        compiler_params=_cparams("parallel"),
    )(positions[:, :, None], invf)
    pad_c = lambda a: jnp.pad(a[:, CMP_LEN - 1::CMP_STRIDE], ((0, 0), (0, 1), (0, 0)))
    cos_c, sin_c = pad_c(cos_t), pad_c(sin_t)
    cos_f, sin_f = cos_t.reshape(T, LANES), sin_t.reshape(T, LANES)

    mod = pl.pallas_call(
        _ada_kernel, name="ada_mod", grid=(L, 6),
        in_specs=[pl.BlockSpec((B, D), lambda l, j: (0, 0)),
                  pl.BlockSpec((1, D, D), lambda l, j: (l, 0, j)),
                  pl.BlockSpec((1, 1, 1, D), lambda l, j: (l, j, 0, 0))],
        out_specs=pl.BlockSpec((1, 1, B, D), lambda l, j: (l, j, 0, 0)),
        out_shape=jax.ShapeDtypeStruct((L, 6, B, D), F32),
        compiler_params=_cparams("parallel", "parallel"),
    )(c, ada_w, ada_b.reshape(L, 6, 1, D))
    mod = mod.transpose(0, 2, 1, 3)

    d_in, d_q, d_k, d_c = _decay_tables()
    gexp = _gate_expand()
    expand = _block_expand(S)
    rw_hi = router_w.T.astype(BF16)
    rw_lo = (router_w.T - rw_hi.astype(F32)).astype(BF16)
    rwt = jnp.stack([rw_hi, rw_lo])
    rb = router_b.reshape(E, 1)
    tri = jnp.asarray(np.triu(np.ones((tm, tm), np.float32), k=1), BF16)
    x2 = x.reshape(T, D)
    tiles_per_batch = S // tm
    row_spec = lambda w: pl.BlockSpec((tm, w), lambda i: (i, 0))
    full = lambda shape: pl.BlockSpec(shape, lambda *_: (0,) * len(shape))

    for l in range(L):
        mod_l = mod[l]
        w_in_p = jnp.pad(w_in[l], ((0, 0), (0, N_IN_PAD - N_IN))).astype(BF16)
        raw_widths = tuple(hi - lo for lo, hi in _IN_RAW)
        mod_spec = pl.BlockSpec((1, 6, D), lambda i: (i // tiles_per_batch, 0, 0))
        kt_out = pl.BlockSpec((1, KV_W, tm), lambda i: (i // tiles_per_batch, 0, i % tiles_per_batch))
        kgain = jnp.tile(nsa_k_norm_g[l], (1, NSA_KV_HEADS))
        zr, zg, zkc, zvc, zgl, q_r, ks_t, vs_b, kw_t, vw_b = pl.pallas_call(
            _in_proj_kernel, name="in_proj", grid=(T // tm,),
            in_specs=[row_spec(D), mod_spec, full((1, D)), full((D, N_IN_PAD)), row_spec(LANES), row_spec(LANES),
                      full((1, NSA_W)), full((N_BRANCH, KV_W))],
            out_specs=[row_spec(w) for w in raw_widths]
            + [row_spec(NSA_W), kt_out, row_spec(KV_W), kt_out, row_spec(KV_W)],
            out_shape=[jax.ShapeDtypeStruct((T, w), F32) for w in raw_widths]
            + [jax.ShapeDtypeStruct((T, NSA_W), BF16), jax.ShapeDtypeStruct((B, KV_W, S), BF16),
               jax.ShapeDtypeStruct((T, KV_W), BF16), jax.ShapeDtypeStruct((B, KV_W, S), BF16),
               jax.ShapeDtypeStruct((T, KV_W), BF16)],
            compiler_params=_cparams("parallel"),
        )(x2, mod_l, norm_mix_g[l][None, :], w_in_p, cos_f, sin_f,
          jnp.tile(nsa_q_norm_g[l], NSA_HEADS)[None, :], kgain)

        crow = lambda w: pl.BlockSpec((MIX_BATCH, CHUNK, w), lambda b, cc: (b, cc, 0))
        y_ret, y_gm = pl.pallas_call(
            _mixer_kernel, name="ret_gmlp", grid=(B // MIX_BATCH, n_chunk),
            in_specs=[crow(4 * RET_W), crow(2 * GMLP_W), crow(LANES), crow(LANES),
                      full((RET_HEADS, CHUNK, CHUNK)), full((RET_HEADS, CHUNK, HEAD_DIM)),
                      full((RET_HEADS, CHUNK, HEAD_DIM)), full((RET_HEADS, HEAD_DIM, HEAD_DIM)),
                      full((RET_HEADS, 1, HEAD_DIM)), full((1, GMLP_W)),
                      full((GMLP_GROUPS, CHUNK, CHUNK)), full((GMLP_GROUPS, CHUNK, HEAD_DIM))],
            out_specs=[crow(RET_W), crow(GMLP_W)],
            out_shape=[jax.ShapeDtypeStruct((B, S, RET_W), BF16), jax.ShapeDtypeStruct((B, S, GMLP_W), BF16)],
            scratch_shapes=[pltpu.VMEM((MIX_BATCH, RET_HEADS, HEAD_DIM, HEAD_DIM), F32)],
            compiler_params=_cparams("parallel", "arbitrary"),
        )(zr.reshape(B, S, 4 * RET_W), zg.reshape(B, S, 2 * GMLP_W), cos_t, sin_t, d_in, d_q, d_k, d_c,
          ret_norm_g[l][:, None, :], gmlp_ln_g[l][None, :], gmlp_ws[l],
          jnp.broadcast_to(gmlp_b[l][:, :, None], (GMLP_GROUPS, CHUNK, HEAD_DIM)))
        y_ret, y_gm = y_ret.reshape(T, RET_W), y_gm.reshape(T, GMLP_W)

        pek, w1k, w2k = _compress_weights(cmp_pe_k[l], cmp_w1_k[l], cmp_w2_k[l])
        pev, w1v, w2v = _compress_weights(cmp_pe_v[l], cmp_w1_v[l], cmp_w2_v[l])
        cblk = pl.BlockSpec((S, KV_W), lambda b: (b, 0))
        ghid = w1k.shape[-1]
        ctab = pl.BlockSpec((1, n_cmp, LANES), lambda b: (b, 0, 0))
        kc, vc = pl.pallas_call(
            _compress_kernel, name="nsa_compress", grid=(B,),
            in_specs=[cblk, cblk, full((2, CMP_STRIDE, KV_W)), full((2, CMP_STRIDE, KV_W)),
                      full((2, CMP_STRIDE, KV_W, ghid)), full((ghid, KV_W)),
                      full((2, CMP_STRIDE, KV_W, ghid)), full((ghid, KV_W)),
                      full((N_BRANCH, KV_W)), ctab, ctab],
            out_specs=[ctab, ctab],
            out_shape=[jax.ShapeDtypeStruct((B, n_cmp, KV_W), F32)] * 2,
            compiler_params=_cparams("parallel"),
        )(zkc, zvc, pek, pev, w1k, w2k, w1v, w2v,
          kgain, cos_c, sin_c)

        qrow = lambda w: pl.BlockSpec((Q_TILE, w), lambda b, i: (b * n_qt + i, 0))
        kt_in = pl.BlockSpec((1, KV_W, S), lambda b, i: (b, 0, 0))
        v_in = pl.BlockSpec((S, KV_W), lambda b, i: (b, 0))
        ctab2 = pl.BlockSpec((1, n_cmp, KV_W), lambda b, i: (b, 0, 0))
        y_nsa = pl.pallas_call(
            functools.partial(_nsa_kernel, n_sel=n_sel, top_n=top_n), name="nsa_attn", grid=(B, n_qt),
            in_specs=[qrow(NSA_W), qrow(LANES), ctab2, ctab2, kt_in, v_in, kt_in, v_in,
                      full((LANES, N_BRANCH * NSA_W)), full((LANES, S))],
            out_specs=qrow(NSA_W),
            out_shape=jax.ShapeDtypeStruct((T, NSA_W), BF16),
            scratch_shapes=[
                pltpu.VMEM((NSA_KV_HEADS, NSA_GROUP * Q_TILE, WINDOW + Q_TILE), F32),
                pltpu.VMEM((NSA_KV_HEADS, NSA_GROUP * Q_TILE, WINDOW + Q_TILE), BF16),
                pltpu.VMEM((NSA_KV_HEADS, NSA_GROUP * Q_TILE, LANES), F32),
                pltpu.VMEM((NSA_KV_HEADS, NSA_GROUP * Q_TILE, KV_W), F32)],
            compiler_params=_cparams("parallel", "parallel"),
        )(q_r, zgl, kc, vc, ks_t, vs_b, kw_t, vw_b, gexp, expand)

        n_tok_tiles = T // tm
        DX = D + LANES
        x_new, hx, bgrp, cnt = pl.pallas_call(
            functools.partial(_out_proj_kernel, n_experts=E), name="out_proj_router", grid=(n_tok_tiles,),
            in_specs=[row_spec(RET_W), row_spec(GMLP_W), row_spec(NSA_W), row_spec(D),
                      pl.BlockSpec((1, 6, D), lambda i: (i // tiles_per_batch, 0, 0)),
                      full((D, D)), full((1, D)), full((2, E, D)), full((E, 1))],
            out_specs=[row_spec(D), row_spec(DX), pl.BlockSpec((1, tm), lambda i: (0, i)),
                       pl.BlockSpec((1, N_EXPERT_GROUPS, LANES), lambda i: (i, 0, 0))],
            out_shape=[jax.ShapeDtypeStruct((T, D), F32), jax.ShapeDtypeStruct((T, DX), F32),
                       jax.ShapeDtypeStruct((1, T), jnp.int32),
                       jax.ShapeDtypeStruct((n_tok_tiles, N_EXPERT_GROUPS, LANES), F32)],
            compiler_params=_cparams("parallel"),
        )(y_ret, y_gm, y_nsa, x2, mod_l, w_out[l].astype(BF16), norm_ffn_g[l][None, :], rwt, rb)

        tile_cnt = cnt[:, :, 0].astype(jnp.int32)
        seg = ((jnp.sum(tile_cnt, axis=0) + tm_moe - 1) // tm_moe) * tm_moe
        earlier = lambda n: jnp.arange(n)[:, None] > jnp.arange(n)[None, :]
        seg_start = jnp.sum(jnp.where(earlier(N_EXPERT_GROUPS), seg[None, :], 0), axis=1)
        tile_off = seg_start[None, :] + jnp.sum(
            jnp.where(earlier(n_tok_tiles)[:, :, None], tile_cnt[None, :, :], 0), axis=1)
        n_row_tiles = T // tm_moe + N_EXPERT_GROUPS
        row0 = jnp.arange(n_row_tiles) * tm_moe
        tile_grp = jnp.minimum(jnp.sum(((seg_start + seg)[None, :] <= row0[:, None]).astype(jnp.int32), axis=1),
                               N_EXPERT_GROUPS - 1)
        dest = pl.pallas_call(
            _route_kernel, name="moe_route", grid=(n_tok_tiles,),
            in_specs=[pl.BlockSpec((1, tm), lambda i: (0, i)),
                      pl.BlockSpec((1, N_EXPERT_GROUPS, LANES), lambda i: (i, 0, 0)), full((tm, tm))],
            out_specs=pl.BlockSpec((1, tm), lambda i: (0, i)),
            out_shape=jax.ShapeDtypeStruct((1, T), jnp.int32),
            compiler_params=_cparams("parallel"),
        )(bgrp, jnp.broadcast_to(tile_off.astype(F32)[:, :, None], (n_tok_tiles, N_EXPERT_GROUPS, LANES)), tri)
        dest3 = dest.reshape(n_tok_tiles, 1, tm)
        dest_spec = pl.BlockSpec((1, 1, tm), lambda i: (i, 0, 0), memory_space=pltpu.SMEM)
        any_spec = pl.BlockSpec(memory_space=pl.ANY)

        n_rows = n_row_tiles * tm_moe
        xs = pl.pallas_call(
            _dispatch_kernel, name="moe_dispatch", grid=(n_tok_tiles,),
            in_specs=[dest_spec, row_spec(DX), any_spec],
            out_specs=any_spec,
            out_shape=jax.ShapeDtypeStruct((n_rows, DX), F32),
            scratch_shapes=[pltpu.VMEM((2, tm, DX), F32), pltpu.SemaphoreType.DMA((2,))],
            input_output_aliases={2: 0},
            compiler_params=_cparams("arbitrary"),
        )(dest3, hx, jnp.zeros((n_rows, DX), F32) if l == 0 else xs)

        per = E // N_EXPERT_GROUPS
        tiles_used = (jnp.sum(seg) // tm_moe).astype(jnp.int32).reshape(1)

        def w_idx(k, u, grp, used):
            return l, grp[k] * per + jnp.where(k < used[0], u, per - 1), 0, 0

        ys = pl.pallas_call(
            _moe_kernel, name="moe",
            grid_spec=pltpu.PrefetchScalarGridSpec(
                num_scalar_prefetch=2, grid=(n_row_tiles, per),
                in_specs=[pl.BlockSpec((tm_moe, DX), lambda k, u, grp, used: (k, 0)),
                          pl.BlockSpec((None, 1, D, DE), w_idx), pl.BlockSpec((None, 1, D, DE), w_idx),
                          pl.BlockSpec((None, 1, DE, D), w_idx)],
                out_specs=pl.BlockSpec((tm_moe, D), lambda k, u, grp, used: (k, 0)),
                scratch_shapes=[pltpu.VMEM((tm_moe, D), F32)]),
            out_shape=jax.ShapeDtypeStruct((n_rows, D), F32),
            compiler_params=_cparams("parallel", "arbitrary"),
        )(tile_grp, tiles_used, xs, moe_w_gate, moe_w_up, moe_w_down)

        dest_next = pl.BlockSpec((1, 1, tm), lambda i: (jnp.minimum(i + 1, n_tok_tiles - 1), 0, 0),
                                 memory_space=pltpu.SMEM)
        x2 = pl.pallas_call(
            _combine_kernel, name="moe_combine", grid=(n_tok_tiles,),
            in_specs=[dest_spec, dest_next, any_spec, row_spec(D),
                      pl.BlockSpec((1, 6, D), lambda i: (i // tiles_per_batch, 0, 0))],
            out_specs=row_spec(D),
            out_shape=jax.ShapeDtypeStruct((T, D), F32),
            scratch_shapes=[pltpu.VMEM((2, tm, D), F32), pltpu.SemaphoreType.DMA((2,))],
            compiler_params=_cparams("arbitrary"),
        )(dest3, dest3, ys, x_new, mod_l)

    return x2.reshape(B, S, D)
```

```python
import functools

import numpy as np
import jax
import jax.numpy as jnp
from jax import lax
from jax.experimental import pallas as pl
from jax.experimental.pallas import tpu as pltpu

F32 = jnp.float32
BF16 = jnp.bfloat16

HEAD_DIM = 64
LANES = 128
RET_HEADS = 4
GMLP_GROUPS = 4
NSA_HEADS = 8
NSA_KV_HEADS = 2
NSA_GROUP = NSA_HEADS // NSA_KV_HEADS
N_BRANCH = 3
CHUNK = 128
CMP_LEN = 32
CMP_STRIDE = 16
SEL_BLOCK = 64
SEL_TOPK = 8
WINDOW = 512
ROPE_THETA = 10000.0
N_EXPERT_GROUPS = 4
EXPERT_TOPK = 2
NORM_EPS = 1e-6
NEG_INF = -1e30
FORCE_SCORE = 1e4

RET_W = RET_HEADS * HEAD_DIM
GMLP_W = GMLP_GROUPS * HEAD_DIM
NSA_W = NSA_HEADS * HEAD_DIM
KV_W = NSA_KV_HEADS * HEAD_DIM
GATE_W = NSA_HEADS * N_BRANCH
OFF_RET = 0
OFF_GMLP = 4 * RET_W
OFF_Q = OFF_GMLP + 2 * GMLP_W
OFF_KV = OFF_Q + NSA_W
OFF_GL = OFF_KV + 6 * KV_W
N_IN = OFF_GL + GATE_W
N_IN_PAD = OFF_GL + LANES

Q_TILE = 256
KEY_CHUNK = 512
MIX_BATCH = 2
ROW_BLOCK = 32
ROW_UNROLL = True
VMEM_LIMIT = 48 * 1024 * 1024


def _cparams(*sem):
    return pltpu.CompilerParams(dimension_semantics=sem, vmem_limit_bytes=VMEM_LIMIT)


def _dot(a, b):
    return jnp.dot(a.astype(BF16), b.astype(BF16), preferred_element_type=F32)


def _dot_nt(a, b):
    return lax.dot_general(a.astype(BF16), b.astype(BF16), (((1,), (1,)), ((), ())),
                           preferred_element_type=F32)


def _split(a):
    hi = a.astype(BF16)
    lo = (a - hi.astype(F32)).astype(BF16)
    return hi, lo


def _dot_split_lhs(a, b):
    hi, lo = _split(a)
    return (jnp.dot(hi, b, preferred_element_type=F32) + jnp.dot(lo, b, preferred_element_type=F32))


def _dot_split_rhs(a, b):
    hi, lo = _split(b)
    return (jnp.dot(a, hi, preferred_element_type=F32) + jnp.dot(a, lo, preferred_element_type=F32))


def _head_ones(width=LANES):
    r = lax.broadcasted_iota(jnp.int32, (width, width), 0) // HEAD_DIM
    c = lax.broadcasted_iota(jnp.int32, (width, width), 1) // HEAD_DIM
    return jnp.where(r == c, 1.0, 0.0).astype(BF16)


def _rotate_half(y):
    lane = lax.broadcasted_iota(jnp.int32, y.shape, 1)
    first = (lane & (HEAD_DIM - 1)) < (HEAD_DIM // 2)
    return jnp.where(first, -pltpu.roll(y, LANES - HEAD_DIM // 2, 1), pltpu.roll(y, HEAD_DIM // 2, 1))


def _rope_slab(y, cos, sin):
    return y * cos + _rotate_half(y) * sin


def _rope(x, cos, sin):
    return jnp.concatenate(
        [_rope_slab(x[:, k * LANES:(k + 1) * LANES], cos, sin) for k in range(x.shape[1] // LANES)], axis=1)


def _head_rms_rope(x, gain, cos, sin, ones):
    outs = []
    for k in range(x.shape[1] // LANES):
        xs = x[:, k * LANES:(k + 1) * LANES]
        ssq = _dot_split_lhs(xs * xs, ones)
        y = xs * lax.rsqrt(ssq * (1.0 / HEAD_DIM) + NORM_EPS) * gain[:, k * LANES:(k + 1) * LANES]
        outs.append(_rope_slab(y, cos, sin))
    return outs[0] if len(outs) == 1 else jnp.concatenate(outs, axis=1)


def _rms_mod(x, gain, scale, shift):
    y = x * lax.rsqrt(jnp.mean(x * x, axis=-1, keepdims=True) + NORM_EPS)
    return (y * gain) * (1.0 + scale) + shift


def _softmax_rows(s):
    e = jnp.exp(s - jnp.max(s, axis=-1, keepdims=True))
    return e / jnp.sum(e, axis=-1, keepdims=True)


def _mask_bias(keep):
    return jnp.where(keep, 0.0, NEG_INF)


def _rope_table_kernel(pos_ref, invf_ref, cos_ref, sin_ref):
    ang = pos_ref[0].astype(F32) * invf_ref[...]
    cos_ref[0] = jnp.cos(ang)
    sin_ref[0] = jnp.sin(ang)


def _ada_kernel(c_ref, w_ref, b_ref, o_ref):
    c = c_ref[...]
    a = c * jax.nn.sigmoid(c)
    o_ref[0, 0] = jnp.dot(a, w_ref[0], preferred_element_type=F32,
                          precision=lax.Precision.HIGHEST) + b_ref[0, 0]


_IN_RAW = ((OFF_RET, OFF_GMLP), (OFF_GMLP, OFF_Q), (OFF_KV, OFF_KV + KV_W), (OFF_KV + KV_W, OFF_KV + 2 * KV_W),
           (OFF_GL, N_IN_PAD))


def _in_proj_kernel(x_ref, mod_ref, g_ref, w_ref, cos_ref, sin_ref, qg_ref, kg_ref, *out_refs):
    *raw_refs, q_out, kst_out, vs_out, kwt_out, vw_out = out_refs
    mod = mod_ref[0]
    h = _rms_mod(x_ref[...], g_ref[...], mod[1:2], mod[0:1]).astype(BF16)
    proj = lambda lo, hi: jnp.dot(h, w_ref[:, lo:hi], preferred_element_type=F32)
    tail = proj(OFF_KV, N_IN_PAD)
    cols = lambda lo, hi: tail[:, lo - OFF_KV:hi - OFF_KV] if lo >= OFF_KV else proj(lo, hi)
    for ref, (lo, hi) in zip(raw_refs, _IN_RAW):
        ref[...] = cols(lo, hi)

    ones = _head_ones()
    cos, sin = cos_ref[...], sin_ref[...]
    q_out[...] = (_head_rms_rope(proj(OFF_Q, OFF_KV), qg_ref[...], cos, sin, ones) * (HEAD_DIM ** -0.5)).astype(BF16)
    sel_kv = OFF_KV + 2 * KV_W
    kst_out[0] = _head_rms_rope(cols(sel_kv, sel_kv + KV_W), kg_ref[1:2], cos, sin, ones).T.astype(BF16)
    vs_out[...] = cols(sel_kv + KV_W, sel_kv + 2 * KV_W).astype(BF16)
    kwt_out[0] = _head_rms_rope(cols(sel_kv + 2 * KV_W, sel_kv + 3 * KV_W), kg_ref[2:3], cos, sin, ones).T.astype(BF16)
    vw_out[...] = cols(sel_kv + 3 * KV_W, OFF_GL).astype(BF16)


def _mixer_kernel(zr_ref, zg_ref, cos_ref, sin_ref, din_ref, dq_ref, dk_ref, dc_ref, rg_ref,
                  lng_ref, ws_ref, bs_ref, yret_ref, ygm_ref, state_ref):
    @pl.when(pl.program_id(1) == 0)
    def _():
        state_ref[...] = jnp.zeros_like(state_ref)

    row = lax.broadcasted_iota(jnp.int32, (CHUNK, CHUNK), 0)
    col = lax.broadcasted_iota(jnp.int32, (CHUNK, CHUNK), 1)
    ws = [jnp.where(row >= col, ws_ref[gi], 0.0).astype(BF16) for gi in range(GMLP_GROUPS)]

    for bb in range(zr_ref.shape[0]):
        cos, sin = cos_ref[bb], sin_ref[bb]
        q = _rope(zr_ref[bb, :, 0:RET_W], cos, sin)
        k = _rope(zr_ref[bb, :, RET_W:2 * RET_W], cos, sin) * (HEAD_DIM ** -0.5)
        v = zr_ref[bb, :, 2 * RET_W:3 * RET_W]
        g = zr_ref[bb, :, 3 * RET_W:4 * RET_W]
        outs = []
        for h in range(RET_HEADS):
            sl = slice(h * HEAD_DIM, (h + 1) * HEAD_DIM)
            qh, kh, vh = q[:, sl], k[:, sl], v[:, sl]
            st = state_ref[bb, h]
            inner = _dot_nt(qh, kh) * din_ref[h]
            o = _dot(inner, vh) + _dot(qh, st) * dq_ref[h]
            state_ref[bb, h] = st * dc_ref[h] + _dot((kh * dk_ref[h]).T, vh)
            o = o * lax.rsqrt(jnp.mean(o * o, axis=-1, keepdims=True) + NORM_EPS) * rg_ref[h]
            outs.append(o)
        yret_ref[bb] = (jnp.concatenate(outs, axis=1) * (g * jax.nn.sigmoid(g))).astype(BF16)

        u = jax.nn.gelu(zg_ref[bb, :, 0:GMLP_W])
        vv = jax.nn.gelu(zg_ref[bb, :, GMLP_W:2 * GMLP_W])
        mu = jnp.mean(vv, axis=-1, keepdims=True)
        var = jnp.mean(jnp.square(vv - mu), axis=-1, keepdims=True)
        vn = (vv - mu) * lax.rsqrt(var + NORM_EPS) * lng_ref[...]
        ss = [_dot(ws[gi], vn[:, gi * HEAD_DIM:(gi + 1) * HEAD_DIM]) + bs_ref[gi] for gi in range(GMLP_GROUPS)]
        ygm_ref[bb] = (u * jnp.concatenate(ss, axis=1)).astype(BF16)


def _compress_kernel(xk_ref, xv_ref, pek_ref, pev_ref, w1k_ref, w2k_ref, w1v_ref, w2v_ref, kg_ref,
                     cos_ref, sin_ref, kc_ref, vc_ref):
    n_rows = kc_ref.shape[1]

    def comp(x_ref, pe_ref, w1_ref, w2_ref):
        a = b = None
        for tok in range(CMP_STRIDE):
            x = x_ref[pl.ds(tok, n_rows, stride=CMP_STRIDE), :]
            ta = _dot(x + pe_ref[0, tok:tok + 1, :], w1_ref[0, tok])
            tb = _dot(x + pe_ref[1, tok:tok + 1, :], w1_ref[1, tok])
            a, b = (ta, tb) if a is None else (a + ta, b + tb)
        h = a + pltpu.roll(b, n_rows - 1, 0)
        return _dot(jax.nn.gelu(h), w2_ref[...])

    kc_ref[0] = _head_rms_rope(comp(xk_ref, pek_ref, w1k_ref, w2k_ref), kg_ref[0:1],
                               cos_ref[0], sin_ref[0], _head_ones())
    vc_ref[0] = comp(xv_ref, pev_ref, w1v_ref, w2v_ref)


def _nsa_kernel(q_ref, gl_ref, kc_ref, vc_ref, kst_ref, vs_ref, kwt_ref, vw_ref, gexp_ref, expand_ref, o_ref,
                s_scr, p_scr, m_scr, acc_scr, *, n_sel, top_n):
    i = pl.program_id(1)
    qt = q_ref.shape[0]
    n_cmp = kc_ref.shape[1]
    t0 = i * qt
    n_chunks = (t0 + qt + KEY_CHUNK - 1) // KEY_CHUNK
    wk = WINDOW + qt
    wstart = pl.multiple_of(jnp.maximum(t0 - WINDOW, 0), LANES)

    t_r = t0 + lax.broadcasted_iota(jnp.int32, (qt, n_cmp), 0)
    n_c = lax.broadcasted_iota(jnp.int32, (qt, n_cmp), 1)
    keep_c = jnp.where(n_c * CMP_STRIDE + (CMP_LEN - 1) <= t_r, 1.0, 0.0)
    bias_c = (keep_c - 1.0) * (-NEG_INF)
    t_l = t0 + lax.broadcasted_iota(jnp.int32, (n_cmp, qt), 1)
    n_s = lax.broadcasted_iota(jnp.int32, (n_cmp, qt), 0)
    keep_ct = jnp.where(n_s * CMP_STRIDE + (CMP_LEN - 1) <= t_l, 1.0, 0.0)
    keep_ct4 = jnp.concatenate([keep_ct] * NSA_GROUP, axis=1)
    bias_ct4 = (keep_ct4 - 1.0) * (-NEG_INF)
    jo = lax.broadcasted_iota(jnp.int32, (n_sel, n_cmp), 0) * SEL_BLOCK
    no = lax.broadcasted_iota(jnp.int32, (n_sel, n_cmp), 1) * CMP_STRIDE
    overlap = jnp.where((no < jo + SEL_BLOCK) & (no + CMP_LEN > jo), 1.0, 0.0).astype(BF16)
    j = lax.broadcasted_iota(jnp.int32, (n_sel, qt), 0)
    tq = t0 + lax.broadcasted_iota(jnp.int32, (n_sel, qt), 1)
    cur = tq // SEL_BLOCK
    forced = (j == 0) | (j == cur) | (j == cur - 1)
    allowed = j * SEL_BLOCK <= tq
    t_w = t0 + lax.broadcasted_iota(jnp.int32, (qt, wk), 0)
    s_w = wstart + lax.broadcasted_iota(jnp.int32, (qt, wk), 1)
    bias_w = _mask_bias((s_w <= t_w) & (t_w - s_w < WINDOW))
    t_k = t0 + lax.broadcasted_iota(jnp.int32, (qt, KEY_CHUNK), 0)
    s_k = lax.broadcasted_iota(jnp.int32, (qt, KEY_CHUNK), 1)

    rep = lambda a: jnp.concatenate([a] * NSA_GROUP, axis=0)
    keep_c4, bias_c4 = rep(keep_c), rep(bias_c)
    unstack = lambda o, gs: [o[r * qt:(r + 1) * qt, gs] for r in range(NSA_GROUP)]
    rows = NSA_GROUP * qt
    lane_v = lax.broadcasted_iota(jnp.int32, (1, KV_W), 1)

    def attend(g, qs, kt, vv, bias, first):
        width = bias.shape[1]
        own = (lane_v >= g * HEAD_DIM) & (lane_v < (g + 1) * HEAD_DIM)
        vv1 = jnp.where(own, vv, jnp.ones_like(vv))
        s = jnp.dot(qs, kt, preferred_element_type=F32) + rep(bias)
        s_scr[g, :, 0:width] = s
        mx = jnp.max(s, axis=-1, keepdims=True)
        if first:
            m_scr[g] = jnp.broadcast_to(mx, (rows, LANES))
        else:
            m_old = m_scr[g]
            m_new = jnp.maximum(m_old, mx)
            alpha = jnp.exp(m_old - m_new)
            m_scr[g] = m_new

        def exp_pass(b, carry):
            r0 = pl.multiple_of(b * ROW_BLOCK, ROW_BLOCK)
            m_new = m_scr[g, pl.ds(r0, ROW_BLOCK), :]
            p = jnp.exp(s_scr[g, pl.ds(r0, ROW_BLOCK), 0:width]
                        - jnp.concatenate([m_new] * (width // LANES), axis=1))
            p_scr[g, pl.ds(r0, ROW_BLOCK), 0:width] = p.astype(BF16)
            return carry

        lax.fori_loop(0, rows // ROW_BLOCK, exp_pass, 0, unroll=ROW_UNROLL)
        pv = jnp.dot(p_scr[g, :, 0:width], vv1, preferred_element_type=F32)
        acc_scr[g] = pv if first else alpha * acc_scr[g] + pv

    def finish(g):
        acc = acc_scr[g]
        num, den = acc[:, g * HEAD_DIM:(g + 1) * HEAD_DIM], acc[:, (1 - g) * HEAD_DIM:(2 - g) * HEAD_DIM]
        return unstack(num / den, slice(None))

    o_cmp, o_win, qs_g, sel_g = [], [], [], []
    for g in range(NSA_KV_HEADS):
        gs = slice(g * HEAD_DIM, (g + 1) * HEAD_DIM)
        qs = jnp.concatenate([q_ref[:, (g * NSA_GROUP + r) * HEAD_DIM:(g * NSA_GROUP + r + 1) * HEAD_DIM]
                              for r in range(NSA_GROUP)], axis=0)
        qs_g.append(qs)
        kc = kc_ref[0][:, gs].astype(BF16)
        vc = vc_ref[0].astype(BF16)

        p = _softmax_rows(_dot_nt(qs, kc) + bias_c4) * keep_c4
        o_cmp += unstack(_dot(p, vc), gs)

        st = _dot_nt(kc, qs) + bias_ct4
        e = jnp.exp(st - jnp.max(st, axis=0, keepdims=True))
        pt = e / jnp.sum(e, axis=0, keepdims=True) * keep_ct4
        psum = pt[:, 0:qt]
        for r in range(1, NSA_GROUP):
            psum = psum + pt[:, r * qt:(r + 1) * qt]
        imp = _dot_split_rhs(overlap, psum)
        score = jnp.where(forced, FORCE_SCORE, jnp.where(allowed, imp, -1.0))
        rank = jnp.zeros((n_sel, qt), F32)
        for jp in range(n_sel):
            other = score[jp:jp + 1, :]
            tie = jnp.where(j > jp, 1.0, 0.0)
            rank = rank + jnp.where(other > score, 1.0, jnp.where(other == score, tie, 0.0))
        sel_t = jnp.where(rank < top_n, 1.0, 0.0)
        if n_sel < LANES:
            sel_t = jnp.concatenate([sel_t, jnp.zeros((LANES - n_sel, qt), F32)], axis=0)
        sel_g.append(sel_t.T.astype(BF16))

        attend(g, qs, kwt_ref[0, gs, pl.ds(wstart, wk)], vw_ref[pl.ds(wstart, wk), :], bias_w, True)
        o_win += finish(g)

    def chunk(c, first):
        k0 = pl.multiple_of(c * KEY_CHUNK, KEY_CHUNK)
        vv = vs_ref[pl.ds(k0, KEY_CHUNK), :]
        ex = expand_ref[:, pl.ds(k0, KEY_CHUNK)]
        causal = k0 + s_k <= t_k
        for g in range(NSA_KV_HEADS):
            chosen = jnp.dot(sel_g[g], ex, preferred_element_type=F32)
            attend(g, qs_g[g], kst_ref[0, g * HEAD_DIM:(g + 1) * HEAD_DIM, pl.ds(k0, KEY_CHUNK)], vv,
                   _mask_bias((chosen > 0.5) & causal), first)

    chunk(0, True)

    def body(c, carry):
        chunk(c, False)
        return carry

    lax.fori_loop(1, n_chunks, body, 0)
    o_sel = finish(0) + finish(1)

    gx = _dot_split_lhs(jax.nn.sigmoid(gl_ref[...]), gexp_ref[...])
    cat = lambda parts: jnp.concatenate(parts, axis=1)
    o_ref[...] = (gx[:, 0:NSA_W] * cat(o_cmp) + gx[:, NSA_W:2 * NSA_W] * cat(o_sel)
                  + gx[:, 2 * NSA_W:3 * NSA_W] * cat(o_win)).astype(BF16)


def _out_proj_kernel(yret_ref, ygm_ref, ynsa_ref, x_ref, mod_ref, wout_ref,
                     g2_ref, rwt_ref, rb_ref, xn_ref, hx_ref, bg_ref, cnt_ref, *, n_experts):
    mod = mod_ref[0]
    d_model = x_ref.shape[1]
    y = (jnp.dot(yret_ref[...], wout_ref[0:RET_W], preferred_element_type=F32)
         + jnp.dot(ygm_ref[...], wout_ref[RET_W:RET_W + GMLP_W], preferred_element_type=F32)
         + jnp.dot(ynsa_ref[...], wout_ref[RET_W + GMLP_W:], preferred_element_type=F32))
    xn = x_ref[...] + mod[2:3] * y
    xn_ref[...] = xn
    h2 = _rms_mod(xn, g2_ref[...], mod[4:5], mod[3:4])
    hx_ref[:, 0:d_model] = h2

    hh, hl = _split(h2)
    wh, wl = rwt_ref[0], rwt_ref[1]
    nt = lambda a, b: lax.dot_general(a, b, (((1,), (1,)), ((), ())), preferred_element_type=F32)
    logits = nt(wh, hh) + nt(wh, hl) + nt(wl, hh)
    ex = jnp.exp(logits - jnp.max(logits, axis=0, keepdims=True))
    probs = ex / jnp.sum(ex, axis=0, keepdims=True)
    sel = probs + rb_ref[...]
    per = n_experts // N_EXPERT_GROUPS
    srow = [sel[e:e + 1, :] for e in range(n_experts)]
    prow = [probs[e:e + 1, :] for e in range(n_experts)]
    gscore = []
    for gi in range(N_EXPERT_GROUPS):
        a = srow[gi * per:(gi + 1) * per]
        best = None
        for u in range(per):
            for w in range(u + 1, per):
                pair = a[u] + a[w]
                best = pair if best is None else jnp.maximum(best, pair)
        gscore.append(best)
    bg = jnp.zeros_like(gscore[0], dtype=jnp.int32)
    bs = gscore[0]
    for gi in range(1, N_EXPERT_GROUPS):
        upd = gscore[gi] > bs
        bg = jnp.where(upd, gi, bg)
        bs = jnp.where(upd, gscore[gi], bs)
    cs, cp = [], []
    for u in range(per):
        su, pu = srow[u], prow[u]
        for gi in range(1, N_EXPERT_GROUPS):
            su = jnp.where(bg == gi, srow[gi * per + u], su)
            pu = jnp.where(bg == gi, prow[gi * per + u], pu)
        cs.append(su)
        cp.append(pu)
    chosen = []
    for u in range(per):
        rank = jnp.zeros_like(cs[u])
        for w in range(per):
            if w == u:
                continue
            before = (cs[w] >= cs[u]) if w < u else (cs[w] > cs[u])
            rank = rank + jnp.where(before, 1.0, 0.0)
        chosen.append(jnp.where(rank < EXPERT_TOPK, 1.0, 0.0))
    denom = jnp.zeros_like(cp[0])
    for u in range(per):
        denom = denom + chosen[u] * cp[u]
    tm = bg.shape[1]
    grows = [chosen[u] * cp[u] / denom for u in range(per)]
    gpad = jnp.concatenate(grows + [jnp.zeros((LANES - per, tm), F32)], axis=0)
    hx_ref[:, d_model:d_model + LANES] = gpad.T
    bg_ref[...] = bg
    cnt_ref[0] = jnp.concatenate(
        [jnp.broadcast_to(jnp.sum(jnp.where(bg == gi, 1.0, 0.0), axis=1, keepdims=True), (1, LANES))
         for gi in range(N_EXPERT_GROUPS)], axis=0)


def _route_kernel(bg_ref, off_ref, tri_ref, dest_ref):
    bg = bg_ref[...]
    tm = bg.shape[1]
    member = [jnp.where(bg == gi, 1.0, 0.0) for gi in range(N_EXPERT_GROUPS)]
    pad = jnp.zeros((8 - N_EXPERT_GROUPS, tm), F32)
    before = jnp.dot(jnp.concatenate(member + [pad], axis=0).astype(BF16), tri_ref[...],
                     preferred_element_type=F32)
    off = off_ref[0]
    dest = jnp.zeros((1, tm), F32)
    for gi in range(N_EXPERT_GROUPS):
        start = jnp.concatenate([off[gi:gi + 1, :]] * (tm // LANES), axis=1)
        dest = dest + member[gi] * (start + before[gi:gi + 1, :])
    dest_ref[...] = dest.astype(jnp.int32)


def _dispatch_kernel(dest_ref, hx_ref, init_ref, xs_ref, stage, sems):
    del init_ref
    i = pl.program_id(0)
    n = hx_ref.shape[0]
    slot = i % 2

    def wait_slot(s):
        def wait(r, carry):
            pltpu.make_async_copy(stage.at[s, pl.ds(0, 1)], xs_ref.at[pl.ds(0, 1)], sems.at[s]).wait()
            return carry
        lax.fori_loop(0, n, wait, 0, unroll=8)

    @pl.when(i >= 2)
    def _():
        wait_slot(slot)

    stage[slot] = hx_ref[...]

    def start(j, carry):
        for k in range(8):
            r = j * 8 + k
            pltpu.make_async_copy(stage.at[slot, pl.ds(r, 1)], xs_ref.at[pl.ds(dest_ref[0, 0, r], 1)],
                                  sems.at[slot]).start(priority=k % 2)
        return carry

    lax.fori_loop(0, n // 8, start, 0)

    @pl.when(i == pl.num_programs(0) - 1)
    def _():
        @pl.when(i >= 1)
        def _():
            wait_slot(1 - slot)
        wait_slot(slot)


def _moe_kernel(grp_ref, used_ref, xs_ref, wg_ref, wu_ref, wd_ref, y_ref, acc_ref):
    del grp_ref
    u = pl.program_id(1)
    d_model = y_ref.shape[1]

    @pl.when(u == 0)
    def _():
        acc_ref[...] = jnp.zeros_like(acc_ref)

    @pl.when(pl.program_id(0) < used_ref[0])
    def _():
        h = xs_ref[:, 0:d_model].astype(BF16)
        gates = xs_ref[:, d_model:d_model + LANES]
        lane = lax.broadcasted_iota(jnp.int32, gates.shape, 1)
        gcol = jnp.sum(jnp.where(lane == u, gates, 0.0), axis=-1, keepdims=True)
        hg = jnp.dot(h, wg_ref[0].astype(BF16), preferred_element_type=F32)
        hu = jnp.dot(h, wu_ref[0].astype(BF16), preferred_element_type=F32)
        a = (hg * jax.nn.sigmoid(hg)) * hu
        acc_ref[...] += _dot(gcol * a, wd_ref[0])

    @pl.when(u == pl.num_programs(1) - 1)
    def _():
        y_ref[...] = acc_ref[...]


def _combine_kernel(cur_ref, nxt_ref, y_ref, xn_ref, mod_ref, o_ref, buf, sems):
    i = pl.program_id(0)
    n = xn_ref.shape[0]

    def request(idx_ref, slot):
        def start(j, carry):
            for k in range(8):
                r = j * 8 + k
                pltpu.make_async_copy(y_ref.at[pl.ds(idx_ref[0, 0, r], 1)], buf.at[slot, pl.ds(r, 1)],
                                      sems.at[slot]).start(priority=k % 2)
            return carry
        lax.fori_loop(0, n // 8, start, 0)

    @pl.when(i == 0)
    def _():
        request(cur_ref, 0)

    @pl.when(i < pl.num_programs(0) - 1)
    def _():
        request(nxt_ref, (i + 1) % 2)

    slot = i % 2

    def wait(r, carry):
        pltpu.make_async_copy(y_ref.at[pl.ds(0, 1)], buf.at[slot, pl.ds(0, 1)], sems.at[slot]).wait()
        return carry

    lax.fori_loop(0, n, wait, 0, unroll=8)
    o_ref[...] = xn_ref[...] + mod_ref[0][5:6] * buf[slot]


def _decay_tables():
    H, C, d = RET_HEADS, CHUNK, HEAD_DIM
    log_gamma = jnp.log1p(-jnp.power(2.0, -5.0 - jnp.arange(H, dtype=F32)))
    idx = jnp.arange(C, dtype=F32)
    diff = idx[:, None] - idx[None, :]
    d_in = jnp.where(diff >= 0, jnp.exp(jnp.maximum(diff, 0.0)[None] * log_gamma[:, None, None]), 0.0).astype(F32)
    d_q = jnp.exp((idx + 1.0)[:, None] * log_gamma[None]).astype(F32)
    d_k = jnp.exp((C - 1.0 - idx)[:, None] * log_gamma[None]).astype(F32)
    d_c = jnp.exp(C * log_gamma).astype(F32)
    bcast = lambda a: jnp.broadcast_to(a.T[:, :, None], (H, C, d))
    return d_in, bcast(d_q), bcast(d_k), jnp.broadcast_to(d_c[:, None, None], (H, d, d))


def _gate_expand():
    m = np.zeros((LANES, N_BRANCH * NSA_W), np.float32)
    for h in range(NSA_HEADS):
        for br in range(N_BRANCH):
            m[h * N_BRANCH + br, br * NSA_W + h * HEAD_DIM: br * NSA_W + (h + 1) * HEAD_DIM] = 1.0
    return jnp.asarray(m, BF16)


def _block_expand(seq):
    m = (np.arange(LANES)[:, None] == (np.arange(seq) // SEL_BLOCK)[None, :]).astype(np.float32)
    return jnp.asarray(m, BF16)


def _compress_weights(pe, w1, w2):
    G, d = NSA_KV_HEADS, HEAD_DIM
    hid = w1.shape[-1]
    eye = jnp.eye(G, dtype=F32)
    w1h = w1.reshape(2, CMP_STRIDE, d, hid)
    w1b = jnp.einsum('pldh,ge->plgdeh', w1h, eye).reshape(2, CMP_STRIDE, G * d, G * hid)
    w2b = jnp.einsum('hd,ge->ghed', w2, eye).reshape(G * hid, G * d)
    peb = jnp.broadcast_to(pe.reshape(2, CMP_STRIDE, 1, d), (2, CMP_STRIDE, G, d)).reshape(2, CMP_STRIDE, G * d)
    return peb, w1b.astype(BF16), w2b.astype(BF16)


def kernel(x, c, positions, ada_w, ada_b, norm_mix_g, norm_ffn_g, w_in, w_out, ret_norm_g, gmlp_ln_g, gmlp_ws,
           gmlp_b, nsa_q_norm_g, nsa_k_norm_g, cmp_pe_k, cmp_pe_v, cmp_w1_k, cmp_w2_k, cmp_w1_v, cmp_w2_v,
           router_w, router_b, moe_w_gate, moe_w_up, moe_w_down):
    B, S, D = x.shape
    L = ada_w.shape[0]
    E = router_w.shape[1]
    DE = moe_w_gate.shape[-1]
    T = B * S
    n_chunk = S // CHUNK
    n_cmp = S // CMP_STRIDE
    n_sel = S // SEL_BLOCK
    top_n = min(SEL_TOPK, n_sel)
    n_qt = S // Q_TILE
    tm = min(512, S)
    tm_moe = min(1024, S)
    assert S % KEY_CHUNK == 0 and S >= WINDOW + Q_TILE and D == RET_W + GMLP_W + NSA_W
    assert w_in.shape[-1] == N_IN and T % tm_moe == 0 and n_sel <= LANES and NSA_KV_HEADS == 2

    half = HEAD_DIM // 2
    inv_freq = jnp.power(ROPE_THETA, -jnp.arange(half, dtype=F32) / half)
    invf = jnp.tile(inv_freq, LANES // half)[None, :]
    cos_t, sin_t = pl.pallas_call(
        _rope_table_kernel, name="rope_tables", grid=(B,),
        in_specs=[pl.BlockSpec((1, S, 1), lambda b: (b, 0, 0)), pl.BlockSpec((1, LANES), lambda b: (0, 0))],
        out_specs=[pl.BlockSpec((1, S, LANES), lambda b: (b, 0, 0))] * 2,
        out_shape=[jax.ShapeDtypeStruct((B, S, LANES), F32)] * 2,
        compiler_params=_cparams("parallel"),
    )(positions[:, :, None], invf)
    pad_c = lambda a: jnp.pad(a[:, CMP_LEN - 1::CMP_STRIDE], ((0, 0), (0, 1), (0, 0)))
    cos_c, sin_c = pad_c(cos_t), pad_c(sin_t)
    cos_f, sin_f = cos_t.reshape(T, LANES), sin_t.reshape(T, LANES)

    mod = pl.pallas_call(
        _ada_kernel, name="ada_mod", grid=(L, 6),
        in_specs=[pl.BlockSpec((B, D), lambda l, j: (0, 0)),
                  pl.BlockSpec((1, D, D), lambda l, j: (l, 0, j)),
                  pl.BlockSpec((1, 1, 1, D), lambda l, j: (l, j, 0, 0))],
        out_specs=pl.BlockSpec((1, 1, B, D), lambda l, j: (l, j, 0, 0)),
        out_shape=jax.ShapeDtypeStruct((L, 6, B, D), F32),
        compiler_params=_cparams("parallel", "parallel"),
    )(c, ada_w, ada_b.reshape(L, 6, 1, D))
    mod = mod.transpose(0, 2, 1, 3)

    d_in, d_q, d_k, d_c = _decay_tables()
    gexp = _gate_expand()
    expand = _block_expand(S)
    rw_hi = router_w.T.astype(BF16)
    rw_lo = (router_w.T - rw_hi.astype(F32)).astype(BF16)
    rwt = jnp.stack([rw_hi, rw_lo])
    rb = router_b.reshape(E, 1)
    tri = jnp.asarray(np.triu(np.ones((tm, tm), np.float32), k=1), BF16)
    x2 = x.reshape(T, D)
    tiles_per_batch = S // tm
    row_spec = lambda w: pl.BlockSpec((tm, w), lambda i: (i, 0))
    full = lambda shape: pl.BlockSpec(shape, lambda *_: (0,) * len(shape))

    for l in range(L):
        mod_l = mod[l]
        w_in_p = jnp.pad(w_in[l], ((0, 0), (0, N_IN_PAD - N_IN))).astype(BF16)
        raw_widths = tuple(hi - lo for lo, hi in _IN_RAW)
        mod_spec = pl.BlockSpec((1, 6, D), lambda i: (i // tiles_per_batch, 0, 0))
        kt_out = pl.BlockSpec((1, KV_W, tm), lambda i: (i // tiles_per_batch, 0, i % tiles_per_batch))
        kgain = jnp.tile(nsa_k_norm_g[l], (1, NSA_KV_HEADS))
        zr, zg, zkc, zvc, zgl, q_r, ks_t, vs_b, kw_t, vw_b = pl.pallas_call(
            _in_proj_kernel, name="in_proj", grid=(T // tm,),
            in_specs=[row_spec(D), mod_spec, full((1, D)), full((D, N_IN_PAD)), row_spec(LANES), row_spec(LANES),
                      full((1, NSA_W)), full((N_BRANCH, KV_W))],
            out_specs=[row_spec(w) for w in raw_widths]
            + [row_spec(NSA_W), kt_out, row_spec(KV_W), kt_out, row_spec(KV_W)],
            out_shape=[jax.ShapeDtypeStruct((T, w), F32) for w in raw_widths]
            + [jax.ShapeDtypeStruct((T, NSA_W), BF16), jax.ShapeDtypeStruct((B, KV_W, S), BF16),
               jax.ShapeDtypeStruct((T, KV_W), BF16), jax.ShapeDtypeStruct((B, KV_W, S), BF16),
               jax.ShapeDtypeStruct((T, KV_W), BF16)],
            compiler_params=_cparams("parallel"),
        )(x2, mod_l, norm_mix_g[l][None, :], w_in_p, cos_f, sin_f,
          jnp.tile(nsa_q_norm_g[l], NSA_HEADS)[None, :], kgain)

        crow = lambda w: pl.BlockSpec((MIX_BATCH, CHUNK, w), lambda b, cc: (b, cc, 0))
        y_ret, y_gm = pl.pallas_call(
            _mixer_kernel, name="ret_gmlp", grid=(B // MIX_BATCH, n_chunk),
            in_specs=[crow(4 * RET_W), crow(2 * GMLP_W), crow(LANES), crow(LANES),
                      full((RET_HEADS, CHUNK, CHUNK)), full((RET_HEADS, CHUNK, HEAD_DIM)),
                      full((RET_HEADS, CHUNK, HEAD_DIM)), full((RET_HEADS, HEAD_DIM, HEAD_DIM)),
                      full((RET_HEADS, 1, HEAD_DIM)), full((1, GMLP_W)),
                      full((GMLP_GROUPS, CHUNK, CHUNK)), full((GMLP_GROUPS, CHUNK, HEAD_DIM))],
            out_specs=[crow(RET_W), crow(GMLP_W)],
            out_shape=[jax.ShapeDtypeStruct((B, S, RET_W), BF16), jax.ShapeDtypeStruct((B, S, GMLP_W), BF16)],
            scratch_shapes=[pltpu.VMEM((MIX_BATCH, RET_HEADS, HEAD_DIM, HEAD_DIM), F32)],
            compiler_params=_cparams("parallel", "arbitrary"),
        )(zr.reshape(B, S, 4 * RET_W), zg.reshape(B, S, 2 * GMLP_W), cos_t, sin_t, d_in, d_q, d_k, d_c,
          ret_norm_g[l][:, None, :], gmlp_ln_g[l][None, :], gmlp_ws[l],
          jnp.broadcast_to(gmlp_b[l][:, :, None], (GMLP_GROUPS, CHUNK, HEAD_DIM)))
        y_ret, y_gm = y_ret.reshape(T, RET_W), y_gm.reshape(T, GMLP_W)

        pek, w1k, w2k = _compress_weights(cmp_pe_k[l], cmp_w1_k[l], cmp_w2_k[l])
        pev, w1v, w2v = _compress_weights(cmp_pe_v[l], cmp_w1_v[l], cmp_w2_v[l])
        cblk = pl.BlockSpec((S, KV_W), lambda b: (b, 0))
        ghid = w1k.shape[-1]
        ctab = pl.BlockSpec((1, n_cmp, LANES), lambda b: (b, 0, 0))
        kc, vc = pl.pallas_call(
            _compress_kernel, name="nsa_compress", grid=(B,),
            in_specs=[cblk, cblk, full((2, CMP_STRIDE, KV_W)), full((2, CMP_STRIDE, KV_W)),
                      full((2, CMP_STRIDE, KV_W, ghid)), full((ghid, KV_W)),
                      full((2, CMP_STRIDE, KV_W, ghid)), full((ghid, KV_W)),
                      full((N_BRANCH, KV_W)), ctab, ctab],
            out_specs=[ctab, ctab],
            out_shape=[jax.ShapeDtypeStruct((B, n_cmp, KV_W), F32)] * 2,
            compiler_params=_cparams("parallel"),
        )(zkc, zvc, pek, pev, w1k, w2k, w1v, w2v,
          kgain, cos_c, sin_c)

        qrow = lambda w: pl.BlockSpec((Q_TILE, w), lambda b, i: (b * n_qt + i, 0))
        kt_in = pl.BlockSpec((1, KV_W, S), lambda b, i: (b, 0, 0))
        v_in = pl.BlockSpec((S, KV_W), lambda b, i: (b, 0))
        ctab2 = pl.BlockSpec((1, n_cmp, KV_W), lambda b, i: (b, 0, 0))
        y_nsa = pl.pallas_call(
            functools.partial(_nsa_kernel, n_sel=n_sel, top_n=top_n), name="nsa_attn", grid=(B, n_qt),
            in_specs=[qrow(NSA_W), qrow(LANES), ctab2, ctab2, kt_in, v_in, kt_in, v_in,
                      full((LANES, N_BRANCH * NSA_W)), full((LANES, S))],
            out_specs=qrow(NSA_W),
            out_shape=jax.ShapeDtypeStruct((T, NSA_W), BF16),
            scratch_shapes=[
                pltpu.VMEM((NSA_KV_HEADS, NSA_GROUP * Q_TILE, WINDOW + Q_TILE), F32),
                pltpu.VMEM((NSA_KV_HEADS, NSA_GROUP * Q_TILE, WINDOW + Q_TILE), BF16),
                pltpu.VMEM((NSA_KV_HEADS, NSA_GROUP * Q_TILE, LANES), F32),
                pltpu.VMEM((NSA_KV_HEADS, NSA_GROUP * Q_TILE, KV_W), F32)],
            compiler_params=_cparams("parallel", "parallel"),
        )(q_r, zgl, kc, vc, ks_t, vs_b, kw_t, vw_b, gexp, expand)

        n_tok_tiles = T // tm
        DX = D + LANES
        x_new, hx, bgrp, cnt = pl.pallas_call(
            functools.partial(_out_proj_kernel, n_experts=E), name="out_proj_router", grid=(n_tok_tiles,),
            in_specs=[row_spec(RET_W), row_spec(GMLP_W), row_spec(NSA_W), row_spec(D),
                      pl.BlockSpec((1, 6, D), lambda i: (i // tiles_per_batch, 0, 0)),
                      full((D, D)), full((1, D)), full((2, E, D)), full((E, 1))],
            out_specs=[row_spec(D), row_spec(DX), pl.BlockSpec((1, tm), lambda i: (0, i)),
                       pl.BlockSpec((1, N_EXPERT_GROUPS, LANES), lambda i: (i, 0, 0))],
            out_shape=[jax.ShapeDtypeStruct((T, D), F32), jax.ShapeDtypeStruct((T, DX), F32),
                       jax.ShapeDtypeStruct((1, T), jnp.int32),
                       jax.ShapeDtypeStruct((n_tok_tiles, N_EXPERT_GROUPS, LANES), F32)],
            compiler_params=_cparams("parallel"),
        )(y_ret, y_gm, y_nsa, x2, mod_l, w_out[l].astype(BF16), norm_ffn_g[l][None, :], rwt, rb)

        tile_cnt = cnt[:, :, 0].astype(jnp.int32)
        seg = ((jnp.sum(tile_cnt, axis=0) + tm_moe - 1) // tm_moe) * tm_moe
        earlier = lambda n: jnp.arange(n)[:, None] > jnp.arange(n)[None, :]
        seg_start = jnp.sum(jnp.where(earlier(N_EXPERT_GROUPS), seg[None, :], 0), axis=1)
        tile_off = seg_start[None, :] + jnp.sum(
            jnp.where(earlier(n_tok_tiles)[:, :, None], tile_cnt[None, :, :], 0), axis=1)
        n_row_tiles = T // tm_moe + N_EXPERT_GROUPS
        row0 = jnp.arange(n_row_tiles) * tm_moe
        tile_grp = jnp.minimum(jnp.sum(((seg_start + seg)[None, :] <= row0[:, None]).astype(jnp.int32), axis=1),
                               N_EXPERT_GROUPS - 1)
        dest = pl.pallas_call(
            _route_kernel, name="moe_route", grid=(n_tok_tiles,),
            in_specs=[pl.BlockSpec((1, tm), lambda i: (0, i)),
                      pl.BlockSpec((1, N_EXPERT_GROUPS, LANES), lambda i: (i, 0, 0)), full((tm, tm))],
            out_specs=pl.BlockSpec((1, tm), lambda i: (0, i)),
            out_shape=jax.ShapeDtypeStruct((1, T), jnp.int32),
            compiler_params=_cparams("parallel"),
        )(bgrp, jnp.broadcast_to(tile_off.astype(F32)[:, :, None], (n_tok_tiles, N_EXPERT_GROUPS, LANES)), tri)
        dest3 = dest.reshape(n_tok_tiles, 1, tm)
        dest_spec = pl.BlockSpec((1, 1, tm), lambda i: (i, 0, 0), memory_space=pltpu.SMEM)
        any_spec = pl.BlockSpec(memory_space=pl.ANY)

        n_rows = n_row_tiles * tm_moe
        xs = pl.pallas_call(
            _dispatch_kernel, name="moe_dispatch", grid=(n_tok_tiles,),
            in_specs=[dest_spec, row_spec(DX), any_spec],
            out_specs=any_spec,
            out_shape=jax.ShapeDtypeStruct((n_rows, DX), F32),
            scratch_shapes=[pltpu.VMEM((2, tm, DX), F32), pltpu.SemaphoreType.DMA((2,))],
            input_output_aliases={2: 0},
            compiler_params=_cparams("arbitrary"),
        )(dest3, hx, jnp.zeros((n_rows, DX), F32) if l == 0 else xs)

        per = E // N_EXPERT_GROUPS
        tiles_used = (jnp.sum(seg) // tm_moe).astype(jnp.int32).reshape(1)

        def w_idx(k, u, grp, used):
            return l, grp[k] * per + jnp.where(k < used[0], u, per - 1), 0, 0

        ys = pl.pallas_call(
            _moe_kernel, name="moe",
            grid_spec=pltpu.PrefetchScalarGridSpec(
                num_scalar_prefetch=2, grid=(n_row_tiles, per),
                in_specs=[pl.BlockSpec((tm_moe, DX), lambda k, u, grp, used: (k, 0)),
                          pl.BlockSpec((None, 1, D, DE), w_idx), pl.BlockSpec((None, 1, D, DE), w_idx),
                          pl.BlockSpec((None, 1, DE, D), w_idx)],
                out_specs=pl.BlockSpec((tm_moe, D), lambda k, u, grp, used: (k, 0)),
                scratch_shapes=[pltpu.VMEM((tm_moe, D), F32)]),
            out_shape=jax.ShapeDtypeStruct((n_rows, D), F32),
            compiler_params=_cparams("parallel", "arbitrary"),
        )(tile_grp, tiles_used, xs, moe_w_gate, moe_w_up, moe_w_down)

        dest_next = pl.BlockSpec((1, 1, tm), lambda i: (jnp.minimum(i + 1, n_tok_tiles - 1), 0, 0),
                                 memory_space=pltpu.SMEM)
        x2 = pl.pallas_call(
            _combine_kernel, name="moe_combine", grid=(n_tok_tiles,),
            in_specs=[dest_spec, dest_next, any_spec, row_spec(D),
                      pl.BlockSpec((1, 6, D), lambda i: (i // tiles_per_batch, 0, 0))],
            out_specs=row_spec(D),
            out_shape=jax.ShapeDtypeStruct((T, D), F32),
            scratch_shapes=[pltpu.VMEM((2, tm, D), F32), pltpu.SemaphoreType.DMA((2,))],
            compiler_params=_cparams("arbitrary"),
        )(dest3, dest3, ys, x_new, mod_l)

    return x2.reshape(B, S, D)
```
